```python
import math
import jax, jax.numpy as jnp
from jax import lax
import numpy as np

D_MODEL = 2048
BATCH = 4
SEQ = 2048
DEPTH = 2

HEAD_DIM = 128
ATTN_W = D_MODEL // 2
ATTN_HEADS = ATTN_W // HEAD_DIM
KV_HEADS = 2
KV_W = KV_HEADS * HEAD_DIM
IDX_HEADS = 8
IDX_DIM = 64
MAX_TOPK = 256
Q_BLOCK = 128
SSM_W = D_MODEL // 4
SSM_GROUP_CH = 16
SSM_GROUPS = SSM_W // SSM_GROUP_CH
SSM_STATE = 64
POOL_W = D_MODEL // 4
POOL_WINDOWS = (2, 4, 8, 16)
POOL_GROUPS = len(POOL_WINDOWS)
POOL_GROUP_CH = POOL_W // POOL_GROUPS
MIX_W = ATTN_W + SSM_W + POOL_W
IN_SIZES = (ATTN_W, KV_W, KV_W, IDX_HEADS * IDX_DIM, IDX_DIM, IDX_HEADS, SSM_W, POOL_W)
IN_W = sum(IN_SIZES)
ROPE_THETA = 500000.0
ROPE_FRAC = 4
LN_EPS = 1e-5
DEEPNORM_ALPHA = (2.0 * DEPTH) ** 0.25
DEEPNORM_BETA = (8.0 * DEPTH) ** -0.25
N_EXPERTS = 16
N_EXPERT_GROUPS = 4
EXPERTS_PER_GROUP = N_EXPERTS // N_EXPERT_GROUPS
TOP_K_EXPERTS = 2
D_FF_EXPERT = 1024

kernel_name = "hybrid_dsa_s5_pool_moe_deepnorm"


def layer_norm(x, g, b):
    xf = x.astype(jnp.float32)
    mu = jnp.mean(xf, axis=-1, keepdims=True)
    var = jnp.mean(jnp.square(xf - mu), axis=-1, keepdims=True)
    return ((xf - mu) * lax.rsqrt(var + LN_EPS) * g + b).astype(x.dtype)


def partial_rope(x, positions):
    d = x.shape[-1]
    rot = d // ROPE_FRAC
    half = rot // 2
    inv_freq = ROPE_THETA ** (-jnp.arange(half, dtype=jnp.float32) * 2.0 / rot)
    ang = positions.astype(jnp.float32)[..., None] * inv_freq
    cos = jnp.cos(ang)[:, :, None, :]
    sin = jnp.sin(ang)[:, :, None, :]
    xr = x[..., :rot].astype(jnp.float32)
    x1, x2 = xr[..., :half], xr[..., half:]
    rotated = jnp.concatenate([x1 * cos - x2 * sin, x2 * cos + x1 * sin], axis=-1)
    return jnp.concatenate([rotated.astype(x.dtype), x[..., rot:]], axis=-1)


def dsa_attention(q, k, v, q_idx, k_idx, w_idx):
    bsz, seq = q.shape[0], q.shape[1]
    top_k = min(MAX_TOPK, seq // 4)
    n_blocks = seq // Q_BLOCK
    group = ATTN_HEADS // KV_HEADS
    key_pos = jnp.arange(seq, dtype=jnp.int32)
    t_blocks = key_pos.reshape(n_blocks, Q_BLOCK)
    k_idx_f = k_idx.astype(jnp.float32)

    def to_blocks(a):
        return a.reshape(bsz, n_blocks, Q_BLOCK, *a.shape[2:]).swapaxes(0, 1)

    def one_block(args):
        qb, qib, wb, tb = args
        rel = jax.nn.relu(jnp.einsum("bqhd,bsd->bqhs", qib.astype(jnp.float32), k_idx_f)) * (IDX_DIM ** -0.5)
        iscore = jnp.einsum("bqhs,bqh->bqs", rel, wb.astype(jnp.float32)) * (IDX_HEADS ** -0.5)
        causal = key_pos[None, :] <= tb[:, None]
        iscore = jnp.where(causal[None], iscore, -jnp.inf)
        _, sel = lax.top_k(iscore, top_k)
        valid = sel <= tb[None, :, None]
        k_sel = jax.vmap(lambda kk, ii: kk[ii])(k, sel)
        v_sel = jax.vmap(lambda vv, ii: vv[ii])(v, sel)
        qg = qb.reshape(bsz, Q_BLOCK, KV_HEADS, group, HEAD_DIM)
        s = jnp.einsum("bqkgd,bqnkd->bqkgn", qg, k_sel).astype(jnp.float32) * (HEAD_DIM ** -0.5)
        s = jnp.where(valid[:, :, None, None, :], s, -jnp.inf)
        p = jax.nn.softmax(s, axis=-1).astype(v.dtype)
        o = jnp.einsum("bqkgn,bqnkd->bqkgd", p, v_sel)
        return o.reshape(bsz, Q_BLOCK, ATTN_W)

    out = lax.map(one_block, (to_blocks(q), to_blocks(q_idx), to_blocks(w_idx), t_blocks))
    return out.swapaxes(0, 1).reshape(bsz, seq, ATTN_W)


def s5_mixer(u, lam_re, lam_im, log_step, b_re, b_im, c_re, c_im, d_skip, w_glu, b_glu):
    bsz, seq, _ = u.shape
    f32 = jnp.float32
    uf = u.astype(f32).reshape(bsz, seq, SSM_GROUPS, SSM_GROUP_CH)
    lam = lax.complex(lam_re.astype(f32), lam_im.astype(f32))
    step = jnp.exp(log_step.astype(f32))[:, None]
    lam_bar = jnp.exp(lam * step)
    b = lax.complex(b_re.astype(f32), b_im.astype(f32))
    b_bar = ((lam_bar - 1.0) / lam)[..., None] * b
    bu = jnp.einsum("blgc,gpc->blgp", uf.astype(jnp.complex64), b_bar)
    a = jnp.broadcast_to(lam_bar, bu.shape)

    def combine(left, right):
        a_l, b_l = left
        a_r, b_r = right
        return a_r * a_l, a_r * b_l + b_r

    _, states = lax.associative_scan(combine, (a, bu), axis=1)
    cmat = lax.complex(c_re.astype(f32), c_im.astype(f32))
    y = jnp.real(jnp.einsum("blgp,gcp->blgc", states, cmat))
    y = y + d_skip.astype(f32).reshape(SSM_GROUPS, SSM_GROUP_CH) * uf
    y = jax.nn.gelu(y.reshape(bsz, seq, SSM_W))
    y = y * jax.nn.sigmoid(y @ w_glu.astype(f32) + b_glu.astype(f32))
    return y.astype(u.dtype)


def pool_mixer(u, w_pool, pool_scale):
    bsz, seq, _ = u.shape
    uf = u.astype(jnp.float32).reshape(bsz, seq, POOL_GROUPS, POOL_GROUP_CH)
    cs = jnp.concatenate([jnp.zeros_like(uf[:, :1]), jnp.cumsum(uf, axis=1)], axis=1)
    t = jnp.arange(seq, dtype=jnp.int32)
    outs = []
    for g, win in enumerate(POOL_WINDOWS):
        start = jnp.maximum(t + 1 - win, 0)
        cg = cs[:, :, g]
        window_sum = cg[:, 1:] - cg[:, start]
        count = jnp.minimum(t + 1, win).astype(jnp.float32)
        outs.append(window_sum / count[None, :, None] - uf[:, :, g])
    pooled = jnp.stack(outs, axis=2)
    y = jnp.einsum("blgc,gcd->blgd", pooled, w_pool.astype(jnp.float32)).reshape(bsz, seq, POOL_W)
    return (y * pool_scale.astype(jnp.float32)).astype(u.dtype)


def grouped_moe(u, w_router, e_gate, e_up, e_down):
    bsz, seq, d = u.shape
    xt = u.reshape(-1, d)
    n_tok = xt.shape[0]
    probs = jax.nn.softmax(xt.astype(jnp.float32) @ w_router.astype(jnp.float32), axis=-1)
    grp = probs.reshape(n_tok, N_EXPERT_GROUPS, EXPERTS_PER_GROUP)
    grp_score = jnp.sum(lax.top_k(grp, TOP_K_EXPERTS)[0], axis=-1)
    best = jnp.argmax(grp_score, axis=-1)
    in_grp = jnp.take_along_axis(grp, best[:, None, None], axis=1)[:, 0]
    top_p, top_i = lax.top_k(in_grp, TOP_K_EXPERTS)
    top_p = top_p / jnp.sum(top_p, axis=-1, keepdims=True)
    expert_ids = best[:, None] * EXPERTS_PER_GROUP + top_i
    gates = jnp.sum(jax.nn.one_hot(expert_ids, N_EXPERTS, dtype=jnp.float32) * top_p[..., None], axis=1)
    out = jnp.zeros((n_tok, d), jnp.float32)
    for e in range(N_EXPERTS):
        h = jax.nn.silu(xt @ e_gate[e]) * (xt @ e_up[e])
        out = out + gates[:, e:e + 1] * (h @ e_down[e]).astype(jnp.float32)
    return out.astype(u.dtype).reshape(bsz, seq, d)


def setup_inputs(seed: int = 0) -> dict:
    key = jax.random.key(seed)
    ks = jax.random.split(key, 32)
    f32 = jnp.float32
    nrm = lambda k, shape, s: jax.random.normal(k, shape, f32) * s
    x = jax.random.normal(ks[0], (BATCH, SEQ, D_MODEL), f32)
    c = jax.random.normal(ks[1], (BATCH, D_MODEL), f32)
    offset = jax.random.randint(ks[2], (BATCH, 1), 0, 4096, dtype=jnp.int32)
    positions = offset + jnp.arange(SEQ, dtype=jnp.int32)[None, :]
    w_ada = nrm(ks[3], (DEPTH, D_MODEL, 6 * D_MODEL), 0.1 * D_MODEL ** -0.5)
    b_ada = nrm(ks[4], (DEPTH, 6 * D_MODEL), 0.02)
    w_in = nrm(ks[5], (DEPTH, D_MODEL, IN_W), D_MODEL ** -0.5)
    w_out = nrm(ks[6], (DEPTH, MIX_W, D_MODEL), DEEPNORM_BETA * MIX_W ** -0.5)
    n_idx = jnp.arange(SSM_STATE, dtype=f32)
    ssm_lam_re = -0.5 + nrm(ks[7], (DEPTH, SSM_GROUPS, SSM_STATE), 0.01)
    ssm_lam_im = math.pi * n_idx[None, None, :] + nrm(ks[8], (DEPTH, SSM_GROUPS, SSM_STATE), 0.01)
    ssm_log_step = jax.random.uniform(ks[9], (DEPTH, SSM_GROUPS), f32, math.log(1e-3), math.log(1e-1))
    bs = (0.5 / SSM_GROUP_CH) ** 0.5
    cs = (0.5 / SSM_STATE) ** 0.5
    ssm_b_re = nrm(ks[10], (DEPTH, SSM_GROUPS, SSM_STATE, SSM_GROUP_CH), bs)
    ssm_b_im = nrm(ks[11], (DEPTH, SSM_GROUPS, SSM_STATE, SSM_GROUP_CH), bs)
    ssm_c_re = nrm(ks[12], (DEPTH, SSM_GROUPS, SSM_GROUP_CH, SSM_STATE), cs)
    ssm_c_im = nrm(ks[13], (DEPTH, SSM_GROUPS, SSM_GROUP_CH, SSM_STATE), cs)
    ssm_d = nrm(ks[14], (DEPTH, SSM_W), 1.0)
    ssm_w_glu = nrm(ks[15], (DEPTH, SSM_W, SSM_W), SSM_W ** -0.5)
    ssm_b_glu = nrm(ks[16], (DEPTH, SSM_W), 0.02)
    pool_w = nrm(ks[17], (DEPTH, POOL_GROUPS, POOL_GROUP_CH, POOL_GROUP_CH), POOL_GROUP_CH ** -0.5)
    pool_scale = 1.0 + nrm(ks[18], (DEPTH, POOL_W), 0.02)
    ln1_g = 1.0 + nrm(ks[19], (DEPTH, D_MODEL), 0.02)
    ln1_b = nrm(ks[20], (DEPTH, D_MODEL), 0.02)
    ln2_g = 1.0 + nrm(ks[21], (DEPTH, D_MODEL), 0.02)
    ln2_b = nrm(ks[22], (DEPTH, D_MODEL), 0.02)
    w_router = nrm(ks[23], (D_MODEL, N_EXPERTS), D_MODEL ** -0.5)
    e_gate = nrm(ks[24], (DEPTH, N_EXPERTS, D_MODEL, D_FF_EXPERT), D_MODEL ** -0.5)
    e_up = nrm(ks[25], (DEPTH, N_EXPERTS, D_MODEL, D_FF_EXPERT), D_MODEL ** -0.5)
    e_down = nrm(ks[26], (DEPTH, N_EXPERTS, D_FF_EXPERT, D_MODEL), DEEPNORM_BETA * D_FF_EXPERT ** -0.5)
    return {"x": x, "c": c, "positions": positions, "w_ada": w_ada, "b_ada": b_ada,
            "w_in": w_in, "w_out": w_out, "ssm_lam_re": ssm_lam_re, "ssm_lam_im": ssm_lam_im,
            "ssm_log_step": ssm_log_step, "ssm_b_re": ssm_b_re, "ssm_b_im": ssm_b_im,
            "ssm_c_re": ssm_c_re, "ssm_c_im": ssm_c_im, "ssm_d": ssm_d, "ssm_w_glu": ssm_w_glu,
            "ssm_b_glu": ssm_b_glu, "pool_w": pool_w, "pool_scale": pool_scale,
            "ln1_g": ln1_g, "ln1_b": ln1_b, "ln2_g": ln2_g, "ln2_b": ln2_b,
            "w_router": w_router, "e_gate": e_gate, "e_up": e_up, "e_down": e_down}


def reference(x, c, positions, w_ada, b_ada, w_in, w_out, ssm_lam_re, ssm_lam_im, ssm_log_step,
              ssm_b_re, ssm_b_im, ssm_c_re, ssm_c_im, ssm_d, ssm_w_glu, ssm_b_glu, pool_w, pool_scale,
              ln1_g, ln1_b, ln2_g, ln2_b, w_router, e_gate, e_up, e_down):
    bsz, seq, _ = x.shape
    cond = jax.nn.silu(c)
    offs = [0]
    for s in IN_SIZES:
        offs.append(offs[-1] + s)
    for l in range(DEPTH):
        ada = cond @ w_ada[l] + b_ada[l]
        sh1, sc1, g1, sh2, sc2, g2 = [a[:, None, :] for a in jnp.split(ada, 6, axis=-1)]
        u = x * (1.0 + sc1) + sh1
        proj = u @ w_in[l]
        q, k, v, iq, ik, iw, us, up = [proj[..., offs[i]:offs[i + 1]] for i in range(len(IN_SIZES))]
        q = partial_rope(q.reshape(bsz, seq, ATTN_HEADS, HEAD_DIM), positions)
        k = partial_rope(k.reshape(bsz, seq, KV_HEADS, HEAD_DIM), positions)
        v = v.reshape(bsz, seq, KV_HEADS, HEAD_DIM)
        iq = partial_rope(iq.reshape(bsz, seq, IDX_HEADS, IDX_DIM), positions)
        ik = partial_rope(ik[:, :, None, :], positions)[:, :, 0, :]
        y_attn = dsa_attention(q, k, v, iq, ik, iw)
        y_ssm = s5_mixer(us, ssm_lam_re[l], ssm_lam_im[l], ssm_log_step[l], ssm_b_re[l], ssm_b_im[l],
                         ssm_c_re[l], ssm_c_im[l], ssm_d[l], ssm_w_glu[l], ssm_b_glu[l])
        y_pool = pool_mixer(up, pool_w[l], pool_scale[l])
        mix = jnp.concatenate([y_attn, y_ssm, y_pool], axis=-1) @ w_out[l]
        x = layer_norm(DEEPNORM_ALPHA * x + (1.0 + g1) * mix, ln1_g[l], ln1_b[l])
        u2 = x * (1.0 + sc2) + sh2
        ffn = grouped_moe(u2, w_router, e_gate[l], e_up[l], e_down[l])
        x = layer_norm(DEEPNORM_ALPHA * x + (1.0 + g2) * ffn, ln2_g[l], ln2_b[l])
    return x
```

```python
import functools
import math

import jax
import jax.numpy as jnp
from jax import lax
from jax.experimental import pallas as pl
from jax.experimental.pallas import tpu as pltpu

F32 = jnp.float32
BF16 = jnp.bfloat16

HEAD_DIM = 128
KV_HEADS = 2
IDX_HEADS = 8
IDX_DIM = 64
MAX_TOPK = 256
SSM_GROUP_CH = 16
SSM_STATE = 64
POOL_WINDOWS = (2, 4, 8, 16)
ROPE_THETA = 500000.0
ROPE_FRAC = 4
LN_EPS = 1e-5
N_EXPERTS = 16
N_EXPERT_GROUPS = 4
EXPERTS_PER_GROUP = N_EXPERTS // N_EXPERT_GROUPS

LANES = 128
VMEM_LIMIT = 56 * 1024 * 1024
ROW_TILE = 256
ATTN_Q_TILE = 128
ATTN_KEY_STEP = 512
SSM_CHUNK = 32
EXPERT_TILE = 256
COPY_ROWS = 256
ADA_COLS = 1024


def _params(*sem):
    return pltpu.CompilerParams(dimension_semantics=sem, vmem_limit_bytes=VMEM_LIMIT)


def _sigmoid(x):
    return 1.0 / (1.0 + jnp.exp(-x))


def _ada_kernel(c_ref, w_ref, b_ref, o_ref):
    c = c_ref[...]
    cond = (c * _sigmoid(c)).astype(BF16)
    o_ref[0] = jnp.dot(cond, w_ref[0].astype(BF16), preferred_element_type=F32) + b_ref[0]


def _ada_all(c_pad, w_ada, b_ada):
    depth, d, n6 = w_ada.shape
    rows = c_pad.shape[0]
    return pl.pallas_call(
        _ada_kernel,
        grid=(depth, n6 // ADA_COLS),
        in_specs=[pl.BlockSpec((rows, d), lambda l, j: (0, 0)),
                  pl.BlockSpec((1, d, ADA_COLS), lambda l, j: (l, 0, j)),
                  pl.BlockSpec((1, 1, ADA_COLS), lambda l, j: (l, 0, j))],
        out_specs=pl.BlockSpec((1, rows, ADA_COLS), lambda l, j: (l, 0, j)),
        out_shape=jax.ShapeDtypeStruct((depth, rows, n6), F32),
        compiler_params=_params("arbitrary", "arbitrary"),
        name="ada",
    )(c_pad, w_ada, b_ada.reshape(depth, 1, n6))


def _rope(xv, tab, half):
    w = xv.shape[1]
    rep = w // LANES
    c = jnp.tile(tab[:, 0:LANES], (1, rep))
    s1 = jnp.tile(tab[:, LANES:2 * LANES], (1, rep))
    s2 = jnp.tile(tab[:, 2 * LANES:3 * LANES], (1, rep))
    return xv * c + pltpu.roll(xv, w - half, 1) * s1 + pltpu.roll(xv, half, 1) * s2


def _inproj_kernel(x_ref, sc_ref, sh_ref, w_ref, rq_ref, ri_ref,
                   q_ref, k_ref, v_ref, iq_ref, ik_ref, iw_ref, us_ref, up_ref, *, widths, idx_scale):
    aw, kw, iqw, sw, pw = widths
    u = (x_ref[0] * (1.0 + sc_ref[0]) + sh_ref[0]).astype(BF16)
    rq = rq_ref[0]
    ri = ri_ref[0]

    def mm(lo, n):
        return jnp.dot(u, w_ref[:, lo:lo + n], preferred_element_type=F32)

    qhalf = HEAD_DIM // ROPE_FRAC // 2
    ihalf = IDX_DIM // ROPE_FRAC // 2
    o = 0
    q_ref[0] = (_rope(mm(o, aw), rq, qhalf) * (HEAD_DIM ** -0.5)).astype(BF16)
    o += aw
    k_ref[0] = _rope(mm(o, kw), rq, qhalf).astype(BF16)
    o += kw
    v_ref[0] = mm(o, kw).astype(BF16)
    o += kw
    iq_ref[0] = _rope(mm(o, iqw), ri, ihalf).astype(BF16)
    o += iqw
    us_ref[0] = mm(o, sw)
    o += sw
    up_ref[0] = mm(o, pw)
    o += pw
    tail = mm(o, LANES)
    ik_ref[0] = _rope(tail, ri, ihalf)[:, 0:IDX_DIM].astype(BF16)
    iw_ref[0] = tail[:, IDX_DIM:IDX_DIM + IDX_HEADS] * idx_scale


def _inproj(x, sc, sh, w_r, rq, ri, widths):
    bsz, seq, d = x.shape
    aw, kw, iqw, sw, pw = widths
    tm = min(ROW_TILE, seq)
    tile = lambda n: pl.BlockSpec((1, tm, n), lambda b, i: (b, i, 0))
    vec = pl.BlockSpec((1, 1, d), lambda b, i: (b, 0, 0))
    out_w = (aw, kw, kw, iqw, IDX_DIM, IDX_HEADS, sw, pw)
    out_dt = (BF16, BF16, BF16, BF16, BF16, F32, F32, F32)
    return pl.pallas_call(
        functools.partial(_inproj_kernel, widths=widths, idx_scale=(IDX_DIM ** -0.5) * (IDX_HEADS ** -0.5)),
        grid=(bsz, seq // tm),
        in_specs=[tile(d), vec, vec, pl.BlockSpec(w_r.shape, lambda b, i: (0, 0)),
                  tile(3 * LANES), tile(3 * LANES)],
        out_specs=[tile(n) for n in out_w],
        out_shape=[jax.ShapeDtypeStruct((bsz, seq, n), dt) for n, dt in zip(out_w, out_dt)],
        compiler_params=_params("parallel", "arbitrary"),
        name="inproj",
    )(x, sc, sh, w_r, rq, ri)


def _attn_kernel(q_ref, iq_ref, iw_ref, k_ref, v_ref, ik_ref, o_ref, mask_ref, *, q0, tq, keys, top_k):
    i = pl.program_id(1)
    t = q0 + i * tq + lax.broadcasted_iota(jnp.int32, (tq, 1), 0)
    kpos = lax.broadcasted_iota(jnp.int32, (1, keys), 1)
    causal = kpos <= t

    iq = iq_ref[0]
    ik = ik_ref[0]
    iw = iw_ref[0]
    isc = jnp.zeros((tq, keys), F32)
    for h in range(IDX_HEADS):
        r = lax.dot_general(iq[:, h * IDX_DIM:(h + 1) * IDX_DIM], ik, (((1,), (1,)), ((), ())),
                            preferred_element_type=F32)
        isc = isc + jnp.maximum(r, 0.0) * iw[:, h:h + 1]
    x = jnp.where(causal, isc, -jnp.inf)

    kf = float(top_k)
    n_valid = (t + 1).astype(F32)
    need = n_valid > kf

    def count_ge(theta):
        return jnp.sum(jnp.where(x >= theta, 1.0, 0.0), axis=1, keepdims=True)

    rmax = jnp.max(x, axis=1, keepdims=True)
    rmin = jnp.min(jnp.where(causal, isc, jnp.inf), axis=1, keepdims=True)
    c_max = count_ge(rmax)
    top_tie = c_max >= kf
    lo0 = jnp.where(need, jnp.where(top_tie, rmax, rmin), -jnp.inf)
    clo0 = jnp.where(top_tie, c_max, n_valid)
    done0 = jnp.where(need & jnp.logical_not(top_tie) & (clo0 != kf), 0.0, 1.0)

    def cond(s):
        return s[0] > 0

    def body(s):
        _, lo, hi, clo, done = s
        mid = 0.5 * lo + 0.5 * hi
        stuck = (mid <= lo) | (mid >= hi)
        c = count_ge(mid)
        upd = (done == 0.0) & jnp.logical_not(stuck)
        ge = c >= kf
        lo = jnp.where(upd & ge, mid, lo)
        clo = jnp.where(upd & ge, c, clo)
        hi = jnp.where(upd & jnp.logical_not(ge), mid, hi)
        done = jnp.where(stuck | (clo == kf), 1.0, done)
        active = jnp.sum(1.0 - done).astype(jnp.int32)
        return active, lo, hi, clo, done

    active0 = jnp.sum(1.0 - done0).astype(jnp.int32)
    _, lo, _, clo, _ = lax.while_loop(cond, body, (active0, lo0, rmax, clo0, done0))

    mask_ref[...] = jnp.where(causal & (x >= lo), 0.0, -1e30)

    tie_rows = jnp.sum(jnp.where(need & (clo > kf), 1.0, 0.0)).astype(jnp.int32)

    @pl.when(tie_rows > 0)
    def _():
        room = kf - jnp.sum(jnp.where(x > lo, 1.0, 0.0), axis=1, keepdims=True)
        step = 256
        rr = lax.broadcasted_iota(jnp.int32, (step, step), 0)
        cc = lax.broadcasted_iota(jnp.int32, (step, step), 1)
        before = jnp.where(rr < cc, 1.0, 0.0).astype(BF16)
        carry = jnp.zeros((tq, 1), F32)
        for c0 in range(0, keys, step):
            xs = x[:, c0:c0 + step]
            e = jnp.where(xs == lo, 1.0, 0.0)
            rank = carry + jnp.dot(e.astype(BF16), before, preferred_element_type=F32)
            keep = (xs > lo) | ((xs == lo) & (rank < room))
            mask_ref[:, c0:c0 + step] = jnp.where(keep & (xs > -jnp.inf), 0.0, -1e30)
            carry = carry + jnp.sum(e, axis=1, keepdims=True)

    bias = mask_ref[...]
    q = q_ref[0]
    group = q.shape[1] // HEAD_DIM // KV_HEADS
    bias_g = jnp.concatenate([bias] * group, axis=0)
    for g in range(KV_HEADS):
        qg = jnp.concatenate([q[:, (g * group + j) * HEAD_DIM:(g * group + j + 1) * HEAD_DIM]
                              for j in range(group)], axis=0)
        kg = k_ref[0, :, g * HEAD_DIM:(g + 1) * HEAD_DIM]
        vg = v_ref[0, :, g * HEAD_DIM:(g + 1) * HEAD_DIM]
        s = lax.dot_general(qg, kg, (((1,), (1,)), ((), ())), preferred_element_type=F32) + bias_g
        m = jnp.max(s, axis=1, keepdims=True)
        p = jnp.exp(s - m)
        den = jnp.sum(p, axis=1, keepdims=True)
        og = jnp.dot(p.astype(BF16), vg, preferred_element_type=F32) / den
        for j in range(group):
            h = g * group + j
            o_ref[0, :, h * HEAD_DIM:(h + 1) * HEAD_DIM] = og[j * tq:(j + 1) * tq].astype(BF16)


def _attention(q, k, v, iq, ik, iw, top_k):
    bsz, seq, aw = q.shape
    tq = min(ATTN_Q_TILE, seq)
    step = min(ATTN_KEY_STEP, seq)
    outs = []
    for q0 in range(0, seq, step):
        keys = q0 + step
        nq = step // tq
        qtile = lambda n, q0=q0: pl.BlockSpec((1, tq, n), lambda b, i: (b, q0 // tq + i, 0))
        ktile = lambda n, keys=keys: pl.BlockSpec((1, keys, n), lambda b, i: (b, 0, 0))
        outs.append(pl.pallas_call(
            functools.partial(_attn_kernel, q0=q0, tq=tq, keys=keys, top_k=top_k),
            grid=(bsz, nq),
            in_specs=[qtile(aw), qtile(iq.shape[2]), qtile(iw.shape[2]),
                      ktile(k.shape[2]), ktile(v.shape[2]), ktile(ik.shape[2])],
            out_specs=pl.BlockSpec((1, tq, aw), lambda b, i: (b, i, 0)),
            out_shape=jax.ShapeDtypeStruct((bsz, step, aw), BF16),
            scratch_shapes=[pltpu.VMEM((tq, keys), F32)],
            compiler_params=_params("parallel", "arbitrary"),
            name=f"attn_k{keys}",
        )(q, iq, iw, k, v, ik))
    return jnp.concatenate(outs, axis=1)


def _ssm_kernel(u_ref, kin_ref, wre_ref, wim_ref, vre_ref, vim_ref, are_ref, aim_ref, y_ref,
                sre_ref, sim_ref, xre_ref, xim_ref, *, bsz, n_chunks):
    pair = u_ref.shape[0]
    p = wre_ref.shape[2]
    for j in range(pair):
        u = u_ref[j]
        sre_ref[:, j * p:(j + 1) * p] = jnp.dot(u, wre_ref[j], preferred_element_type=F32)
        sim_ref[:, j * p:(j + 1) * p] = jnp.dot(u, wim_ref[j], preferred_element_type=F32)
    a_re = are_ref[0]
    a_im = aim_ref[0]
    x_re = jnp.zeros((bsz, pair * p), F32)
    x_im = jnp.zeros((bsz, pair * p), F32)
    for c in range(n_chunks):
        rows = slice(c * bsz, (c + 1) * bsz)
        xre_ref[rows, :] = x_re
        xim_ref[rows, :] = x_im
        n_re = a_re * x_re - a_im * x_im + sre_ref[rows, :]
        n_im = a_re * x_im + a_im * x_re + sim_ref[rows, :]
        x_re, x_im = n_re, n_im
    for j in range(pair):
        xr = xre_ref[:, j * p:(j + 1) * p].astype(BF16)
        xi = xim_ref[:, j * p:(j + 1) * p].astype(BF16)
        y_ref[j] = (jnp.dot(u_ref[j], kin_ref[j], preferred_element_type=F32)
                    + jnp.dot(xr, vre_ref[j], preferred_element_type=F32)
                    + jnp.dot(xi, vim_ref[j], preferred_element_type=F32))


def _ssm_prepare(lam_re, lam_im, log_step, b_re, b_im, c_re, c_im, d_skip, chunk):
    g, p = lam_re.shape
    ch = b_re.shape[2]
    lam = lax.complex(lam_re, lam_im)
    step = jnp.exp(log_step)[:, None]
    ls = lam * step
    lam_bar = jnp.exp(ls)
    b_bar = ((lam_bar - 1.0) / lam)[..., None] * lax.complex(b_re, b_im)
    cm = lax.complex(c_re, c_im)
    n = jnp.arange(chunk + 1, dtype=F32)
    pw = jnp.exp(ls[..., None] * n)
    ker = jnp.real(jnp.einsum("gcp,gpt,gpd->gtcd", cm, pw[..., :chunk], b_bar))
    lag = jnp.arange(chunk)[None, :] - jnp.arange(chunk)[:, None]
    kexp = jnp.where((lag >= 0)[None, :, :, None, None], ker[:, jnp.clip(lag, 0, chunk - 1)], 0.0)
    kin = kexp.transpose(0, 1, 4, 2, 3)
    eye_t = jnp.eye(chunk, dtype=F32)
    eye_c = jnp.eye(ch, dtype=F32)
    skip = d_skip.reshape(g, ch)
    kin = kin + (eye_t[None, :, None, :, None] * eye_c[None, None, :, None, :] * skip[:, None, :, None, None])
    kin = kin.reshape(g, chunk * ch, chunk * ch)
    win = pw[:, :, chunk - 1 - jnp.arange(chunk)][..., None] * b_bar[:, :, None, :]
    win = win.transpose(0, 2, 3, 1).reshape(g, chunk * ch, p)
    vout = cm[:, :, :, None] * pw[:, None, :, 1:chunk + 1]
    vout = vout.transpose(0, 2, 3, 1).reshape(g, p, chunk * ch)
    a = pw[..., chunk]
    pair = 2
    a_re = jnp.real(a).reshape(g // pair, 1, pair * p)
    a_im = jnp.imag(a).reshape(g // pair, 1, pair * p)
    return (kin.astype(BF16), jnp.real(win).astype(BF16), jnp.imag(win).astype(BF16),
            jnp.real(vout).astype(BF16), (-jnp.imag(vout)).astype(BF16), a_re, a_im)


def _ssm(us, prep, chunk):
    bsz, seq, w = us.shape
    kin, wre, wim, vre, vim, a_re, a_im = prep
    g, p = wre.shape[0], wre.shape[2]
    ch = w // g
    nc = seq // chunk
    rows = nc * bsz
    cols = chunk * ch
    pair = 2
    u = us.reshape(bsz, nc, chunk, g, ch).transpose(3, 1, 0, 2, 4).reshape(g, rows, cols).astype(BF16)
    blk = lambda a, b: pl.BlockSpec((pair, a, b), lambda i: (i, 0, 0))
    y = pl.pallas_call(
        functools.partial(_ssm_kernel, bsz=bsz, n_chunks=nc),
        grid=(g // pair,),
        in_specs=[blk(rows, cols), blk(cols, cols), blk(cols, p), blk(cols, p), blk(p, cols), blk(p, cols),
                  pl.BlockSpec((1, 1, pair * p), lambda i: (i, 0, 0)),
                  pl.BlockSpec((1, 1, pair * p), lambda i: (i, 0, 0))],
        out_specs=blk(rows, cols),
        out_shape=jax.ShapeDtypeStruct((g, rows, cols), F32),
        scratch_shapes=[pltpu.VMEM((rows, pair * p), F32)] * 4,
        compiler_params=_params("parallel"),
        name="ssm",
    )(u, kin, wre, wim, vre, vim, a_re, a_im)
    return y.reshape(g, nc, bsz, chunk, ch).transpose(2, 1, 3, 0, 4).reshape(bsz, seq, w)


def _layer_norm(h, g, b):
    mu = jnp.mean(h, axis=1, keepdims=True)
    hc = h - mu
    var = jnp.mean(hc * hc, axis=1, keepdims=True)
    return hc * lax.rsqrt(var + LN_EPS) * g + b


def _route(logits_t):
    m = jnp.max(logits_t, axis=0, keepdims=True)
    e = jnp.exp(logits_t - m)
    prob = e / jnp.sum(e, axis=0, keepdims=True)
    best = None
    for g in range(N_EXPERT_GROUPS):
        v = [prob[g * EXPERTS_PER_GROUP + j:g * EXPERTS_PER_GROUP + j + 1, :] for j in range(EXPERTS_PER_GROUP)]
        m1 = functools.reduce(jnp.maximum, v)
        i1 = jnp.full(m1.shape, EXPERTS_PER_GROUP - 1, jnp.int32)
        for j in range(EXPERTS_PER_GROUP - 2, -1, -1):
            i1 = jnp.where(v[j] == m1, j, i1)
        w = [jnp.where(i1 == j, -1.0, v[j]) for j in range(EXPERTS_PER_GROUP)]
        m2 = functools.reduce(jnp.maximum, w)
        i2 = jnp.full(m1.shape, EXPERTS_PER_GROUP - 1, jnp.int32)
        for j in range(EXPERTS_PER_GROUP - 2, -1, -1):
            i2 = jnp.where(w[j] == m2, j, i2)
        cand = (m1 + m2, m1, m2, i1 + g * EXPERTS_PER_GROUP, i2 + g * EXPERTS_PER_GROUP)
        if best is None:
            best = cand
        else:
            better = cand[0] > best[0]
            best = tuple(jnp.where(better, a, b) for a, b in zip(cand, best))
    _, p1, p2, e1, e2 = best
    den = p1 + p2
    return jnp.concatenate([e1, e2], axis=0), jnp.concatenate([p1 / den, p2 / den], axis=0)


def _mix_kernel(x_ref, ya_ref, ys_ref, upc_ref, upp_ref, g1_ref, sc2_ref, sh2_ref, lng_ref, lnb_ref,
                wout_ref, wglu_ref, bglu_ref, wpool_ref, pscale_ref, wr_ref,
                x1_ref, u2_ref, ids_ref, gates_ref, *, alpha):
    i = pl.program_id(1)
    tm = x_ref.shape[1]
    aw = ya_ref.shape[2]
    sw = ys_ref.shape[2]
    y = ys_ref[0]
    y = 0.5 * y * (1.0 + jnp.tanh(math.sqrt(2.0 / math.pi) * (y + 0.044715 * (y * y * y))))
    z = jnp.dot(y.astype(BF16), wglu_ref[...], preferred_element_type=F32) + bglu_ref[...]
    y = y * _sigmoid(z)
    upc = upc_ref[0]
    upp = jnp.where(i > 0, upp_ref[0], 0.0)
    cat = jnp.concatenate([upp, upc], axis=0).astype(BF16)
    r = lax.broadcasted_iota(jnp.int32, (tm, 1), 0)
    lagm = (r + tm) - lax.broadcasted_iota(jnp.int32, (1, 2 * tm), 1)
    tpos = (i * tm + r + 1).astype(F32)
    gc = upc.shape[1] // len(POOL_WINDOWS)
    pooled = []
    for g, win in enumerate(POOL_WINDOWS):
        band = jnp.where((lagm >= 0) & (lagm < win), 1.0, 0.0).astype(BF16)
        ws = jnp.dot(band, cat[:, g * gc:(g + 1) * gc], preferred_element_type=F32)
        pg = ws / jnp.minimum(tpos, float(win)) - upc[:, g * gc:(g + 1) * gc]
        pooled.append(jnp.dot(pg.astype(BF16), wpool_ref[g], preferred_element_type=F32))
    yp = jnp.concatenate(pooled, axis=1) * pscale_ref[...]
    mix = (jnp.dot(ya_ref[0], wout_ref[0:aw, :], preferred_element_type=F32)
           + jnp.dot(y.astype(BF16), wout_ref[aw:aw + sw, :], preferred_element_type=F32)
           + jnp.dot(yp.astype(BF16), wout_ref[aw + sw:, :], preferred_element_type=F32))
    x1 = _layer_norm(alpha * x_ref[0] + (1.0 + g1_ref[0]) * mix, lng_ref[...], lnb_ref[...])
    x1_ref[0] = x1
    u2 = x1 * (1.0 + sc2_ref[0]) + sh2_ref[0]
    u2_ref[0] = u2
    logits_t = lax.dot_general(wr_ref[...], u2, (((1,), (1,)), ((), ())),
                               precision=lax.Precision.HIGHEST, preferred_element_type=F32)
    ids, gates = _route(logits_t)
    ids_ref[0] = ids
    gates_ref[0] = gates


def _mix(x, ya, ys, up, g1, sc2, sh2, lng, lnb, wout, wglu, bglu, wpool, pscale, wr_t, alpha):
    bsz, seq, d = x.shape
    tm = min(ROW_TILE, seq)
    tile = lambda n: pl.BlockSpec((1, tm, n), lambda b, i: (b, i, 0))
    prev = pl.BlockSpec((1, tm, up.shape[2]), lambda b, i: (b, jnp.maximum(i - 1, 0), 0))
    vec = pl.BlockSpec((1, 1, d), lambda b, i: (b, 0, 0))
    full = lambda a: pl.BlockSpec(a.shape, lambda b, i: (0,) * a.ndim)
    lane_rows = pl.BlockSpec((1, 2, tm), lambda b, i: (b, 0, i))
    return pl.pallas_call(
        functools.partial(_mix_kernel, alpha=alpha),
        grid=(bsz, seq // tm),
        in_specs=[tile(d), tile(ya.shape[2]), tile(ys.shape[2]), tile(up.shape[2]), prev, vec, vec, vec,
                  full(lng), full(lnb), full(wout), full(wglu), full(bglu), full(wpool), full(pscale), full(wr_t)],
        out_specs=[tile(d), tile(d), lane_rows, lane_rows],
        out_shape=[jax.ShapeDtypeStruct((bsz, seq, d), F32), jax.ShapeDtypeStruct((bsz, seq, d), F32),
                   jax.ShapeDtypeStruct((bsz, 2, seq), jnp.int32), jax.ShapeDtypeStruct((bsz, 2, seq), F32)],
        compiler_params=_params("parallel", "arbitrary"),
        name="mix",
    )(x, ya, ys, up, up, g1, sc2, sh2, lng, lnb, wout, wglu, bglu, wpool, pscale, wr_t)


def _rowcopy_kernel(idx_ref, src_ref, dst_ref, sem):
    base = pl.program_id(0) * COPY_ROWS

    def row_copy(r):
        return pltpu.make_async_copy(src_ref.at[pl.ds(idx_ref[base + r], 1)],
                                     dst_ref.at[pl.ds(base + r, 1)], sem)

    def issue(r, carry):
        row_copy(r).start()
        return carry

    def drain(r, carry):
        row_copy(r).wait()
        return carry

    lax.fori_loop(0, COPY_ROWS, issue, 0)
    lax.fori_loop(0, COPY_ROWS, drain, 0)


def _gather_rows(src, idx):
    n = idx.shape[0]
    return pl.pallas_call(
        _rowcopy_kernel,
        grid_spec=pltpu.PrefetchScalarGridSpec(
            num_scalar_prefetch=1, grid=(n // COPY_ROWS,),
            in_specs=[pl.BlockSpec(memory_space=pl.ANY)],
            out_specs=pl.BlockSpec(memory_space=pl.ANY),
            scratch_shapes=[pltpu.SemaphoreType.DMA(())]),
        out_shape=jax.ShapeDtypeStruct((n, src.shape[1]), src.dtype),
        compiler_params=_params("arbitrary"),
        name="rowcopy",
    )(idx, src)


def _expert_kernel(te_ref, act_ref, x_ref, wg_ref, wu_ref, wd_ref, y_ref):
    i = pl.program_id(0)

    @pl.when(act_ref[i] != 0)
    def _():
        x = x_ref[...].astype(BF16)
        g = jnp.dot(x, wg_ref[0], preferred_element_type=F32)
        u = jnp.dot(x, wu_ref[0], preferred_element_type=F32)
        h = (g * _sigmoid(g)) * u
        y_ref[...] = jnp.dot(h.astype(BF16), wd_ref[0], preferred_element_type=F32)

    @pl.when(act_ref[i] == 0)
    def _():
        y_ref[...] = jnp.zeros(y_ref.shape, y_ref.dtype)


def _experts(xg, tile_expert, tile_active, wg, wu, wd):
    rows, d = xg.shape
    f = wg.shape[2]
    tm = EXPERT_TILE
    return pl.pallas_call(
        _expert_kernel,
        grid_spec=pltpu.PrefetchScalarGridSpec(
            num_scalar_prefetch=2, grid=(rows // tm,),
            in_specs=[pl.BlockSpec((tm, d), lambda i, te, act: (i, 0)),
                      pl.BlockSpec((1, d, f), lambda i, te, act: (te[i], 0, 0)),
                      pl.BlockSpec((1, d, f), lambda i, te, act: (te[i], 0, 0)),
                      pl.BlockSpec((1, f, d), lambda i, te, act: (te[i], 0, 0))],
            out_specs=pl.BlockSpec((tm, d), lambda i, te, act: (i, 0))),
        out_shape=jax.ShapeDtypeStruct((rows, d), F32),
        compiler_params=_params("arbitrary"),
        name="experts",
    )(tile_expert, tile_active, xg, wg, wu, wd)


def _combine_kernel(x_ref, z_ref, gc_ref, g2_ref, lng_ref, lnb_ref, o_ref, *, alpha):
    gc = gc_ref[0]
    ffn = gc[:, 0:1] * z_ref[0, 0] + gc[:, 1:2] * z_ref[1, 0]
    o_ref[0] = _layer_norm(alpha * x_ref[0] + (1.0 + g2_ref[0]) * ffn, lng_ref[...], lnb_ref[...])


def _combine(x1, z, gates_col, g2, lng, lnb, alpha):
    bsz, seq, d = x1.shape
    tm = min(ROW_TILE, seq)
    tile = pl.BlockSpec((1, tm, d), lambda b, i: (b, i, 0))
    return pl.pallas_call(
        functools.partial(_combine_kernel, alpha=alpha),
        grid=(bsz, seq // tm),
        in_specs=[tile, pl.BlockSpec((2, 1, tm, d), lambda b, i: (0, b, i, 0)),
                  pl.BlockSpec((1, tm, 2), lambda b, i: (b, i, 0)),
                  pl.BlockSpec((1, 1, d), lambda b, i: (b, 0, 0)),
                  pl.BlockSpec(lng.shape, lambda b, i: (0, 0)), pl.BlockSpec(lnb.shape, lambda b, i: (0, 0))],
        out_specs=tile,
        out_shape=jax.ShapeDtypeStruct((bsz, seq, d), F32),
        compiler_params=_params("parallel", "arbitrary"),
        name="combine",
    )(x1, z, gates_col, g2, lng, lnb)


def _dispatch_plan(ids, n_tok):
    tm = EXPERT_TILE
    e_pair = jnp.concatenate([ids[:, 0, :].reshape(n_tok), ids[:, 1, :].reshape(n_tok)])
    tok_pair = jnp.concatenate([jnp.arange(n_tok, dtype=jnp.int32)] * 2)
    onehot = (e_pair[:, None] == jnp.arange(N_EXPERTS, dtype=jnp.int32)[None, :]).astype(jnp.int32)
    csum = jnp.cumsum(onehot, axis=0)
    rank = jnp.sum(onehot * csum, axis=1) - 1
    counts = csum[-1]
    padded = ((counts + tm - 1) // tm) * tm
    ends = jnp.cumsum(padded)
    offs = ends - padded
    pos = jnp.sum(onehot * offs[None, :], axis=1) + rank
    rows = 2 * n_tok + N_EXPERTS * tm
    src_token = jnp.zeros((rows,), jnp.int32).at[pos].set(tok_pair)
    tile_start = jnp.arange(rows // tm, dtype=jnp.int32) * tm
    tile_expert = jnp.minimum(jnp.sum((tile_start[:, None] >= ends[None, :]).astype(jnp.int32), axis=1),
                              N_EXPERTS - 1)
    tile_active = (tile_start < ends[-1]).astype(jnp.int32)
    return pos.astype(jnp.int32), src_token, tile_expert, tile_active


def _rope_tables(positions, dim):
    rot = dim // ROPE_FRAC
    half = rot // 2
    inv_freq = ROPE_THETA ** (-jnp.arange(half, dtype=F32) * 2.0 / rot)
    ang = positions.astype(F32)[..., None] * inv_freq
    cos, sin = jnp.cos(ang), jnp.sin(ang)
    shape = ang.shape[:-1]
    one = jnp.ones(shape + (dim - rot,), F32)
    zero = jnp.zeros(shape + (dim - rot,), F32)
    zh = jnp.zeros(shape + (half,), F32)
    c = jnp.concatenate([cos, cos, one], axis=-1)
    s1 = jnp.concatenate([-sin, zh, zero], axis=-1)
    s2 = jnp.concatenate([zh, sin, zero], axis=-1)
    rep = LANES // dim
    return jnp.concatenate([jnp.tile(a, (1, 1, rep)) for a in (c, s1, s2)], axis=-1)


def kernel(x, c, positions, w_ada, b_ada, w_in, w_out, ssm_lam_re, ssm_lam_im, ssm_log_step, ssm_b_re, ssm_b_im, ssm_c_re, ssm_c_im, ssm_d, ssm_w_glu, ssm_b_glu, pool_w, pool_scale, ln1_g, ln1_b, ln2_g, ln2_b, w_router, e_gate, e_up, e_down):
    bsz, seq, d = x.shape
    depth = w_ada.shape[0]
    n_tok = bsz * seq
    alpha = (2.0 * depth) ** 0.25
    sw = ssm_d.shape[1]
    pw = pool_scale.shape[1]
    aw = w_out.shape[1] - sw - pw
    kw = KV_HEADS * HEAD_DIM
    iqw = IDX_HEADS * IDX_DIM
    widths = (aw, kw, iqw, sw, pw)
    top_k = min(MAX_TOPK, seq // 4)

    c_pad = jnp.concatenate([c, jnp.zeros((8 - bsz % 8, d), F32)], axis=0) if bsz % 8 else c
    ada = _ada_all(c_pad, w_ada, b_ada)
    rq = _rope_tables(positions, HEAD_DIM)
    ri = _rope_tables(positions, IDX_DIM)
    wr_t = w_router.T

    o_q, o_k, o_v, o_iq, o_ik, o_iw, o_us, o_up = (0, aw, aw + kw, aw + 2 * kw, aw + 2 * kw + iqw,
                                                   aw + 2 * kw + iqw + IDX_DIM,
                                                   aw + 2 * kw + iqw + IDX_DIM + IDX_HEADS,
                                                   aw + 2 * kw + iqw + IDX_DIM + IDX_HEADS + sw)
    for l in range(depth):
        sh1, sc1, g1, sh2, sc2, g2 = [ada[l, :bsz, j * d:(j + 1) * d].reshape(bsz, 1, d) for j in range(6)]
        wl = w_in[l]
        tail = jnp.concatenate([wl[:, o_ik:o_us], jnp.zeros((d, LANES - IDX_DIM - IDX_HEADS), F32)], axis=1)
        w_r = jnp.concatenate([wl[:, o_q:o_ik], wl[:, o_us:], tail], axis=1).astype(BF16)
        q, k, v, iq, ik, iw, us, up = _inproj(x, sc1, sh1, w_r, rq, ri, widths)
        ya = _attention(q, k, v, iq, ik, iw, top_k)
        prep = _ssm_prepare(ssm_lam_re[l], ssm_lam_im[l], ssm_log_step[l], ssm_b_re[l], ssm_b_im[l],
                            ssm_c_re[l], ssm_c_im[l], ssm_d[l], SSM_CHUNK)
        ys = _ssm(us, prep, SSM_CHUNK)
        x1, u2, ids, gates = _mix(x, ya, ys, up, g1, sc2, sh2, ln1_g[l][None], ln1_b[l][None],
                                  w_out[l].astype(BF16), ssm_w_glu[l].astype(BF16), ssm_b_glu[l][None],
                                  pool_w[l].astype(BF16), pool_scale[l][None], wr_t, alpha)
        pos, src_token, tile_expert, tile_active = _dispatch_plan(ids, n_tok)
        xg = _gather_rows(u2.reshape(n_tok, d), src_token)
        yg = _experts(xg, tile_expert, tile_active, e_gate[l].astype(BF16), e_up[l].astype(BF16),
                      e_down[l].astype(BF16))
        z = _gather_rows(yg, pos).reshape(2, bsz, seq, d)
        x = _combine(x1, z, gates.transpose(0, 2, 1), g2, ln2_g[l][None], ln2_b[l][None], alpha)
    return x
```

```python
import functools
import math

import jax
import jax.numpy as jnp
from jax import lax
from jax.experimental import pallas as pl
from jax.experimental.pallas import tpu as pltpu

F32 = jnp.float32
BF16 = jnp.bfloat16

HEAD_DIM = 128
KV_HEADS = 2
IDX_HEADS = 8
IDX_DIM = 64
MAX_TOPK = 256
SSM_GROUP_CH = 16
SSM_STATE = 64
POOL_WINDOWS = (2, 4, 8, 16)
ROPE_THETA = 500000.0
ROPE_FRAC = 4
LN_EPS = 1e-5
N_EXPERTS = 16
N_EXPERT_GROUPS = 4
EXPERTS_PER_GROUP = N_EXPERTS // N_EXPERT_GROUPS

LANES = 128
VMEM_LIMIT = 56 * 1024 * 1024
ROW_TILE = 256
ATTN_Q_TILE = 128
ATTN_KEY_STEP = 512
BISECT_STEPS = 14
REDUCE_SLAB = 64
ROW_SUBLANES = 16
GATHER_SRC_ROWS = 10240
SSM_CHUNK = 32
EXPERT_TILE = 256
COPY_ROWS = 256
ADA_COLS = 1024


def _params(*sem):
    return pltpu.CompilerParams(dimension_semantics=sem, vmem_limit_bytes=VMEM_LIMIT)


def _sigmoid(x):
    return 1.0 / (1.0 + jnp.exp(-x))


def _ada_kernel(c_ref, w_ref, b_ref, o_ref):
    c = c_ref[...]
    cond = (c * _sigmoid(c)).astype(BF16)
    o_ref[0] = jnp.dot(cond, w_ref[0].astype(BF16), preferred_element_type=F32) + b_ref[0]


def _ada_all(c_pad, w_ada, b_ada):
    depth, d, n6 = w_ada.shape
    rows = c_pad.shape[0]
    return pl.pallas_call(
        _ada_kernel,
        grid=(depth, n6 // ADA_COLS),
        in_specs=[pl.BlockSpec((rows, d), lambda l, j: (0, 0)),
                  pl.BlockSpec((1, d, ADA_COLS), lambda l, j: (l, 0, j)),
                  pl.BlockSpec((1, 1, ADA_COLS), lambda l, j: (l, 0, j))],
        out_specs=pl.BlockSpec((1, rows, ADA_COLS), lambda l, j: (l, 0, j)),
        out_shape=jax.ShapeDtypeStruct((depth, rows, n6), F32),
        compiler_params=_params("arbitrary", "arbitrary"),
        name="ada",
    )(c_pad, w_ada, b_ada.reshape(depth, 1, n6))


def _rope(xv, tab, half):
    w = xv.shape[1]
    rep = w // LANES
    c = jnp.tile(tab[:, 0:LANES], (1, rep))
    s1 = jnp.tile(tab[:, LANES:2 * LANES], (1, rep))
    s2 = jnp.tile(tab[:, 2 * LANES:3 * LANES], (1, rep))
    return xv * c + pltpu.roll(xv, w - half, 1) * s1 + pltpu.roll(xv, half, 1) * s2


def _inproj_kernel(x_ref, sc_ref, sh_ref, w_ref, wvt_ref, wiwt_ref, rq_ref, ri_ref,
                   q_ref, k_ref, vt_ref, iq_ref, ik_ref, iwt_ref, us_ref, up_ref, *, widths, idx_scale):
    aw, kw, iqw, sw, pw = widths
    u = (x_ref[0] * (1.0 + sc_ref[0]) + sh_ref[0]).astype(BF16)
    rq = rq_ref[0]
    ri = ri_ref[0]

    def mm(lo, n):
        return jnp.dot(u, w_ref[:, lo:lo + n], preferred_element_type=F32)

    def mm_t(wt_ref):
        return lax.dot_general(wt_ref[...], u, (((1,), (1,)), ((), ())), preferred_element_type=F32)

    qhalf = HEAD_DIM // ROPE_FRAC // 2
    ihalf = IDX_DIM // ROPE_FRAC // 2
    o = 0
    q_ref[0] = (_rope(mm(o, aw), rq, qhalf) * (HEAD_DIM ** -0.5)).astype(BF16)
    o += aw
    k_ref[0] = _rope(mm(o, kw), rq, qhalf).astype(BF16)
    o += kw
    iq_ref[0] = _rope(mm(o, iqw), ri, ihalf).astype(BF16)
    o += iqw
    us_ref[0] = mm(o, sw)
    o += sw
    up_ref[0] = mm(o, pw)
    o += pw
    ik_ref[0] = _rope(mm(o, LANES), ri, ihalf)[:, 0:IDX_DIM].astype(BF16)
    vt_ref[0] = mm_t(wvt_ref).astype(BF16)
    iwt_ref[0] = mm_t(wiwt_ref) * idx_scale


def _inproj(x, sc, sh, w_r, wvt, wiwt, rq, ri, widths):
    bsz, seq, d = x.shape
    aw, kw, iqw, sw, pw = widths
    tm = min(ROW_TILE, seq)
    tile = lambda n: pl.BlockSpec((1, tm, n), lambda b, i: (b, i, 0))
    tile_t = lambda n: pl.BlockSpec((1, n, tm), lambda b, i: (b, 0, i))
    vec = pl.BlockSpec((1, 1, d), lambda b, i: (b, 0, 0))
    full = lambda a: pl.BlockSpec(a.shape, lambda b, i: (0, 0))
    sds = jax.ShapeDtypeStruct
    return pl.pallas_call(
        functools.partial(_inproj_kernel, widths=widths, idx_scale=(IDX_DIM ** -0.5) * (IDX_HEADS ** -0.5)),
        grid=(bsz, seq // tm),
        in_specs=[tile(d), vec, vec, full(w_r), full(wvt), full(wiwt), tile(3 * LANES), tile(3 * LANES)],
        out_specs=[tile(aw), tile(kw), tile_t(kw), tile(iqw), tile(IDX_DIM), tile_t(IDX_HEADS), tile(sw), tile(pw)],
        out_shape=[sds((bsz, seq, aw), BF16), sds((bsz, seq, kw), BF16), sds((bsz, kw, seq), BF16),
                   sds((bsz, seq, iqw), BF16), sds((bsz, seq, IDX_DIM), BF16), sds((bsz, IDX_HEADS, seq), F32),
                   sds((bsz, seq, sw), F32), sds((bsz, seq, pw), F32)],
        compiler_params=_params("parallel", "arbitrary"),
        name="inproj",
    )(x, sc, sh, w_r, wvt, wiwt, rq, ri)


def _col_reduce(op, v):
    rows, n = v.shape
    slab = op(v.reshape(rows // REDUCE_SLAB, REDUCE_SLAB, n), axis=0)
    return op(slab, axis=0, keepdims=True)


def _attn_kernel(q_ref, iq_ref, iwt_ref, k_ref, vt_ref, ik_ref, o_ref, mask_ref, *, q0, tq, keys, top_k):
    i = pl.program_id(1)
    t = q0 + i * tq + lax.broadcasted_iota(jnp.int32, (1, tq), 1)
    kpos = lax.broadcasted_iota(jnp.int32, (keys, 1), 0)
    causal = kpos <= t

    iq = iq_ref[0]
    ik = ik_ref[0]
    iwt = iwt_ref[0]
    isc = jnp.zeros((keys, tq), F32)
    for h in range(IDX_HEADS):
        r = lax.dot_general(ik, iq[:, h * IDX_DIM:(h + 1) * IDX_DIM], (((1,), (1,)), ((), ())),
                            preferred_element_type=F32)
        isc = isc + jnp.maximum(r, 0.0) * iwt[h:h + 1, :]
    x = jnp.where(causal, isc, -jnp.inf)

    kf = float(top_k)
    n_valid = (t + 1).astype(F32)
    need = n_valid > kf

    def count_ge(theta):
        return _col_reduce(jnp.sum, jnp.where(x >= theta, 1.0, 0.0))

    rmax = _col_reduce(jnp.max, x)
    rmin = _col_reduce(jnp.min, jnp.where(causal, isc, jnp.inf))
    c_max = count_ge(rmax)
    top_tie = c_max >= kf
    lo0 = jnp.where(need, jnp.where(top_tie, rmax, rmin), -jnp.inf)
    clo0 = jnp.where(top_tie, c_max, n_valid)
    done0 = jnp.where(need & jnp.logical_not(top_tie) & (clo0 != kf), 0.0, 1.0)

    def bisect(lo, hi, clo, chi, done):
        mid = 0.5 * lo + 0.5 * hi
        c = count_ge(mid)
        live = done == 0.0
        ge = c >= kf
        up = live & ge
        dn = live & jnp.logical_not(ge)
        lo = jnp.where(up, mid, lo)
        clo = jnp.where(up, c, clo)
        hi = jnp.where(dn, mid, hi)
        chi = jnp.where(dn, c, chi)
        done = jnp.where(clo == kf, 1.0, done)
        return lo, hi, clo, chi, done

    def n_active(done):
        return jnp.sum(1.0 - done).astype(jnp.int32)

    def body1(s):
        _, it, lo, hi, clo, chi, done = s
        lo, hi, clo, chi, done = bisect(lo, hi, clo, chi, done)
        return n_active(done), it + 1, lo, hi, clo, chi, done

    def body2(s):
        _, it, lo, hi, clo, chi, done = s
        lo, hi, clo, chi, done = bisect(lo, hi, clo, chi, done)
        inb = (x >= lo) & (x < hi)
        vmin = _col_reduce(jnp.min, jnp.where(inb, x, jnp.inf))
        vmax = _col_reduce(jnp.max, jnp.where(inb, x, -jnp.inf))
        c2 = count_ge(vmax)
        live = done == 0.0
        single = vmin == vmax
        top_ok = c2 >= kf
        take_top = live & jnp.logical_not(single) & top_ok
        drop_top = live & jnp.logical_not(single) & jnp.logical_not(top_ok)
        lo = jnp.where(live, jnp.where(take_top, vmax, vmin), lo)
        clo = jnp.where(take_top, c2, clo)
        hi = jnp.where(drop_top, vmax, hi)
        chi = jnp.where(drop_top, c2, chi)
        done = jnp.where(live & (single | top_ok), 1.0, done)
        return n_active(done), it + 1, lo, hi, clo, chi, done

    state = (n_active(done0), jnp.int32(0), lo0, rmax, clo0, c_max, done0)
    state = lax.while_loop(lambda s: (s[0] > 0) & (s[1] < BISECT_STEPS), body1, state)
    state = lax.while_loop(lambda s: s[0] > 0, body2, state)
    _, _, lo, _, clo, _, _ = state

    mask_ref[...] = jnp.where(causal & (x >= lo), 0.0, -1e30)

    tie_q = jnp.sum(jnp.where(need & (clo > kf), 1.0, 0.0)).astype(jnp.int32)

    @pl.when(tie_q > 0)
    def _():
        room = kf - jnp.sum(jnp.where(x > lo, 1.0, 0.0), axis=0, keepdims=True)
        step = 256
        rr = lax.broadcasted_iota(jnp.int32, (step, step), 0)
        cc = lax.broadcasted_iota(jnp.int32, (step, step), 1)
        before = jnp.where(cc < rr, 1.0, 0.0).astype(BF16)
        carry = jnp.zeros((1, tq), F32)
        for c0 in range(0, keys, step):
            xs = x[c0:c0 + step, :]
            e = jnp.where(xs == lo, 1.0, 0.0)
            rank = carry + jnp.dot(before, e.astype(BF16), preferred_element_type=F32)
            keep = (xs > lo) | ((xs == lo) & (rank < room))
            mask_ref[c0:c0 + step, :] = jnp.where(keep & (xs > -jnp.inf), 0.0, -1e30)
            carry = carry + jnp.sum(e, axis=0, keepdims=True)

    bias = mask_ref[...]
    q = q_ref[0]
    group = q.shape[1] // HEAD_DIM // KV_HEADS
    bias_g = jnp.concatenate([bias] * group, axis=1)
    for g in range(KV_HEADS):
        qg = jnp.concatenate([q[:, (g * group + j) * HEAD_DIM:(g * group + j + 1) * HEAD_DIM]
                              for j in range(group)], axis=0)
        kg = k_ref[0, :, g * HEAD_DIM:(g + 1) * HEAD_DIM]
        vtg = vt_ref[0, g * HEAD_DIM:(g + 1) * HEAD_DIM, :]
        s = lax.dot_general(kg, qg, (((1,), (1,)), ((), ())), preferred_element_type=F32) + bias_g
        m = _col_reduce(jnp.max, s)
        p = jnp.exp(s - m)
        den = _col_reduce(jnp.sum, p)
        og = jnp.dot(vtg, p.astype(BF16), preferred_element_type=F32) / den
        for j in range(group):
            h = g * group + j
            o_ref[0, :, h * HEAD_DIM:(h + 1) * HEAD_DIM] = og[:, j * tq:(j + 1) * tq].T.astype(BF16)


def _attention(q, k, vt, iq, ik, iwt, top_k):
    bsz, seq, aw = q.shape
    tq = min(ATTN_Q_TILE, seq)
    step = min(ATTN_KEY_STEP, seq)
    outs = []
    for q0 in range(0, seq, step):
        keys = q0 + step
        nq = step // tq
        qtile = lambda n, q0=q0: pl.BlockSpec((1, tq, n), lambda b, i: (b, q0 // tq + i, 0))
        ktile = lambda n, keys=keys: pl.BlockSpec((1, keys, n), lambda b, i: (b, 0, 0))
        outs.append(pl.pallas_call(
            functools.partial(_attn_kernel, q0=q0, tq=tq, keys=keys, top_k=top_k),
            grid=(bsz, nq),
            in_specs=[qtile(aw), qtile(iq.shape[2]),
                      pl.BlockSpec((1, iwt.shape[1], tq), lambda b, i, q0=q0: (b, 0, q0 // tq + i)),
                      ktile(k.shape[2]),
                      pl.BlockSpec((1, vt.shape[1], keys), lambda b, i: (b, 0, 0)),
                      ktile(ik.shape[2])],
            out_specs=pl.BlockSpec((1, tq, aw), lambda b, i: (b, i, 0)),
            out_shape=jax.ShapeDtypeStruct((bsz, step, aw), BF16),
            scratch_shapes=[pltpu.VMEM((keys, tq), F32)],
            compiler_params=_params("parallel", "arbitrary"),
            name=f"attn_k{keys}",
        )(q, iq, iwt, k, vt, ik))
    return jnp.concatenate(outs, axis=1)


def _ssm_kernel(u_ref, kin_ref, wre_ref, wim_ref, vre_ref, vim_ref, are_ref, aim_ref, y_ref,
                sre_ref, sim_ref, xre_ref, xim_ref, *, bsz, n_chunks):
    pair = u_ref.shape[0]
    p = wre_ref.shape[2]
    for j in range(pair):
        u = u_ref[j]
        sre_ref[:, j * p:(j + 1) * p] = jnp.dot(u, wre_ref[j], preferred_element_type=F32)
        sim_ref[:, j * p:(j + 1) * p] = jnp.dot(u, wim_ref[j], preferred_element_type=F32)
    a_re = are_ref[0]
    a_im = aim_ref[0]
    x_re = jnp.zeros((bsz, pair * p), F32)
    x_im = jnp.zeros((bsz, pair * p), F32)
    for c in range(n_chunks):
        rows = slice(c * bsz, (c + 1) * bsz)
        xre_ref[rows, :] = x_re
        xim_ref[rows, :] = x_im
        n_re = a_re * x_re - a_im * x_im + sre_ref[rows, :]
        n_im = a_re * x_im + a_im * x_re + sim_ref[rows, :]
        x_re, x_im = n_re, n_im
    for j in range(pair):
        xr = xre_ref[:, j * p:(j + 1) * p].astype(BF16)
        xi = xim_ref[:, j * p:(j + 1) * p].astype(BF16)
        y_ref[j] = (jnp.dot(u_ref[j], kin_ref[j], preferred_element_type=F32)
                    + jnp.dot(xr, vre_ref[j], preferred_element_type=F32)
                    + jnp.dot(xi, vim_ref[j], preferred_element_type=F32))


def _ssm_prepare(lam_re, lam_im, log_step, b_re, b_im, c_re, c_im, d_skip, chunk):
    g, p = lam_re.shape
    ch = b_re.shape[2]
    hp = lax.Precision.HIGHEST
    step = jnp.exp(log_step)[:, None]
    lsr, lsi = lam_re * step, lam_im * step
    er = jnp.exp(lsr)
    nr, ni = er * jnp.cos(lsi) - 1.0, er * jnp.sin(lsi)
    den = lam_re * lam_re + lam_im * lam_im
    fr, fi = (nr * lam_re + ni * lam_im) / den, (ni * lam_re - nr * lam_im) / den
    bbr = fr[..., None] * b_re - fi[..., None] * b_im
    bbi = fr[..., None] * b_im + fi[..., None] * b_re
    n = jnp.arange(chunk + 1, dtype=F32)
    mag = jnp.exp(lsr[..., None] * n)
    pr, pi = mag * jnp.cos(lsi[..., None] * n), mag * jnp.sin(lsi[..., None] * n)
    prt, pit = pr[:, :, :chunk, None], pi[:, :, :chunk, None]
    wr = prt * bbr[:, :, None, :] - pit * bbi[:, :, None, :]
    wi = prt * bbi[:, :, None, :] + pit * bbr[:, :, None, :]
    ker = (jnp.einsum("gcp,gptd->gtcd", c_re, wr, precision=hp)
           - jnp.einsum("gcp,gptd->gtcd", c_im, wi, precision=hp))
    lag = jnp.arange(chunk)[None, :] - jnp.arange(chunk)[:, None]
    kexp = jnp.where((lag >= 0)[None, :, :, None, None], ker[:, jnp.clip(lag, 0, chunk - 1)], 0.0)
    kin = kexp.transpose(0, 1, 4, 2, 3)
    eye_t = jnp.eye(chunk, dtype=F32)
    eye_c = jnp.eye(ch, dtype=F32)
    skip = d_skip.reshape(g, ch)
    kin = kin + (eye_t[None, :, None, :, None] * eye_c[None, None, :, None, :] * skip[:, None, :, None, None])
    kin = kin.reshape(g, chunk * ch, chunk * ch)
    to_in = lambda w: w[:, :, ::-1, :].transpose(0, 2, 3, 1).reshape(g, chunk * ch, p)
    pr1, pi1 = pr[:, None, :, 1:], pi[:, None, :, 1:]
    vr = c_re[..., None] * pr1 - c_im[..., None] * pi1
    vi = c_re[..., None] * pi1 + c_im[..., None] * pr1
    to_out = lambda v: v.transpose(0, 2, 3, 1).reshape(g, p, chunk * ch)
    pair = 2
    a_re = pr[..., chunk].reshape(g // pair, 1, pair * p)
    a_im = pi[..., chunk].reshape(g // pair, 1, pair * p)
    return (kin.astype(BF16), to_in(wr).astype(BF16), to_in(wi).astype(BF16),
            to_out(vr).astype(BF16), to_out(-vi).astype(BF16), a_re, a_im)


def _ssm(us, prep, chunk):
    bsz, seq, w = us.shape
    kin, wre, wim, vre, vim, a_re, a_im = prep
    g, p = wre.shape[0], wre.shape[2]
    ch = w // g
    nc = seq // chunk
    rows = nc * bsz
    cols = chunk * ch
    pair = 2
    u = us.reshape(bsz, nc, chunk, g, ch).transpose(3, 1, 0, 2, 4).reshape(g, rows, cols).astype(BF16)
    blk = lambda a, b: pl.BlockSpec((pair, a, b), lambda i: (i, 0, 0))
    y = pl.pallas_call(
        functools.partial(_ssm_kernel, bsz=bsz, n_chunks=nc),
        grid=(g // pair,),
        in_specs=[blk(rows, cols), blk(cols, cols), blk(cols, p), blk(cols, p), blk(p, cols), blk(p, cols),
                  pl.BlockSpec((1, 1, pair * p), lambda i: (i, 0, 0)),
                  pl.BlockSpec((1, 1, pair * p), lambda i: (i, 0, 0))],
        out_specs=blk(rows, cols),
        out_shape=jax.ShapeDtypeStruct((g, rows, cols), F32),
        scratch_shapes=[pltpu.VMEM((rows, pair * p), F32)] * 4,
        compiler_params=_params("parallel"),
        name="ssm",
    )(u, kin, wre, wim, vre, vim, a_re, a_im)
    return y.reshape(g, nc, bsz, chunk, ch).transpose(2, 1, 3, 0, 4).reshape(bsz, seq, w)


def _layer_norm(h, g, b):
    mu = jnp.mean(h, axis=1, keepdims=True)
    hc = h - mu
    var = jnp.mean(hc * hc, axis=1, keepdims=True)
    return hc * lax.rsqrt(var + LN_EPS) * g + b


def _route(logits_t):
    m = jnp.max(logits_t, axis=0, keepdims=True)
    e = jnp.exp(logits_t - m)
    prob = e / jnp.sum(e, axis=0, keepdims=True)
    best = None
    for g in range(N_EXPERT_GROUPS):
        v = [prob[g * EXPERTS_PER_GROUP + j:g * EXPERTS_PER_GROUP + j + 1, :] for j in range(EXPERTS_PER_GROUP)]
        m1 = functools.reduce(jnp.maximum, v)
        i1 = jnp.full(m1.shape, EXPERTS_PER_GROUP - 1, jnp.int32)
        for j in range(EXPERTS_PER_GROUP - 2, -1, -1):
            i1 = jnp.where(v[j] == m1, j, i1)
        w = [jnp.where(i1 == j, -1.0, v[j]) for j in range(EXPERTS_PER_GROUP)]
        m2 = functools.reduce(jnp.maximum, w)
        i2 = jnp.full(m1.shape, EXPERTS_PER_GROUP - 1, jnp.int32)
        for j in range(EXPERTS_PER_GROUP - 2, -1, -1):
            i2 = jnp.where(w[j] == m2, j, i2)
        cand = (m1 + m2, m1, m2, i1 + g * EXPERTS_PER_GROUP, i2 + g * EXPERTS_PER_GROUP)
        if best is None:
            best = cand
        else:
            better = cand[0] > best[0]
            best = tuple(jnp.where(better, a, b) for a, b in zip(cand, best))
    _, p1, p2, e1, e2 = best
    den = p1 + p2
    return jnp.concatenate([e1, e2], axis=0), jnp.concatenate([p1 / den, p2 / den], axis=0)


def _mix_kernel(x_ref, ya_ref, ys_ref, upc_ref, upp_ref, g1_ref, sc2_ref, sh2_ref, lng_ref, lnb_ref,
                wout_ref, wglu_ref, bglu_ref, wpool_ref, pscale_ref, wr_ref,
                x1_ref, u2_ref, ids_ref, gates_ref, *, alpha):
    i = pl.program_id(1)
    tm = x_ref.shape[1]
    aw = ya_ref.shape[2]
    sw = ys_ref.shape[2]
    y = ys_ref[0]
    y = 0.5 * y * (1.0 + jnp.tanh(math.sqrt(2.0 / math.pi) * (y + 0.044715 * (y * y * y))))
    z = jnp.dot(y.astype(BF16), wglu_ref[...], preferred_element_type=F32) + bglu_ref[...]
    y = y * _sigmoid(z)
    upc = upc_ref[0]
    upp = jnp.where(i > 0, upp_ref[0], 0.0)
    cat = jnp.concatenate([upp, upc], axis=0).astype(BF16)
    r = lax.broadcasted_iota(jnp.int32, (tm, 1), 0)
    lagm = (r + tm) - lax.broadcasted_iota(jnp.int32, (1, 2 * tm), 1)
    tpos = (i * tm + r + 1).astype(F32)
    gc = upc.shape[1] // len(POOL_WINDOWS)
    pooled = []
    for g, win in enumerate(POOL_WINDOWS):
        band = jnp.where((lagm >= 0) & (lagm < win), 1.0, 0.0).astype(BF16)
        ws = jnp.dot(band, cat[:, g * gc:(g + 1) * gc], preferred_element_type=F32)
        pg = ws / jnp.minimum(tpos, float(win)) - upc[:, g * gc:(g + 1) * gc]
        pooled.append(jnp.dot(pg.astype(BF16), wpool_ref[g], preferred_element_type=F32))
    yp = jnp.concatenate(pooled, axis=1) * pscale_ref[...]
    mix = (jnp.dot(ya_ref[0], wout_ref[0:aw, :], preferred_element_type=F32)
           + jnp.dot(y.astype(BF16), wout_ref[aw:aw + sw, :], preferred_element_type=F32)
           + jnp.dot(yp.astype(BF16), wout_ref[aw + sw:, :], preferred_element_type=F32))
    x1 = _layer_norm(alpha * x_ref[0] + (1.0 + g1_ref[0]) * mix, lng_ref[...], lnb_ref[...])
    x1_ref[0] = x1
    u2 = x1 * (1.0 + sc2_ref[0]) + sh2_ref[0]
    u2_ref[0] = u2.astype(BF16)
    logits_t = lax.dot_general(wr_ref[...], u2, (((1,), (1,)), ((), ())),
                               precision=lax.Precision.HIGHEST, preferred_element_type=F32)
    ids, gates = _route(logits_t)
    ids_ref[0] = ids
    gates_ref[0] = gates


def _mix(x, ya, ys, up, g1, sc2, sh2, lng, lnb, wout, wglu, bglu, wpool, pscale, wr_t, alpha):
    bsz, seq, d = x.shape
    tm = min(ROW_TILE, seq)
    tile = lambda n: pl.BlockSpec((1, tm, n), lambda b, i: (b, i, 0))
    prev = pl.BlockSpec((1, tm, up.shape[2]), lambda b, i: (b, jnp.maximum(i - 1, 0), 0))
    vec = pl.BlockSpec((1, 1, d), lambda b, i: (b, 0, 0))
    full = lambda a: pl.BlockSpec(a.shape, lambda b, i: (0,) * a.ndim)
    lane_rows = pl.BlockSpec((1, 2, tm), lambda b, i: (b, 0, i))
    return pl.pallas_call(
        functools.partial(_mix_kernel, alpha=alpha),
        grid=(bsz, seq // tm),
        in_specs=[tile(d), tile(ya.shape[2]), tile(ys.shape[2]), tile(up.shape[2]), prev, vec, vec, vec,
                  full(lng), full(lnb), full(wout), full(wglu), full(bglu), full(wpool), full(pscale), full(wr_t)],
        out_specs=[tile(d), tile(d), lane_rows, lane_rows],
        out_shape=[jax.ShapeDtypeStruct((bsz, seq, d), F32), jax.ShapeDtypeStruct((bsz, seq, d), BF16),
                   jax.ShapeDtypeStruct((bsz, 2, seq), jnp.int32), jax.ShapeDtypeStruct((bsz, 2, seq), F32)],
        compiler_params=_params("parallel", "arbitrary"),
        name="mix",
    )(x, ya, ys, up, up, g1, sc2, sh2, lng, lnb, wout, wglu, bglu, wpool, pscale, wr_t)


def _gather_kernel(idx_ref, src_ref, *rest, lo, n_src, has_prev):
    if has_prev:
        prev_ref, out_ref, buf, sem = rest
    else:
        out_ref, buf, sem = rest

    @pl.when(pl.program_id(0) == 0)
    def _():
        cp = pltpu.make_async_copy(src_ref.at[pl.ds(lo, n_src)], buf, sem)
        cp.start()
        cp.wait()

    base = pl.program_id(0) * COPY_ROWS

    def move(r, carry):
        s = idx_ref[base + r] - lo
        row = buf[jnp.clip(s, 0, n_src - 1)]
        if has_prev:
            row = jnp.where((s >= 0) & (s < n_src), row, prev_ref[r])
        out_ref[r] = row
        return carry

    lax.fori_loop(0, COPY_ROWS, move, 0, unroll=8)


def _gather_rows(src, idx):
    n = idx.shape[0]
    n_chunks = pl.cdiv(src.shape[0], GATHER_SRC_ROWS)
    n_src = src.shape[0] // n_chunks
    assert n_src * n_chunks == src.shape[0]
    block = pl.BlockSpec((COPY_ROWS,) + src.shape[1:], lambda i, idx: (i, 0, 0))
    out = None
    for c in range(n_chunks):
        has_prev = c > 0
        out = pl.pallas_call(
            functools.partial(_gather_kernel, lo=c * n_src, n_src=n_src, has_prev=has_prev),
            grid_spec=pltpu.PrefetchScalarGridSpec(
                num_scalar_prefetch=1, grid=(n // COPY_ROWS,),
                in_specs=[pl.BlockSpec(memory_space=pl.ANY)] + ([block] if has_prev else []),
                out_specs=block,
                scratch_shapes=[pltpu.VMEM((n_src,) + src.shape[1:], src.dtype), pltpu.SemaphoreType.DMA(())]),
            out_shape=jax.ShapeDtypeStruct((n,) + src.shape[1:], src.dtype),
            compiler_params=_params("arbitrary"),
            name="gather",
        )(*((idx, src, out) if has_prev else (idx, src)))
    return out


def _expert_kernel(te_ref, act_ref, x_ref, wg_ref, wu_ref, wd_ref, y_ref):
    i = pl.program_id(0)

    @pl.when(act_ref[i] != 0)
    def _():
        x = x_ref[...]
        g = jnp.dot(x, wg_ref[0], preferred_element_type=F32)
        u = jnp.dot(x, wu_ref[0], preferred_element_type=F32)
        h = (g * _sigmoid(g)) * u
        y_ref[...] = jnp.dot(h.astype(BF16), wd_ref[0], preferred_element_type=F32).astype(BF16)

    @pl.when(act_ref[i] == 0)
    def _():
        y_ref[...] = jnp.zeros(y_ref.shape, y_ref.dtype)


def _experts(xg, tile_expert, tile_active, wg, wu, wd):
    rows = xg.shape[0]
    d, f = wg.shape[1], wg.shape[2]
    tm = EXPERT_TILE
    packed = pl.BlockSpec((tm, d), lambda i, te, act: (i, 0))
    return pl.pallas_call(
        _expert_kernel,
        grid_spec=pltpu.PrefetchScalarGridSpec(
            num_scalar_prefetch=2, grid=(rows // tm,),
            in_specs=[packed,
                      pl.BlockSpec((1, d, f), lambda i, te, act: (te[i], 0, 0)),
                      pl.BlockSpec((1, d, f), lambda i, te, act: (te[i], 0, 0)),
                      pl.BlockSpec((1, f, d), lambda i, te, act: (te[i], 0, 0))],
            out_specs=packed),
        out_shape=jax.ShapeDtypeStruct((rows, d), BF16),
        compiler_params=_params("arbitrary"),
        name="experts",
    )(tile_expert, tile_active, xg, wg, wu, wd)


def _combine_kernel(x_ref, z_ref, gc_ref, g2_ref, lng_ref, lnb_ref, o_ref, *, alpha):
    gc = gc_ref[0]
    ffn = gc[:, 0:1] * z_ref[0, 0].astype(F32) + gc[:, 1:2] * z_ref[1, 0].astype(F32)
    o_ref[0] = _layer_norm(alpha * x_ref[0] + (1.0 + g2_ref[0]) * ffn, lng_ref[...], lnb_ref[...])


def _combine(x1, z, gates_col, g2, lng, lnb, alpha):
    bsz, seq, d = x1.shape
    tm = min(ROW_TILE, seq)
    tile = pl.BlockSpec((1, tm, d), lambda b, i: (b, i, 0))
    return pl.pallas_call(
        functools.partial(_combine_kernel, alpha=alpha),
        grid=(bsz, seq // tm),
        in_specs=[tile, pl.BlockSpec((2, 1, tm, d), lambda b, i: (0, b, i, 0)),
                  pl.BlockSpec((1, tm, 2), lambda b, i: (b, i, 0)),
                  pl.BlockSpec((1, 1, d), lambda b, i: (b, 0, 0)),
                  pl.BlockSpec(lng.shape, lambda b, i: (0, 0)), pl.BlockSpec(lnb.shape, lambda b, i: (0, 0))],
        out_specs=tile,
        out_shape=jax.ShapeDtypeStruct((bsz, seq, d), F32),
        compiler_params=_params("parallel", "arbitrary"),
        name="combine",
    )(x1, z, gates_col, g2, lng, lnb)


def _dispatch_plan(ids, n_tok):
    tm = EXPERT_TILE
    e_pair = jnp.concatenate([ids[:, 0, :].reshape(n_tok), ids[:, 1, :].reshape(n_tok)])
    tok_pair = jnp.concatenate([jnp.arange(n_tok, dtype=jnp.int32)] * 2)
    onehot = (e_pair[:, None] == jnp.arange(N_EXPERTS, dtype=jnp.int32)[None, :]).astype(jnp.int32)
    csum = jnp.cumsum(onehot, axis=0)
    rank = jnp.sum(onehot * csum, axis=1) - 1
    counts = csum[-1]
    padded = ((counts + tm - 1) // tm) * tm
    ends = jnp.cumsum(padded)
    offs = ends - padded
    pos = jnp.sum(onehot * offs[None, :], axis=1) + rank
    rows = 2 * n_tok + N_EXPERTS * tm
    src_token = jnp.zeros((rows,), jnp.int32).at[pos].set(tok_pair)
    tile_start = jnp.arange(rows // tm, dtype=jnp.int32) * tm
    tile_expert = jnp.minimum(jnp.sum((tile_start[:, None] >= ends[None, :]).astype(jnp.int32), axis=1),
                              N_EXPERTS - 1)
    tile_active = (tile_start < ends[-1]).astype(jnp.int32)
    return pos.astype(jnp.int32), src_token, tile_expert, tile_active


def _rope_tables(positions, dim):
    rot = dim // ROPE_FRAC
    half = rot // 2
    inv_freq = ROPE_THETA ** (-jnp.arange(half, dtype=F32) * 2.0 / rot)
    ang = positions.astype(F32)[..., None] * inv_freq
    cos, sin = jnp.cos(ang), jnp.sin(ang)
    shape = ang.shape[:-1]
    one = jnp.ones(shape + (dim - rot,), F32)
    zero = jnp.zeros(shape + (dim - rot,), F32)
    zh = jnp.zeros(shape + (half,), F32)
    c = jnp.concatenate([cos, cos, one], axis=-1)
    s1 = jnp.concatenate([-sin, zh, zero], axis=-1)
    s2 = jnp.concatenate([zh, sin, zero], axis=-1)
    rep = LANES // dim
    return jnp.concatenate([jnp.tile(a, (1, 1, rep)) for a in (c, s1, s2)], axis=-1)


def kernel(x, c, positions, w_ada, b_ada, w_in, w_out, ssm_lam_re, ssm_lam_im, ssm_log_step, ssm_b_re, ssm_b_im, ssm_c_re, ssm_c_im, ssm_d, ssm_w_glu, ssm_b_glu, pool_w, pool_scale, ln1_g, ln1_b, ln2_g, ln2_b, w_router, e_gate, e_up, e_down):
    bsz, seq, d = x.shape
    depth = w_ada.shape[0]
    n_tok = bsz * seq
    alpha = (2.0 * depth) ** 0.25
    sw = ssm_d.shape[1]
    pw = pool_scale.shape[1]
    aw = w_out.shape[1] - sw - pw
    kw = KV_HEADS * HEAD_DIM
    iqw = IDX_HEADS * IDX_DIM
    widths = (aw, kw, iqw, sw, pw)
    top_k = min(MAX_TOPK, seq // 4)

    c_pad = jnp.concatenate([c, jnp.zeros((8 - bsz % 8, d), F32)], axis=0) if bsz % 8 else c
    ada = _ada_all(c_pad, w_ada, b_ada)
    rq = _rope_tables(positions, HEAD_DIM)
    ri = _rope_tables(positions, IDX_DIM)
    wr_t = w_router.T

    o_q, o_k, o_v, o_iq, o_ik, o_iw, o_us, o_up = (0, aw, aw + kw, aw + 2 * kw, aw + 2 * kw + iqw,
                                                   aw + 2 * kw + iqw + IDX_DIM,
                                                   aw + 2 * kw + iqw + IDX_DIM + IDX_HEADS,
                                                   aw + 2 * kw + iqw + IDX_DIM + IDX_HEADS + sw)
    for l in range(depth):
        sh1, sc1, g1, sh2, sc2, g2 = [ada[l, :bsz, j * d:(j + 1) * d].reshape(bsz, 1, d) for j in range(6)]
        wl = w_in[l]
        w_r = jnp.concatenate([wl[:, o_q:o_v], wl[:, o_iq:o_ik], wl[:, o_us:], wl[:, o_ik:o_iw],
                               jnp.zeros((d, LANES - IDX_DIM), F32)], axis=1).astype(BF16)
        wvt = wl[:, o_v:o_iq].T.astype(BF16)
        wiwt = wl[:, o_iw:o_us].T.astype(BF16)
        q, k, vt, iq, ik, iwt, us, up = _inproj(x, sc1, sh1, w_r, wvt, wiwt, rq, ri, widths)
        ya = _attention(q, k, vt, iq, ik, iwt, top_k)
        prep = _ssm_prepare(ssm_lam_re[l], ssm_lam_im[l], ssm_log_step[l], ssm_b_re[l], ssm_b_im[l],
                            ssm_c_re[l], ssm_c_im[l], ssm_d[l], SSM_CHUNK)
        ys = _ssm(us, prep, SSM_CHUNK)
        x1, u2, ids, gates = _mix(x, ya, ys, up, g1, sc2, sh2, ln1_g[l][None], ln1_b[l][None],
                                  w_out[l].astype(BF16), ssm_w_glu[l].astype(BF16), ssm_b_glu[l][None],
                                  pool_w[l].astype(BF16), pool_scale[l][None], wr_t, alpha)
        pos, src_token, tile_expert, tile_active = _dispatch_plan(ids, n_tok)
        as_rows = lambda a: a.reshape(-1, ROW_SUBLANES, d // ROW_SUBLANES)
        xg = _gather_rows(as_rows(u2), src_token).reshape(-1, d)
        yg = _experts(xg, tile_expert, tile_active, e_gate[l].astype(BF16), e_up[l].astype(BF16),
                      e_down[l].astype(BF16))
        z = _gather_rows(as_rows(yg), pos).reshape(2, bsz, seq, d)
        x = _combine(x1, z, gates.transpose(0, 2, 1), g2, ln2_g[l][None], ln2_b[l][None], alpha)
    return x
```

```python
import functools
import math

import jax
import jax.numpy as jnp
from jax import lax
from jax.experimental import pallas as pl
from jax.experimental.pallas import tpu as pltpu

F32 = jnp.float32
BF16 = jnp.bfloat16

HEAD_DIM = 128
KV_HEADS = 2
IDX_HEADS = 8
IDX_DIM = 64
MAX_TOPK = 256
SSM_GROUP_CH = 16
SSM_STATE = 64
POOL_WINDOWS = (2, 4, 8, 16)
ROPE_THETA = 500000.0
ROPE_FRAC = 4
LN_EPS = 1e-5
N_EXPERTS = 16
N_EXPERT_GROUPS = 4
EXPERTS_PER_GROUP = N_EXPERTS // N_EXPERT_GROUPS

LANES = 128
VMEM_LIMIT = 56 * 1024 * 1024
ROW_TILE = 256
ATTN_Q_TILE = 128
ATTN_KEY_STEP = 512
BISECT_STEPS = 14
REDUCE_SLAB = 64
ROW_SUBLANES = 16
GATHER_SRC_ROWS = 10240
SSM_CHUNK = 32
EXPERT_TILE = 256
CONVERT_ROWS = 128
COPY_ROWS = 256
ADA_COLS = 1024


def _params(*sem):
    return pltpu.CompilerParams(dimension_semantics=sem, vmem_limit_bytes=VMEM_LIMIT)


def _sigmoid(x):
    return 1.0 / (1.0 + jnp.exp(-x))


def _ada_kernel(c_ref, w_ref, b_ref, o_ref):
    c = c_ref[...]
    cond = (c * _sigmoid(c)).astype(BF16)
    o_ref[0] = jnp.dot(cond, w_ref[0].astype(BF16), preferred_element_type=F32) + b_ref[0]


def _ada_all(c_pad, w_ada, b_ada):
    depth, d, n6 = w_ada.shape
    rows = c_pad.shape[0]
    return pl.pallas_call(
        _ada_kernel,
        grid=(depth, n6 // ADA_COLS),
        in_specs=[pl.BlockSpec((rows, d), lambda l, j: (0, 0)),
                  pl.BlockSpec((1, d, ADA_COLS), lambda l, j: (l, 0, j)),
                  pl.BlockSpec((1, 1, ADA_COLS), lambda l, j: (l, 0, j))],
        out_specs=pl.BlockSpec((1, rows, ADA_COLS), lambda l, j: (l, 0, j)),
        out_shape=jax.ShapeDtypeStruct((depth, rows, n6), F32),
        compiler_params=_params("arbitrary", "arbitrary"),
        name="ada",
    )(c_pad, w_ada, b_ada.reshape(depth, 1, n6))


def _rope(xv, tab, half):
    w = xv.shape[1]
    rep = w // LANES
    c = jnp.tile(tab[:, 0:LANES], (1, rep))
    s1 = jnp.tile(tab[:, LANES:2 * LANES], (1, rep))
    s2 = jnp.tile(tab[:, 2 * LANES:3 * LANES], (1, rep))
    return xv * c + pltpu.roll(xv, w - half, 1) * s1 + pltpu.roll(xv, half, 1) * s2


def _inproj_kernel(x_ref, sc_ref, sh_ref, w_ref, wvt_ref, wiwt_ref, rq_ref, ri_ref,
                   q_ref, k_ref, vt_ref, iq_ref, ik_ref, iwt_ref, us_ref, up_ref, *, widths, idx_scale):
    aw, kw, iqw, sw, pw = widths
    u = (x_ref[0] * (1.0 + sc_ref[0]) + sh_ref[0]).astype(BF16)
    rq = rq_ref[0]
    ri = ri_ref[0]

    def mm(lo, n):
        return jnp.dot(u, w_ref[:, lo:lo + n], preferred_element_type=F32)

    def mm_t(wt_ref):
        return lax.dot_general(wt_ref[...], u, (((1,), (1,)), ((), ())), preferred_element_type=F32)

    qhalf = HEAD_DIM // ROPE_FRAC // 2
    ihalf = IDX_DIM // ROPE_FRAC // 2
    o = 0
    q_ref[0] = (_rope(mm(o, aw), rq, qhalf) * (HEAD_DIM ** -0.5)).astype(BF16)
    o += aw
    k_ref[0] = _rope(mm(o, kw), rq, qhalf).astype(BF16)
    o += kw
    iq_ref[0] = _rope(mm(o, iqw), ri, ihalf).astype(BF16)
    o += iqw
    us_ref[0] = mm(o, sw)
    o += sw
    up_ref[0] = mm(o, pw)
    o += pw
    ik_ref[0] = _rope(mm(o, LANES), ri, ihalf)[:, 0:IDX_DIM].astype(BF16)
    vt_ref[0] = mm_t(wvt_ref).astype(BF16)
    iwt_ref[0] = mm_t(wiwt_ref) * idx_scale


def _inproj(x, sc, sh, w_r, wvt, wiwt, rq, ri, widths):
    bsz, seq, d = x.shape
    aw, kw, iqw, sw, pw = widths
    tm = min(ROW_TILE, seq)
    tile = lambda n: pl.BlockSpec((1, tm, n), lambda b, i: (b, i, 0))
    tile_t = lambda n: pl.BlockSpec((1, n, tm), lambda b, i: (b, 0, i))
    vec = pl.BlockSpec((1, 1, d), lambda b, i: (b, 0, 0))
    full = lambda a: pl.BlockSpec(a.shape, lambda b, i: (0, 0))
    sds = jax.ShapeDtypeStruct
    return pl.pallas_call(
        functools.partial(_inproj_kernel, widths=widths, idx_scale=(IDX_DIM ** -0.5) * (IDX_HEADS ** -0.5)),
        grid=(bsz, seq // tm),
        in_specs=[tile(d), vec, vec, full(w_r), full(wvt), full(wiwt), tile(3 * LANES), tile(3 * LANES)],
        out_specs=[tile(aw), tile(kw), tile_t(kw), tile(iqw), tile(IDX_DIM), tile_t(IDX_HEADS), tile(sw), tile(pw)],
        out_shape=[sds((bsz, seq, aw), BF16), sds((bsz, seq, kw), BF16), sds((bsz, kw, seq), BF16),
                   sds((bsz, seq, iqw), BF16), sds((bsz, seq, IDX_DIM), BF16), sds((bsz, IDX_HEADS, seq), F32),
                   sds((bsz, seq, sw), F32), sds((bsz, seq, pw), F32)],
        compiler_params=_params("parallel", "arbitrary"),
        name="inproj",
    )(x, sc, sh, w_r, wvt, wiwt, rq, ri)


def _col_reduce(op, v):
    rows, n = v.shape
    slab = op(v.reshape(rows // REDUCE_SLAB, REDUCE_SLAB, n), axis=0)
    return op(slab, axis=0, keepdims=True)


def _attn_kernel(q_ref, iq_ref, iwt_ref, k_ref, vt_ref, ik_ref, o_ref, mask_ref, *, q0, tq, keys, top_k):
    i = pl.program_id(1)
    t = q0 + i * tq + lax.broadcasted_iota(jnp.int32, (1, tq), 1)
    kpos = lax.broadcasted_iota(jnp.int32, (keys, 1), 0)
    causal = kpos <= t

    iq = iq_ref[0]
    ik = ik_ref[0]
    iwt = iwt_ref[0]
    isc = jnp.zeros((keys, tq), F32)
    for h in range(IDX_HEADS):
        r = lax.dot_general(ik, iq[:, h * IDX_DIM:(h + 1) * IDX_DIM], (((1,), (1,)), ((), ())),
                            preferred_element_type=F32)
        isc = isc + jnp.maximum(r, 0.0) * iwt[h:h + 1, :]
    x = jnp.where(causal, isc, -jnp.inf)

    kf = float(top_k)
    n_valid = (t + 1).astype(F32)
    need = n_valid > kf

    def count_ge(theta):
        return _col_reduce(jnp.sum, jnp.where(x >= theta, 1.0, 0.0))

    rmax = _col_reduce(jnp.max, x)
    rmin = _col_reduce(jnp.min, jnp.where(causal, isc, jnp.inf))
    c_max = count_ge(rmax)
    top_tie = c_max >= kf
    lo0 = jnp.where(need, jnp.where(top_tie, rmax, rmin), -jnp.inf)
    clo0 = jnp.where(top_tie, c_max, n_valid)
    done0 = jnp.where(need & jnp.logical_not(top_tie) & (clo0 != kf), 0.0, 1.0)

    def bisect(lo, hi, clo, chi, done):
        mid = 0.5 * lo + 0.5 * hi
        c = count_ge(mid)
        live = done == 0.0
        ge = c >= kf
        up = live & ge
        dn = live & jnp.logical_not(ge)
        lo = jnp.where(up, mid, lo)
        clo = jnp.where(up, c, clo)
        hi = jnp.where(dn, mid, hi)
        chi = jnp.where(dn, c, chi)
        done = jnp.where(clo == kf, 1.0, done)
        return lo, hi, clo, chi, done

    def n_active(done):
        return jnp.sum(1.0 - done).astype(jnp.int32)

    def body1(s):
        _, it, lo, hi, clo, chi, done = s
        lo, hi, clo, chi, done = bisect(lo, hi, clo, chi, done)
        return n_active(done), it + 1, lo, hi, clo, chi, done

    def body2(s):
        _, it, lo, hi, clo, chi, done = s
        lo, hi, clo, chi, done = bisect(lo, hi, clo, chi, done)
        inb = (x >= lo) & (x < hi)
        vmin = _col_reduce(jnp.min, jnp.where(inb, x, jnp.inf))
        vmax = _col_reduce(jnp.max, jnp.where(inb, x, -jnp.inf))
        c2 = count_ge(vmax)
        live = done == 0.0
        single = vmin == vmax
        top_ok = c2 >= kf
        take_top = live & jnp.logical_not(single) & top_ok
        drop_top = live & jnp.logical_not(single) & jnp.logical_not(top_ok)
        lo = jnp.where(live, jnp.where(take_top, vmax, vmin), lo)
        clo = jnp.where(take_top, c2, clo)
        hi = jnp.where(drop_top, vmax, hi)
        chi = jnp.where(drop_top, c2, chi)
        done = jnp.where(live & (single | top_ok), 1.0, done)
        return n_active(done), it + 1, lo, hi, clo, chi, done

    state = (n_active(done0), jnp.int32(0), lo0, rmax, clo0, c_max, done0)
    state = lax.while_loop(lambda s: (s[0] > 0) & (s[1] < BISECT_STEPS), body1, state)
    state = lax.while_loop(lambda s: s[0] > 0, body2, state)
    _, _, lo, _, clo, _, _ = state

    mask_ref[...] = jnp.where(causal & (x >= lo), 0.0, -1e30)

    tie_q = jnp.sum(jnp.where(need & (clo > kf), 1.0, 0.0)).astype(jnp.int32)

    @pl.when(tie_q > 0)
    def _():
        room = kf - jnp.sum(jnp.where(x > lo, 1.0, 0.0), axis=0, keepdims=True)
        step = 256
        rr = lax.broadcasted_iota(jnp.int32, (step, step), 0)
        cc = lax.broadcasted_iota(jnp.int32, (step, step), 1)
        before = jnp.where(cc < rr, 1.0, 0.0).astype(BF16)
        carry = jnp.zeros((1, tq), F32)
        for c0 in range(0, keys, step):
            xs = x[c0:c0 + step, :]
            e = jnp.where(xs == lo, 1.0, 0.0)
            rank = carry + jnp.dot(before, e.astype(BF16), preferred_element_type=F32)
            keep = (xs > lo) | ((xs == lo) & (rank < room))
            mask_ref[c0:c0 + step, :] = jnp.where(keep & (xs > -jnp.inf), 0.0, -1e30)
            carry = carry + jnp.sum(e, axis=0, keepdims=True)

    bias = mask_ref[...]
    q = q_ref[0]
    group = q.shape[1] // HEAD_DIM // KV_HEADS
    bias_g = jnp.concatenate([bias] * group, axis=1)
    for g in range(KV_HEADS):
        qg = jnp.concatenate([q[:, (g * group + j) * HEAD_DIM:(g * group + j + 1) * HEAD_DIM]
                              for j in range(group)], axis=0)
        kg = k_ref[0, :, g * HEAD_DIM:(g + 1) * HEAD_DIM]
        vtg = vt_ref[0, g * HEAD_DIM:(g + 1) * HEAD_DIM, :]
        s = lax.dot_general(kg, qg, (((1,), (1,)), ((), ())), preferred_element_type=F32) + bias_g
        m = _col_reduce(jnp.max, s)
        p = jnp.exp(s - m)
        den = _col_reduce(jnp.sum, p)
        og = jnp.dot(vtg, p.astype(BF16), preferred_element_type=F32) / den
        for j in range(group):
            h = g * group + j
            o_ref[0, :, h * HEAD_DIM:(h + 1) * HEAD_DIM] = og[:, j * tq:(j + 1) * tq].T.astype(BF16)


def _attention(q, k, vt, iq, ik, iwt, top_k):
    bsz, seq, aw = q.shape
    tq = min(ATTN_Q_TILE, seq)
    step = min(ATTN_KEY_STEP, seq)
    outs = []
    for q0 in range(0, seq, step):
        keys = q0 + step
        nq = step // tq
        qtile = lambda n, q0=q0: pl.BlockSpec((1, tq, n), lambda b, i: (b, q0 // tq + i, 0))
        ktile = lambda n, keys=keys: pl.BlockSpec((1, keys, n), lambda b, i: (b, 0, 0))
        outs.append(pl.pallas_call(
            functools.partial(_attn_kernel, q0=q0, tq=tq, keys=keys, top_k=top_k),
            grid=(bsz, nq),
            in_specs=[qtile(aw), qtile(iq.shape[2]),
                      pl.BlockSpec((1, iwt.shape[1], tq), lambda b, i, q0=q0: (b, 0, q0 // tq + i)),
                      ktile(k.shape[2]),
                      pl.BlockSpec((1, vt.shape[1], keys), lambda b, i: (b, 0, 0)),
                      ktile(ik.shape[2])],
            out_specs=pl.BlockSpec((1, tq, aw), lambda b, i: (b, i, 0)),
            out_shape=jax.ShapeDtypeStruct((bsz, step, aw), BF16),
            scratch_shapes=[pltpu.VMEM((keys, tq), F32)],
            compiler_params=_params("parallel", "arbitrary"),
            name=f"attn_k{keys}",
        )(q, iq, iwt, k, vt, ik))
    return jnp.concatenate(outs, axis=1)


def _ssm_kernel(u_ref, kin_ref, wre_ref, wim_ref, vre_ref, vim_ref, are_ref, aim_ref, y_ref,
                sre_ref, sim_ref, xre_ref, xim_ref, *, bsz, n_chunks):
    pair = u_ref.shape[0]
    p = wre_ref.shape[2]
    for j in range(pair):
        u = u_ref[j]
        sre_ref[:, j * p:(j + 1) * p] = jnp.dot(u, wre_ref[j], preferred_element_type=F32)
        sim_ref[:, j * p:(j + 1) * p] = jnp.dot(u, wim_ref[j], preferred_element_type=F32)
    a_re = are_ref[0]
    a_im = aim_ref[0]
    x_re = jnp.zeros((bsz, pair * p), F32)
    x_im = jnp.zeros((bsz, pair * p), F32)
    for c in range(n_chunks):
        rows = slice(c * bsz, (c + 1) * bsz)
        xre_ref[rows, :] = x_re
        xim_ref[rows, :] = x_im
        n_re = a_re * x_re - a_im * x_im + sre_ref[rows, :]
        n_im = a_re * x_im + a_im * x_re + sim_ref[rows, :]
        x_re, x_im = n_re, n_im
    for j in range(pair):
        xr = xre_ref[:, j * p:(j + 1) * p].astype(BF16)
        xi = xim_ref[:, j * p:(j + 1) * p].astype(BF16)
        y_ref[j] = (jnp.dot(u_ref[j], kin_ref[j], preferred_element_type=F32)
                    + jnp.dot(xr, vre_ref[j], preferred_element_type=F32)
                    + jnp.dot(xi, vim_ref[j], preferred_element_type=F32))


def _ssm_prepare(lam_re, lam_im, log_step, b_re, b_im, c_re, c_im, d_skip, chunk):
    g, p = lam_re.shape
    ch = b_re.shape[2]
    hp = lax.Precision.HIGHEST
    step = jnp.exp(log_step)[:, None]
    lsr, lsi = lam_re * step, lam_im * step
    er = jnp.exp(lsr)
    nr, ni = er * jnp.cos(lsi) - 1.0, er * jnp.sin(lsi)
    den = lam_re * lam_re + lam_im * lam_im
    fr, fi = (nr * lam_re + ni * lam_im) / den, (ni * lam_re - nr * lam_im) / den
    bbr = fr[..., None] * b_re - fi[..., None] * b_im
    bbi = fr[..., None] * b_im + fi[..., None] * b_re
    n = jnp.arange(chunk + 1, dtype=F32)
    mag = jnp.exp(lsr[..., None] * n)
    pr, pi = mag * jnp.cos(lsi[..., None] * n), mag * jnp.sin(lsi[..., None] * n)
    prt, pit = pr[:, :, :chunk, None], pi[:, :, :chunk, None]
    wr = prt * bbr[:, :, None, :] - pit * bbi[:, :, None, :]
    wi = prt * bbi[:, :, None, :] + pit * bbr[:, :, None, :]
    ker = (jnp.einsum("gcp,gptd->gtcd", c_re, wr, precision=hp)
           - jnp.einsum("gcp,gptd->gtcd", c_im, wi, precision=hp))
    lag = jnp.arange(chunk)[None, :] - jnp.arange(chunk)[:, None]
    kexp = jnp.where((lag >= 0)[None, :, :, None, None], ker[:, jnp.clip(lag, 0, chunk - 1)], 0.0)
    kin = kexp.transpose(0, 1, 4, 2, 3)
    eye_t = jnp.eye(chunk, dtype=F32)
    eye_c = jnp.eye(ch, dtype=F32)
    skip = d_skip.reshape(g, ch)
    kin = kin + (eye_t[None, :, None, :, None] * eye_c[None, None, :, None, :] * skip[:, None, :, None, None])
    kin = kin.reshape(g, chunk * ch, chunk * ch)
    to_in = lambda w: w[:, :, ::-1, :].transpose(0, 2, 3, 1).reshape(g, chunk * ch, p)
    pr1, pi1 = pr[:, None, :, 1:], pi[:, None, :, 1:]
    vr = c_re[..., None] * pr1 - c_im[..., None] * pi1
    vi = c_re[..., None] * pi1 + c_im[..., None] * pr1
    to_out = lambda v: v.transpose(0, 2, 3, 1).reshape(g, p, chunk * ch)
    pair = 2
    a_re = pr[..., chunk].reshape(g // pair, 1, pair * p)
    a_im = pi[..., chunk].reshape(g // pair, 1, pair * p)
    return (kin.astype(BF16), to_in(wr).astype(BF16), to_in(wi).astype(BF16),
            to_out(vr).astype(BF16), to_out(-vi).astype(BF16), a_re, a_im)


def _ssm(us, prep, chunk):
    bsz, seq, w = us.shape
    kin, wre, wim, vre, vim, a_re, a_im = prep
    g, p = wre.shape[0], wre.shape[2]
    ch = w // g
    nc = seq // chunk
    rows = nc * bsz
    cols = chunk * ch
    pair = 2
    u = us.reshape(bsz, nc, chunk, g, ch).transpose(3, 1, 0, 2, 4).reshape(g, rows, cols).astype(BF16)
    blk = lambda a, b: pl.BlockSpec((pair, a, b), lambda i: (i, 0, 0))
    y = pl.pallas_call(
        functools.partial(_ssm_kernel, bsz=bsz, n_chunks=nc),
        grid=(g // pair,),
        in_specs=[blk(rows, cols), blk(cols, cols), blk(cols, p), blk(cols, p), blk(p, cols), blk(p, cols),
                  pl.BlockSpec((1, 1, pair * p), lambda i: (i, 0, 0)),
                  pl.BlockSpec((1, 1, pair * p), lambda i: (i, 0, 0))],
        out_specs=blk(rows, cols),
        out_shape=jax.ShapeDtypeStruct((g, rows, cols), F32),
        scratch_shapes=[pltpu.VMEM((rows, pair * p), F32)] * 4,
        compiler_params=_params("parallel"),
        name="ssm",
    )(u, kin, wre, wim, vre, vim, a_re, a_im)
    return y.reshape(g, nc, bsz, chunk, ch).transpose(2, 1, 3, 0, 4).reshape(bsz, seq, w)


def _layer_norm(h, g, b):
    mu = jnp.mean(h, axis=1, keepdims=True)
    hc = h - mu
    var = jnp.mean(hc * hc, axis=1, keepdims=True)
    return hc * lax.rsqrt(var + LN_EPS) * g + b


def _route(logits_t):
    ng, per = N_EXPERT_GROUPS, EXPERTS_PER_GROUP
    m = jnp.max(logits_t, axis=0, keepdims=True)
    e = jnp.exp(logits_t - m)
    prob = e / jnp.sum(e, axis=0, keepdims=True)
    v = [prob[j * ng:(j + 1) * ng, :] for j in range(per)]

    def top(vals):
        best = functools.reduce(jnp.maximum, vals)
        idx = jnp.full(best.shape, per - 1, jnp.int32)
        for j in range(per - 2, -1, -1):
            idx = jnp.where(vals[j] == best, j, idx)
        return best, idx

    m1, i1 = top(v)
    m2, i2 = top([jnp.where(i1 == j, -1.0, v[j]) for j in range(per)])
    score = m1 + m2
    gid = lax.broadcasted_iota(jnp.int32, score.shape, 0)
    best_g = jnp.min(jnp.where(score == jnp.max(score, axis=0, keepdims=True), gid, ng), axis=0, keepdims=True)
    sel = gid == best_g
    pick_f = lambda a: jnp.sum(jnp.where(sel, a, 0.0), axis=0, keepdims=True)
    pick_i = lambda a: jnp.sum(jnp.where(sel, a, 0), axis=0, keepdims=True)
    p1, p2 = pick_f(m1), pick_f(m2)
    e1, e2 = best_g * per + pick_i(i1), best_g * per + pick_i(i2)
    den = p1 + p2
    return jnp.concatenate([e1, e2], axis=0), jnp.concatenate([p1 / den, p2 / den], axis=0)


def _mix_kernel(x_ref, ya_ref, ys_ref, upc_ref, upp_ref, g1_ref, sc2_ref, sh2_ref, lng_ref, lnb_ref,
                wout_ref, wglu_ref, bglu_ref, wpool_ref, pscale_ref, wr_ref,
                x1_ref, u2_ref, ids_ref, gates_ref, *, alpha):
    i = pl.program_id(1)
    tm = x_ref.shape[1]
    aw = ya_ref.shape[2]
    sw = ys_ref.shape[2]
    y = ys_ref[0]
    y = 0.5 * y * (1.0 + jnp.tanh(math.sqrt(2.0 / math.pi) * (y + 0.044715 * (y * y * y))))
    z = jnp.dot(y.astype(BF16), wglu_ref[...], preferred_element_type=F32) + bglu_ref[...]
    y = y * _sigmoid(z)
    upc = upc_ref[0]
    upp = jnp.where(i > 0, upp_ref[0], 0.0)
    cat = jnp.concatenate([upp, upc], axis=0).astype(BF16)
    r = lax.broadcasted_iota(jnp.int32, (tm, 1), 0)
    lagm = (r + tm) - lax.broadcasted_iota(jnp.int32, (1, 2 * tm), 1)
    tpos = (i * tm + r + 1).astype(F32)
    gc = upc.shape[1] // len(POOL_WINDOWS)
    pooled = []
    for g, win in enumerate(POOL_WINDOWS):
        band = jnp.where((lagm >= 0) & (lagm < win), 1.0, 0.0).astype(BF16)
        ws = jnp.dot(band, cat[:, g * gc:(g + 1) * gc], preferred_element_type=F32)
        pg = ws / jnp.minimum(tpos, float(win)) - upc[:, g * gc:(g + 1) * gc]
        pooled.append(jnp.dot(pg.astype(BF16), wpool_ref[g], preferred_element_type=F32))
    yp = jnp.concatenate(pooled, axis=1) * pscale_ref[...]
    mix = (jnp.dot(ya_ref[0], wout_ref[0:aw, :], preferred_element_type=F32)
           + jnp.dot(y.astype(BF16), wout_ref[aw:aw + sw, :], preferred_element_type=F32)
           + jnp.dot(yp.astype(BF16), wout_ref[aw + sw:, :], preferred_element_type=F32))
    x1 = _layer_norm(alpha * x_ref[0] + (1.0 + g1_ref[0]) * mix, lng_ref[...], lnb_ref[...])
    x1_ref[0] = x1
    u2 = x1 * (1.0 + sc2_ref[0]) + sh2_ref[0]
    u2_ref[0] = u2.astype(BF16)
    logits = jnp.dot(u2, wr_ref[...], precision=lax.Precision.HIGHEST, preferred_element_type=F32)
    ids, gates = _route(logits.T[0:N_EXPERTS, :])
    ids_ref[0] = ids
    gates_ref[0] = gates


def _mix(x, ya, ys, up, g1, sc2, sh2, lng, lnb, wout, wglu, bglu, wpool, pscale, wr_pad, alpha):
    bsz, seq, d = x.shape
    tm = min(ROW_TILE, seq)
    tile = lambda n: pl.BlockSpec((1, tm, n), lambda b, i: (b, i, 0))
    prev = pl.BlockSpec((1, tm, up.shape[2]), lambda b, i: (b, jnp.maximum(i - 1, 0), 0))
    vec = pl.BlockSpec((1, 1, d), lambda b, i: (b, 0, 0))
    full = lambda a: pl.BlockSpec(a.shape, lambda b, i: (0,) * a.ndim)
    lane_rows = pl.BlockSpec((1, 2, tm), lambda b, i: (b, 0, i))
    return pl.pallas_call(
        functools.partial(_mix_kernel, alpha=alpha),
        grid=(bsz, seq // tm),
        in_specs=[tile(d), tile(ya.shape[2]), tile(ys.shape[2]), tile(up.shape[2]), prev, vec, vec, vec,
                  full(lng), full(lnb), full(wout), full(wglu), full(bglu), full(wpool), full(pscale), full(wr_pad)],
        out_specs=[tile(d), tile(d), lane_rows, lane_rows],
        out_shape=[jax.ShapeDtypeStruct((bsz, seq, d), F32), jax.ShapeDtypeStruct((bsz, seq, d), BF16),
                   jax.ShapeDtypeStruct((bsz, 2, seq), jnp.int32), jax.ShapeDtypeStruct((bsz, 2, seq), F32)],
        compiler_params=_params("parallel", "arbitrary"),
        name="mix",
    )(x, ya, ys, up, up, g1, sc2, sh2, lng, lnb, wout, wglu, bglu, wpool, pscale, wr_pad)


def _gather_kernel(idx_ref, src_ref, *rest, lo, n_src, has_prev):
    if has_prev:
        prev_ref, out_ref, buf, sem = rest
    else:
        out_ref, buf, sem = rest

    @pl.when(pl.program_id(0) == 0)
    def _():
        cp = pltpu.make_async_copy(src_ref.at[pl.ds(lo, n_src)], buf, sem)
        cp.start()
        cp.wait()

    base = pl.program_id(0) * COPY_ROWS

    def move(r, carry):
        s = idx_ref[base + r] - lo
        row = buf[jnp.clip(s, 0, n_src - 1)]
        if has_prev:
            row = jnp.where((s >= 0) & (s < n_src), row, prev_ref[r])
        out_ref[r] = row
        return carry

    lax.fori_loop(0, COPY_ROWS, move, 0, unroll=8)


def _gather_rows(src, idx):
    n = idx.shape[0]
    n_chunks = pl.cdiv(src.shape[0], GATHER_SRC_ROWS)
    n_src = src.shape[0] // n_chunks
    assert n_src * n_chunks == src.shape[0]
    block = pl.BlockSpec((COPY_ROWS,) + src.shape[1:], lambda i, idx: (i, 0, 0))
    out = None
    for c in range(n_chunks):
        has_prev = c > 0
        out = pl.pallas_call(
            functools.partial(_gather_kernel, lo=c * n_src, n_src=n_src, has_prev=has_prev),
            grid_spec=pltpu.PrefetchScalarGridSpec(
                num_scalar_prefetch=1, grid=(n // COPY_ROWS,),
                in_specs=[pl.BlockSpec(memory_space=pl.ANY)] + ([block] if has_prev else []),
                out_specs=block,
                scratch_shapes=[pltpu.VMEM((n_src,) + src.shape[1:], src.dtype), pltpu.SemaphoreType.DMA(())]),
            out_shape=jax.ShapeDtypeStruct((n,) + src.shape[1:], src.dtype),
            compiler_params=_params("arbitrary"),
            name="gather",
        )(*((idx, src, out) if has_prev else (idx, src)))
    return out


def _expert_kernel(te_ref, act_ref, first_ref, nxt_ref, x_ref, wg_hbm, wu_hbm, wd_hbm, y_ref,
                   stage_g, stage_u, stage_d, wg_ref, wu_ref, wd_ref, sems, *, layer):
    i = pl.program_id(0)

    def fetch(e):
        return (pltpu.make_async_copy(wg_hbm.at[layer, e], stage_g, sems.at[0]),
                pltpu.make_async_copy(wu_hbm.at[layer, e], stage_u, sems.at[1]),
                pltpu.make_async_copy(wd_hbm.at[layer, e], stage_d, sems.at[2]))

    @pl.when(i == 0)
    def _():
        for cp in fetch(te_ref[0]):
            cp.start()

    @pl.when(first_ref[i] != 0)
    def _():
        for cp in fetch(te_ref[i]):
            cp.wait()
        for stage, work in ((stage_g, wg_ref), (stage_u, wu_ref), (stage_d, wd_ref)):
            rows = stage.shape[0]

            def convert(r, carry, stage=stage, work=work):
                sl = pl.ds(pl.multiple_of(r * CONVERT_ROWS, CONVERT_ROWS), CONVERT_ROWS)
                work[sl, :] = stage[sl, :].astype(BF16)
                return carry

            lax.fori_loop(0, rows // CONVERT_ROWS, convert, 0)

        @pl.when(nxt_ref[i] >= 0)
        def _():
            for cp in fetch(nxt_ref[i]):
                cp.start()

    @pl.when(act_ref[i] != 0)
    def _():
        x = x_ref[...]
        g = jnp.dot(x, wg_ref[...], preferred_element_type=F32)
        u = jnp.dot(x, wu_ref[...], preferred_element_type=F32)
        h = (g * _sigmoid(g)) * u
        y_ref[...] = jnp.dot(h.astype(BF16), wd_ref[...], preferred_element_type=F32).astype(BF16)

    @pl.when(act_ref[i] == 0)
    def _():
        y_ref[...] = jnp.zeros(y_ref.shape, y_ref.dtype)


def _experts(xg, plan, wg, wu, wd, layer):
    rows, d = xg.shape
    f = wg.shape[3]
    tm = EXPERT_TILE
    tile = pl.BlockSpec((tm, d), lambda i, *_: (i, 0))
    hbm = pl.BlockSpec(memory_space=pl.ANY)
    return pl.pallas_call(
        functools.partial(_expert_kernel, layer=layer),
        grid_spec=pltpu.PrefetchScalarGridSpec(
            num_scalar_prefetch=4, grid=(rows // tm,),
            in_specs=[tile, hbm, hbm, hbm],
            out_specs=tile,
            scratch_shapes=[pltpu.VMEM((d, f), F32), pltpu.VMEM((d, f), F32), pltpu.VMEM((f, d), F32),
                            pltpu.VMEM((d, f), BF16), pltpu.VMEM((d, f), BF16), pltpu.VMEM((f, d), BF16),
                            pltpu.SemaphoreType.DMA((3,))]),
        out_shape=jax.ShapeDtypeStruct((rows, d), BF16),
        compiler_params=_params("arbitrary"),
        name="experts",
    )(*plan, xg, wg, wu, wd)


def _combine_kernel(x_ref, z_ref, gc_ref, g2_ref, lng_ref, lnb_ref, o_ref, *, alpha):
    gc = gc_ref[0]
    ffn = gc[:, 0:1] * z_ref[0, 0].astype(F32) + gc[:, 1:2] * z_ref[1, 0].astype(F32)
    o_ref[0] = _layer_norm(alpha * x_ref[0] + (1.0 + g2_ref[0]) * ffn, lng_ref[...], lnb_ref[...])


def _combine(x1, z, gates_col, g2, lng, lnb, alpha):
    bsz, seq, d = x1.shape
    tm = min(ROW_TILE, seq)
    tile = pl.BlockSpec((1, tm, d), lambda b, i: (b, i, 0))
    return pl.pallas_call(
        functools.partial(_combine_kernel, alpha=alpha),
        grid=(bsz, seq // tm),
        in_specs=[tile, pl.BlockSpec((2, 1, tm, d), lambda b, i: (0, b, i, 0)),
                  pl.BlockSpec((1, tm, 2), lambda b, i: (b, i, 0)),
                  pl.BlockSpec((1, 1, d), lambda b, i: (b, 0, 0)),
                  pl.BlockSpec(lng.shape, lambda b, i: (0, 0)), pl.BlockSpec(lnb.shape, lambda b, i: (0, 0))],
        out_specs=tile,
        out_shape=jax.ShapeDtypeStruct((bsz, seq, d), F32),
        compiler_params=_params("parallel", "arbitrary"),
        name="combine",
    )(x1, z, gates_col, g2, lng, lnb)


def _dispatch_plan(ids, n_tok):
    tm = EXPERT_TILE
    e_pair = jnp.concatenate([ids[:, 0, :].reshape(n_tok), ids[:, 1, :].reshape(n_tok)])
    tok_pair = jnp.concatenate([jnp.arange(n_tok, dtype=jnp.int32)] * 2)
    onehot = (e_pair[:, None] == jnp.arange(N_EXPERTS, dtype=jnp.int32)[None, :]).astype(jnp.int32)
    csum = jnp.cumsum(onehot, axis=0)
    rank = jnp.sum(onehot * csum, axis=1) - 1
    counts = csum[-1]
    padded = ((counts + tm - 1) // tm) * tm
    ends = jnp.cumsum(padded)
    offs = ends - padded
    pos = jnp.sum(onehot * offs[None, :], axis=1) + rank
    rows = 2 * n_tok + N_EXPERTS * tm
    src_token = jnp.zeros((rows,), jnp.int32).at[pos].set(tok_pair)
    tile_start = jnp.arange(rows // tm, dtype=jnp.int32) * tm
    tile_expert = jnp.minimum(jnp.sum((tile_start[:, None] >= ends[None, :]).astype(jnp.int32), axis=1),
                              N_EXPERTS - 1)
    tile_active = (tile_start < ends[-1]).astype(jnp.int32)
    prev_expert = jnp.concatenate([jnp.full((1,), -1, jnp.int32), tile_expert[:-1]])
    tile_first = tile_active * (tile_expert != prev_expert).astype(jnp.int32)
    eid = jnp.arange(N_EXPERTS, dtype=jnp.int32)
    later = (padded > 0)[None, :] & (eid[None, :] > eid[:, None])
    next_expert = jnp.min(jnp.where(later, eid[None, :], N_EXPERTS), axis=1)
    next_expert = jnp.where(next_expert == N_EXPERTS, -1, next_expert).astype(jnp.int32)
    tile_next = next_expert[tile_expert]
    return pos.astype(jnp.int32), src_token, (tile_expert, tile_active, tile_first, tile_next)


def _rope_tables(positions, dim):
    rot = dim // ROPE_FRAC
    half = rot // 2
    inv_freq = ROPE_THETA ** (-jnp.arange(half, dtype=F32) * 2.0 / rot)
    ang = positions.astype(F32)[..., None] * inv_freq
    cos, sin = jnp.cos(ang), jnp.sin(ang)
    shape = ang.shape[:-1]
    one = jnp.ones(shape + (dim - rot,), F32)
    zero = jnp.zeros(shape + (dim - rot,), F32)
    zh = jnp.zeros(shape + (half,), F32)
    c = jnp.concatenate([cos, cos, one], axis=-1)
    s1 = jnp.concatenate([-sin, zh, zero], axis=-1)
    s2 = jnp.concatenate([zh, sin, zero], axis=-1)
    rep = LANES // dim
    return jnp.concatenate([jnp.tile(a, (1, 1, rep)) for a in (c, s1, s2)], axis=-1)


def kernel(x, c, positions, w_ada, b_ada, w_in, w_out, ssm_lam_re, ssm_lam_im, ssm_log_step, ssm_b_re, ssm_b_im, ssm_c_re, ssm_c_im, ssm_d, ssm_w_glu, ssm_b_glu, pool_w, pool_scale, ln1_g, ln1_b, ln2_g, ln2_b, w_router, e_gate, e_up, e_down):
    bsz, seq, d = x.shape
    depth = w_ada.shape[0]
    n_tok = bsz * seq
    alpha = (2.0 * depth) ** 0.25
    sw = ssm_d.shape[1]
    pw = pool_scale.shape[1]
    aw = w_out.shape[1] - sw - pw
    kw = KV_HEADS * HEAD_DIM
    iqw = IDX_HEADS * IDX_DIM
    widths = (aw, kw, iqw, sw, pw)
    top_k = min(MAX_TOPK, seq // 4)

    c_pad = jnp.concatenate([c, jnp.zeros((8 - bsz % 8, d), F32)], axis=0) if bsz % 8 else c
    ada = _ada_all(c_pad, w_ada, b_ada)
    rq = _rope_tables(positions, HEAD_DIM)
    ri = _rope_tables(positions, IDX_DIM)
    wr_pad = jnp.concatenate(
        [w_router.reshape(d, N_EXPERT_GROUPS, EXPERTS_PER_GROUP).transpose(0, 2, 1).reshape(d, N_EXPERTS),
         jnp.zeros((d, LANES - N_EXPERTS), F32)], axis=1)

    o_q, o_k, o_v, o_iq, o_ik, o_iw, o_us, o_up = (0, aw, aw + kw, aw + 2 * kw, aw + 2 * kw + iqw,
                                                   aw + 2 * kw + iqw + IDX_DIM,
                                                   aw + 2 * kw + iqw + IDX_DIM + IDX_HEADS,
                                                   aw + 2 * kw + iqw + IDX_DIM + IDX_HEADS + sw)
    for l in range(depth):
        sh1, sc1, g1, sh2, sc2, g2 = [ada[l, :bsz, j * d:(j + 1) * d].reshape(bsz, 1, d) for j in range(6)]
        wl = w_in[l]
        w_r = jnp.concatenate([wl[:, o_q:o_v], wl[:, o_iq:o_ik], wl[:, o_us:], wl[:, o_ik:o_iw],
                               jnp.zeros((d, LANES - IDX_DIM), F32)], axis=1).astype(BF16)
        wvt = wl[:, o_v:o_iq].T.astype(BF16)
        wiwt = wl[:, o_iw:o_us].T.astype(BF16)
        q, k, vt, iq, ik, iwt, us, up = _inproj(x, sc1, sh1, w_r, wvt, wiwt, rq, ri, widths)
        ya = _attention(q, k, vt, iq, ik, iwt, top_k)
        prep = _ssm_prepare(ssm_lam_re[l], ssm_lam_im[l], ssm_log_step[l], ssm_b_re[l], ssm_b_im[l],
                            ssm_c_re[l], ssm_c_im[l], ssm_d[l], SSM_CHUNK)
        ys = _ssm(us, prep, SSM_CHUNK)
        x1, u2, ids, gates = _mix(x, ya, ys, up, g1, sc2, sh2, ln1_g[l][None], ln1_b[l][None],
                                  w_out[l].astype(BF16), ssm_w_glu[l].astype(BF16), ssm_b_glu[l][None],
                                  pool_w[l].astype(BF16), pool_scale[l][None], wr_pad, alpha)
        pos, src_token, tile_plan = _dispatch_plan(ids, n_tok)
        as_rows = lambda a: a.reshape(-1, ROW_SUBLANES, d // ROW_SUBLANES)
        xg = _gather_rows(as_rows(u2), src_token).reshape(-1, d)
        yg = _experts(xg, tile_plan, e_gate, e_up, e_down, l)
        z = _gather_rows(as_rows(yg), pos).reshape(2, bsz, seq, d)
        x = _combine(x1, z, gates.transpose(0, 2, 1), g2, ln2_g[l][None], ln2_b[l][None], alpha)
    return x
```

```python
import functools
import math

import jax
import jax.numpy as jnp
from jax import lax
from jax.experimental import pallas as pl
from jax.experimental.pallas import tpu as pltpu

F32 = jnp.float32
BF16 = jnp.bfloat16

HEAD_DIM = 128
KV_HEADS = 2
IDX_HEADS = 8
IDX_DIM = 64
MAX_TOPK = 256
SSM_GROUP_CH = 16
SSM_STATE = 64
POOL_WINDOWS = (2, 4, 8, 16)
ROPE_THETA = 500000.0
ROPE_FRAC = 4
LN_EPS = 1e-5
N_EXPERTS = 16
N_EXPERT_GROUPS = 4
EXPERTS_PER_GROUP = N_EXPERTS // N_EXPERT_GROUPS

LANES = 128
VMEM_LIMIT = 56 * 1024 * 1024
ROW_TILE = 256
ATTN_Q_TILE = 128
ATTN_KEY_STEP = 512
BISECT_STEPS = 14
REDUCE_SLAB = 64
ROW_SUBLANES = 16
GATHER_SRC_ROWS = 10240
SSM_CHUNK = 16
EXPERT_TILE = 256
CONVERT_ROWS = 128
COPY_ROWS = 256
ADA_COLS = 1024


def _params(*sem):
    return pltpu.CompilerParams(dimension_semantics=sem, vmem_limit_bytes=VMEM_LIMIT)


def _sigmoid(x):
    return 1.0 / (1.0 + jnp.exp(-x))


def _ada_kernel(c_ref, w_ref, b_ref, o_ref):
    c = c_ref[...]
    cond = (c * _sigmoid(c)).astype(BF16)
    o_ref[0] = jnp.dot(cond, w_ref[0].astype(BF16), preferred_element_type=F32) + b_ref[0]


def _ada_all(c_pad, w_ada, b_ada):
    depth, d, n6 = w_ada.shape
    rows = c_pad.shape[0]
    return pl.pallas_call(
        _ada_kernel,
        grid=(depth, n6 // ADA_COLS),
        in_specs=[pl.BlockSpec((rows, d), lambda l, j: (0, 0)),
                  pl.BlockSpec((1, d, ADA_COLS), lambda l, j: (l, 0, j)),
                  pl.BlockSpec((1, 1, ADA_COLS), lambda l, j: (l, 0, j))],
        out_specs=pl.BlockSpec((1, rows, ADA_COLS), lambda l, j: (l, 0, j)),
        out_shape=jax.ShapeDtypeStruct((depth, rows, n6), F32),
        compiler_params=_params("arbitrary", "arbitrary"),
        name="ada",
    )(c_pad, w_ada, b_ada.reshape(depth, 1, n6))


def _rope(xv, tab, half):
    w = xv.shape[1]
    rep = w // LANES
    c = jnp.tile(tab[:, 0:LANES], (1, rep))
    s1 = jnp.tile(tab[:, LANES:2 * LANES], (1, rep))
    s2 = jnp.tile(tab[:, 2 * LANES:3 * LANES], (1, rep))
    return xv * c + pltpu.roll(xv, w - half, 1) * s1 + pltpu.roll(xv, half, 1) * s2


def _inproj_kernel(x_ref, sc_ref, sh_ref, w_ref, wvt_ref, wiwt_ref, rq_ref, ri_ref,
                   q_ref, k_ref, vt_ref, iq_ref, ik_ref, iwt_ref, us_ref, up_ref, *, widths, idx_scale):
    aw, kw, iqw, sw, pw = widths
    u = (x_ref[0] * (1.0 + sc_ref[0]) + sh_ref[0]).astype(BF16)
    rq = rq_ref[0]
    ri = ri_ref[0]

    def mm(lo, n):
        return jnp.dot(u, w_ref[:, lo:lo + n], preferred_element_type=F32)

    def mm_t(wt_ref):
        return lax.dot_general(wt_ref[...], u, (((1,), (1,)), ((), ())), preferred_element_type=F32)

    qhalf = HEAD_DIM // ROPE_FRAC // 2
    ihalf = IDX_DIM // ROPE_FRAC // 2
    o = 0
    q_ref[0] = (_rope(mm(o, aw), rq, qhalf) * (HEAD_DIM ** -0.5)).astype(BF16)
    o += aw
    k_ref[0] = _rope(mm(o, kw), rq, qhalf).astype(BF16)
    o += kw
    iq_ref[0] = _rope(mm(o, iqw), ri, ihalf).astype(BF16)
    o += iqw
    us_ref[0] = mm(o, sw).astype(BF16)
    o += sw
    up_ref[0] = mm(o, pw)
    o += pw
    ik_ref[0] = _rope(mm(o, LANES), ri, ihalf)[:, 0:IDX_DIM].astype(BF16)
    vt_ref[0] = mm_t(wvt_ref).astype(BF16)
    iwt_ref[0] = mm_t(wiwt_ref) * idx_scale


def _inproj(x, sc, sh, w_r, wvt, wiwt, rq, ri, widths):
    bsz, seq, d = x.shape
    aw, kw, iqw, sw, pw = widths
    tm = min(ROW_TILE, seq)
    tile = lambda n: pl.BlockSpec((1, tm, n), lambda b, i: (b, i, 0))
    tile_t = lambda n: pl.BlockSpec((1, n, tm), lambda b, i: (b, 0, i))
    vec = pl.BlockSpec((1, 1, d), lambda b, i: (b, 0, 0))
    full = lambda a: pl.BlockSpec(a.shape, lambda b, i: (0, 0))
    sds = jax.ShapeDtypeStruct
    return pl.pallas_call(
        functools.partial(_inproj_kernel, widths=widths, idx_scale=(IDX_DIM ** -0.5) * (IDX_HEADS ** -0.5)),
        grid=(bsz, seq // tm),
        in_specs=[tile(d), vec, vec, full(w_r), full(wvt), full(wiwt), tile(3 * LANES), tile(3 * LANES)],
        out_specs=[tile(aw), tile(kw), tile_t(kw), tile(iqw), tile(IDX_DIM), tile_t(IDX_HEADS), tile(sw), tile(pw)],
        out_shape=[sds((bsz, seq, aw), BF16), sds((bsz, seq, kw), BF16), sds((bsz, kw, seq), BF16),
                   sds((bsz, seq, iqw), BF16), sds((bsz, seq, IDX_DIM), BF16), sds((bsz, IDX_HEADS, seq), F32),
                   sds((bsz, seq, sw), BF16), sds((bsz, seq, pw), F32)],
        compiler_params=_params("parallel", "arbitrary"),
        name="inproj",
    )(x, sc, sh, w_r, wvt, wiwt, rq, ri)


def _col_reduce(op, v):
    rows, n = v.shape
    slab = op(v.reshape(rows // REDUCE_SLAB, REDUCE_SLAB, n), axis=0)
    return op(slab, axis=0, keepdims=True)


def _attn_kernel(q_ref, iq_ref, iwt_ref, k_ref, vt_ref, ik_ref, o_ref, mask_ref, *, q0, tq, keys, top_k):
    i = pl.program_id(1)
    t = q0 + i * tq + lax.broadcasted_iota(jnp.int32, (1, tq), 1)
    kpos = lax.broadcasted_iota(jnp.int32, (keys, 1), 0)
    causal = kpos <= t

    iq = iq_ref[0]
    ik = ik_ref[0]
    iwt = iwt_ref[0]
    isc = jnp.zeros((keys, tq), F32)
    for h in range(IDX_HEADS):
        r = lax.dot_general(ik, iq[:, h * IDX_DIM:(h + 1) * IDX_DIM], (((1,), (1,)), ((), ())),
                            preferred_element_type=F32)
        isc = isc + jnp.maximum(r, 0.0) * iwt[h:h + 1, :]
    x = jnp.where(causal, isc, -jnp.inf)

    kf = float(top_k)
    n_valid = (t + 1).astype(F32)
    need = n_valid > kf

    def count_ge(theta):
        return _col_reduce(jnp.sum, jnp.where(x >= theta, 1.0, 0.0))

    rmax = _col_reduce(jnp.max, x)
    rmin = _col_reduce(jnp.min, jnp.where(causal, isc, jnp.inf))
    c_max = count_ge(rmax)
    top_tie = c_max >= kf
    lo0 = jnp.where(need, jnp.where(top_tie, rmax, rmin), -jnp.inf)
    clo0 = jnp.where(top_tie, c_max, n_valid)
    done0 = jnp.where(need & jnp.logical_not(top_tie) & (clo0 != kf), 0.0, 1.0)

    def bisect(lo, hi, clo, chi, done):
        mid = 0.5 * lo + 0.5 * hi
        c = count_ge(mid)
        live = done == 0.0
        ge = c >= kf
        up = live & ge
        dn = live & jnp.logical_not(ge)
        lo = jnp.where(up, mid, lo)
        clo = jnp.where(up, c, clo)
        hi = jnp.where(dn, mid, hi)
        chi = jnp.where(dn, c, chi)
        done = jnp.where(clo == kf, 1.0, done)
        return lo, hi, clo, chi, done

    def n_active(done):
        return jnp.sum(1.0 - done).astype(jnp.int32)

    def body1(s):
        _, it, lo, hi, clo, chi, done = s
        lo, hi, clo, chi, done = bisect(lo, hi, clo, chi, done)
        return n_active(done), it + 1, lo, hi, clo, chi, done

    def body2(s):
        _, it, lo, hi, clo, chi, done = s
        lo, hi, clo, chi, done = bisect(lo, hi, clo, chi, done)
        inb = (x >= lo) & (x < hi)
        vmin = _col_reduce(jnp.min, jnp.where(inb, x, jnp.inf))
        vmax = _col_reduce(jnp.max, jnp.where(inb, x, -jnp.inf))
        c2 = count_ge(vmax)
        live = done == 0.0
        single = vmin == vmax
        top_ok = c2 >= kf
        take_top = live & jnp.logical_not(single) & top_ok
        drop_top = live & jnp.logical_not(single) & jnp.logical_not(top_ok)
        lo = jnp.where(live, jnp.where(take_top, vmax, vmin), lo)
        clo = jnp.where(take_top, c2, clo)
        hi = jnp.where(drop_top, vmax, hi)
        chi = jnp.where(drop_top, c2, chi)
        done = jnp.where(live & (single | top_ok), 1.0, done)
        return n_active(done), it + 1, lo, hi, clo, chi, done

    state = (n_active(done0), jnp.int32(0), lo0, rmax, clo0, c_max, done0)
    state = lax.while_loop(lambda s: (s[0] > 0) & (s[1] < BISECT_STEPS), body1, state)
    state = lax.while_loop(lambda s: s[0] > 0, body2, state)
    _, _, lo, _, clo, _, _ = state

    mask_ref[...] = jnp.where(causal & (x >= lo), 0.0, -1e30)

    tie_q = jnp.sum(jnp.where(need & (clo > kf), 1.0, 0.0)).astype(jnp.int32)

    @pl.when(tie_q > 0)
    def _():
        room = kf - jnp.sum(jnp.where(x > lo, 1.0, 0.0), axis=0, keepdims=True)
        step = 256
        rr = lax.broadcasted_iota(jnp.int32, (step, step), 0)
        cc = lax.broadcasted_iota(jnp.int32, (step, step), 1)
        before = jnp.where(cc < rr, 1.0, 0.0).astype(BF16)
        carry = jnp.zeros((1, tq), F32)
        for c0 in range(0, keys, step):
            xs = x[c0:c0 + step, :]
            e = jnp.where(xs == lo, 1.0, 0.0)
            rank = carry + jnp.dot(before, e.astype(BF16), preferred_element_type=F32)
            keep = (xs > lo) | ((xs == lo) & (rank < room))
            mask_ref[c0:c0 + step, :] = jnp.where(keep & (xs > -jnp.inf), 0.0, -1e30)
            carry = carry + jnp.sum(e, axis=0, keepdims=True)

    bias = mask_ref[...]
    q = q_ref[0]
    group = q.shape[1] // HEAD_DIM // KV_HEADS
    bias_g = jnp.concatenate([bias] * group, axis=1)
    for g in range(KV_HEADS):
        qg = jnp.concatenate([q[:, (g * group + j) * HEAD_DIM:(g * group + j + 1) * HEAD_DIM]
                              for j in range(group)], axis=0)
        kg = k_ref[0, :, g * HEAD_DIM:(g + 1) * HEAD_DIM]
        vtg = vt_ref[0, g * HEAD_DIM:(g + 1) * HEAD_DIM, :]
        s = lax.dot_general(kg, qg, (((1,), (1,)), ((), ())), preferred_element_type=F32) + bias_g
        m = _col_reduce(jnp.max, s)
        p = jnp.exp(s - m)
        den = _col_reduce(jnp.sum, p)
        og = jnp.dot(vtg, p.astype(BF16), preferred_element_type=F32) / den
        for j in range(group):
            h = g * group + j
            o_ref[0, :, h * HEAD_DIM:(h + 1) * HEAD_DIM] = og[:, j * tq:(j + 1) * tq].T.astype(BF16)


def _attention(q, k, vt, iq, ik, iwt, top_k):
    bsz, seq, aw = q.shape
    tq = min(ATTN_Q_TILE, seq)
    step = min(ATTN_KEY_STEP, seq)
    outs = []
    for q0 in range(0, seq, step):
        keys = q0 + step
        nq = step // tq
        qtile = lambda n, q0=q0: pl.BlockSpec((1, tq, n), lambda b, i: (b, q0 // tq + i, 0))
        ktile = lambda n, keys=keys: pl.BlockSpec((1, keys, n), lambda b, i: (b, 0, 0))
        outs.append(pl.pallas_call(
            functools.partial(_attn_kernel, q0=q0, tq=tq, keys=keys, top_k=top_k),
            grid=(bsz, nq),
            in_specs=[qtile(aw), qtile(iq.shape[2]),
                      pl.BlockSpec((1, iwt.shape[1], tq), lambda b, i, q0=q0: (b, 0, q0 // tq + i)),
                      ktile(k.shape[2]),
                      pl.BlockSpec((1, vt.shape[1], keys), lambda b, i: (b, 0, 0)),
                      ktile(ik.shape[2])],
            out_specs=pl.BlockSpec((1, tq, aw), lambda b, i: (b, i, 0)),
            out_shape=jax.ShapeDtypeStruct((bsz, step, aw), BF16),
            scratch_shapes=[pltpu.VMEM((keys, tq), F32)],
            compiler_params=_params("parallel", "arbitrary"),
            name=f"attn_k{keys}",
        )(q, iq, iwt, k, vt, ik))
    return jnp.concatenate(outs, axis=1)


def _ssm_kernel(u_ref, kin_ref, wre_ref, wim_ref, vre_ref, vim_ref, are_ref, aim_ref, y_ref,
                sre_ref, sim_ref, xre_ref, xim_ref, *, bsz, n_chunks):
    pair = u_ref.shape[0]
    p = wre_ref.shape[2]
    for j in range(pair):
        u = u_ref[j]
        sre_ref[:, j * p:(j + 1) * p] = jnp.dot(u, wre_ref[j], preferred_element_type=F32)
        sim_ref[:, j * p:(j + 1) * p] = jnp.dot(u, wim_ref[j], preferred_element_type=F32)
    a_re = are_ref[0]
    a_im = aim_ref[0]
    x_re = jnp.zeros((bsz, pair * p), F32)
    x_im = jnp.zeros((bsz, pair * p), F32)
    for c in range(n_chunks):
        rows = slice(c * bsz, (c + 1) * bsz)
        xre_ref[rows, :] = x_re
        xim_ref[rows, :] = x_im
        n_re = a_re * x_re - a_im * x_im + sre_ref[rows, :]
        n_im = a_re * x_im + a_im * x_re + sim_ref[rows, :]
        x_re, x_im = n_re, n_im
    for j in range(pair):
        xr = xre_ref[:, j * p:(j + 1) * p].astype(BF16)
        xi = xim_ref[:, j * p:(j + 1) * p].astype(BF16)
        y_ref[j] = (jnp.dot(u_ref[j], kin_ref[j], preferred_element_type=F32)
                    + jnp.dot(xr, vre_ref[j], preferred_element_type=F32)
                    + jnp.dot(xi, vim_ref[j], preferred_element_type=F32)).astype(BF16)


def _ssm_prepare(lam_re, lam_im, log_step, b_re, b_im, c_re, c_im, d_skip, chunk):
    g, p = lam_re.shape
    ch = b_re.shape[2]
    hp = lax.Precision.HIGHEST
    step = jnp.exp(log_step)[:, None]
    lsr, lsi = lam_re * step, lam_im * step
    er = jnp.exp(lsr)
    nr, ni = er * jnp.cos(lsi) - 1.0, er * jnp.sin(lsi)
    den = lam_re * lam_re + lam_im * lam_im
    fr, fi = (nr * lam_re + ni * lam_im) / den, (ni * lam_re - nr * lam_im) / den
    bbr = fr[..., None] * b_re - fi[..., None] * b_im
    bbi = fr[..., None] * b_im + fi[..., None] * b_re
    n = jnp.arange(chunk + 1, dtype=F32)
    mag = jnp.exp(lsr[..., None] * n)
    pr, pi = mag * jnp.cos(lsi[..., None] * n), mag * jnp.sin(lsi[..., None] * n)
    prt, pit = pr[:, :, :chunk, None], pi[:, :, :chunk, None]
    wr = prt * bbr[:, :, None, :] - pit * bbi[:, :, None, :]
    wi = prt * bbi[:, :, None, :] + pit * bbr[:, :, None, :]
    ker = (jnp.einsum("gcp,gptd->gtcd", c_re, wr, precision=hp)
           - jnp.einsum("gcp,gptd->gtcd", c_im, wi, precision=hp))
    lag = jnp.arange(chunk)[None, :] - jnp.arange(chunk)[:, None]
    kexp = jnp.where((lag >= 0)[None, :, :, None, None], ker[:, jnp.clip(lag, 0, chunk - 1)], 0.0)
    kin = kexp.transpose(0, 1, 4, 2, 3)
    eye_t = jnp.eye(chunk, dtype=F32)
    eye_c = jnp.eye(ch, dtype=F32)
    skip = d_skip.reshape(g, ch)
    kin = kin + (eye_t[None, :, None, :, None] * eye_c[None, None, :, None, :] * skip[:, None, :, None, None])
    kin = kin.reshape(g, chunk * ch, chunk * ch)
    to_in = lambda w: w[:, :, ::-1, :].transpose(0, 2, 3, 1).reshape(g, chunk * ch, p)
    pr1, pi1 = pr[:, None, :, 1:], pi[:, None, :, 1:]
    vr = c_re[..., None] * pr1 - c_im[..., None] * pi1
    vi = c_re[..., None] * pi1 + c_im[..., None] * pr1
    to_out = lambda v: v.transpose(0, 2, 3, 1).reshape(g, p, chunk * ch)
    pair = 2
    a_re = pr[..., chunk].reshape(g // pair, 1, pair * p)
    a_im = pi[..., chunk].reshape(g // pair, 1, pair * p)
    return (kin.astype(BF16), to_in(wr).astype(BF16), to_in(wi).astype(BF16),
            to_out(vr).astype(BF16), to_out(-vi).astype(BF16), a_re, a_im)


def _ssm(us, prep, chunk):
    bsz, seq, w = us.shape
    kin, wre, wim, vre, vim, a_re, a_im = prep
    g, p = wre.shape[0], wre.shape[2]
    ch = w // g
    nc = seq // chunk
    rows = nc * bsz
    cols = chunk * ch
    pair = 2
    u = us.reshape(bsz, nc, chunk, g, ch).transpose(3, 1, 0, 2, 4).reshape(g, rows, cols)
    blk = lambda a, b: pl.BlockSpec((pair, a, b), lambda i: (i, 0, 0))
    y = pl.pallas_call(
        functools.partial(_ssm_kernel, bsz=bsz, n_chunks=nc),
        grid=(g // pair,),
        in_specs=[blk(rows, cols), blk(cols, cols), blk(cols, p), blk(cols, p), blk(p, cols), blk(p, cols),
                  pl.BlockSpec((1, 1, pair * p), lambda i: (i, 0, 0)),
                  pl.BlockSpec((1, 1, pair * p), lambda i: (i, 0, 0))],
        out_specs=blk(rows, cols),
        out_shape=jax.ShapeDtypeStruct((g, rows, cols), BF16),
        scratch_shapes=[pltpu.VMEM((rows, pair * p), F32)] * 4,
        compiler_params=_params("parallel"),
        name="ssm",
    )(u, kin, wre, wim, vre, vim, a_re, a_im)
    return y.reshape(g, nc, bsz, chunk, ch).transpose(2, 1, 3, 0, 4).reshape(bsz, seq, w)


def _row_tile(d):
    return (ROW_SUBLANES, d // ROW_SUBLANES)


def _to_rows(v):
    return v.reshape((v.shape[0],) + _row_tile(v.shape[1]))


def _from_rows(v):
    return v.reshape(v.shape[0], v.shape[1] * v.shape[2])


def _layer_norm(h, g, b):
    mu = jnp.mean(h, axis=1, keepdims=True)
    hc = h - mu
    var = jnp.mean(hc * hc, axis=1, keepdims=True)
    return hc * lax.rsqrt(var + LN_EPS) * g + b


def _route(logits_t):
    ng, per = N_EXPERT_GROUPS, EXPERTS_PER_GROUP
    m = jnp.max(logits_t, axis=0, keepdims=True)
    e = jnp.exp(logits_t - m)
    prob = e / jnp.sum(e, axis=0, keepdims=True)
    v = [prob[j * ng:(j + 1) * ng, :] for j in range(per)]

    def top(vals):
        best = functools.reduce(jnp.maximum, vals)
        idx = jnp.full(best.shape, per - 1, jnp.int32)
        for j in range(per - 2, -1, -1):
            idx = jnp.where(vals[j] == best, j, idx)
        return best, idx

    m1, i1 = top(v)
    m2, i2 = top([jnp.where(i1 == j, -1.0, v[j]) for j in range(per)])
    score = m1 + m2
    gid = lax.broadcasted_iota(jnp.int32, score.shape, 0)
    best_g = jnp.min(jnp.where(score == jnp.max(score, axis=0, keepdims=True), gid, ng), axis=0, keepdims=True)
    sel = gid == best_g
    pick_f = lambda a: jnp.sum(jnp.where(sel, a, 0.0), axis=0, keepdims=True)
    pick_i = lambda a: jnp.sum(jnp.where(sel, a, 0), axis=0, keepdims=True)
    p1, p2 = pick_f(m1), pick_f(m2)
    e1, e2 = best_g * per + pick_i(i1), best_g * per + pick_i(i2)
    den = p1 + p2
    return jnp.concatenate([e1, e2], axis=0), jnp.concatenate([p1 / den, p2 / den], axis=0)


def _mix_kernel(x_ref, ya_ref, ys_ref, upc_ref, upp_ref, g1_ref, sc2_ref, sh2_ref, lng_ref, lnb_ref,
                wout_ref, wglu_ref, bglu_ref, wpool_ref, pscale_ref, wr_ref,
                x1_ref, u2_ref, ids_ref, gates_ref, *, alpha):
    i = pl.program_id(1)
    tm = x_ref.shape[1]
    aw = ya_ref.shape[2]
    sw = ys_ref.shape[2]
    y = ys_ref[0].astype(F32)
    y = 0.5 * y * (1.0 + jnp.tanh(math.sqrt(2.0 / math.pi) * (y + 0.044715 * (y * y * y))))
    z = jnp.dot(y.astype(BF16), wglu_ref[...], preferred_element_type=F32) + bglu_ref[...]
    y = y * _sigmoid(z)
    upc = upc_ref[0]
    upp = jnp.where(i > 0, upp_ref[0], 0.0)
    cat = jnp.concatenate([upp, upc], axis=0).astype(BF16)
    r = lax.broadcasted_iota(jnp.int32, (tm, 1), 0)
    lagm = (r + tm) - lax.broadcasted_iota(jnp.int32, (1, 2 * tm), 1)
    tpos = (i * tm + r + 1).astype(F32)
    gc = upc.shape[1] // len(POOL_WINDOWS)
    pooled = []
    for g, win in enumerate(POOL_WINDOWS):
        band = jnp.where((lagm >= 0) & (lagm < win), 1.0, 0.0).astype(BF16)
        ws = jnp.dot(band, cat[:, g * gc:(g + 1) * gc], preferred_element_type=F32)
        pg = ws / jnp.minimum(tpos, float(win)) - upc[:, g * gc:(g + 1) * gc]
        pooled.append(jnp.dot(pg.astype(BF16), wpool_ref[g], preferred_element_type=F32))
    yp = jnp.concatenate(pooled, axis=1) * pscale_ref[...]
    mix = (jnp.dot(ya_ref[0], wout_ref[0:aw, :], preferred_element_type=F32)
           + jnp.dot(y.astype(BF16), wout_ref[aw:aw + sw, :], preferred_element_type=F32)
           + jnp.dot(yp.astype(BF16), wout_ref[aw + sw:, :], preferred_element_type=F32))
    x1 = _layer_norm(alpha * x_ref[0] + (1.0 + g1_ref[0]) * mix, lng_ref[...], lnb_ref[...])
    x1_ref[0] = x1
    u2 = x1 * (1.0 + sc2_ref[0]) + sh2_ref[0]
    u2_ref[0] = _to_rows(u2.astype(BF16))
    logits = jnp.dot(u2, wr_ref[...], precision=lax.Precision.HIGHEST, preferred_element_type=F32)
    ids, gates = _route(logits.T[0:N_EXPERTS, :])
    ids_ref[0] = ids
    gates_ref[0] = gates


def _mix(x, ya, ys, up, g1, sc2, sh2, lng, lnb, wout, wglu, bglu, wpool, pscale, wr_pad, alpha):
    bsz, seq, d = x.shape
    tm = min(ROW_TILE, seq)
    tile = lambda n: pl.BlockSpec((1, tm, n), lambda b, i: (b, i, 0))
    prev = pl.BlockSpec((1, tm, up.shape[2]), lambda b, i: (b, jnp.maximum(i - 1, 0), 0))
    vec = pl.BlockSpec((1, 1, d), lambda b, i: (b, 0, 0))
    full = lambda a: pl.BlockSpec(a.shape, lambda b, i: (0,) * a.ndim)
    lane_rows = pl.BlockSpec((1, 2, tm), lambda b, i: (b, 0, i))
    return pl.pallas_call(
        functools.partial(_mix_kernel, alpha=alpha),
        grid=(bsz, seq // tm),
        in_specs=[tile(d), tile(ya.shape[2]), tile(ys.shape[2]), tile(up.shape[2]), prev, vec, vec, vec,
                  full(lng), full(lnb), full(wout), full(wglu), full(bglu), full(wpool), full(pscale), full(wr_pad)],
        out_specs=[tile(d), pl.BlockSpec((1, tm) + _row_tile(d), lambda b, i: (b, i, 0, 0)), lane_rows, lane_rows],
        out_shape=[jax.ShapeDtypeStruct((bsz, seq, d), F32), jax.ShapeDtypeStruct((bsz, seq) + _row_tile(d), BF16),
                   jax.ShapeDtypeStruct((bsz, 2, seq), jnp.int32), jax.ShapeDtypeStruct((bsz, 2, seq), F32)],
        compiler_params=_params("parallel", "arbitrary"),
        name="mix",
    )(x, ya, ys, up, up, g1, sc2, sh2, lng, lnb, wout, wglu, bglu, wpool, pscale, wr_pad)


def _gather_kernel(idx_ref, src_ref, *rest, lo, n_src, has_prev):
    if has_prev:
        prev_ref, out_ref, buf, sem = rest
    else:
        out_ref, buf, sem = rest

    @pl.when(pl.program_id(0) == 0)
    def _():
        cp = pltpu.make_async_copy(src_ref.at[pl.ds(lo, n_src)], buf, sem)
        cp.start()
        cp.wait()

    base = pl.program_id(0) * COPY_ROWS

    def move(r, carry):
        s = idx_ref[base + r] - lo
        row = buf[jnp.clip(s, 0, n_src - 1)]
        if has_prev:
            row = jnp.where((s >= 0) & (s < n_src), row, prev_ref[r])
        out_ref[r] = row
        return carry

    lax.fori_loop(0, COPY_ROWS, move, 0, unroll=8)


def _gather_rows(src, idx):
    n = idx.shape[0]
    n_chunks = pl.cdiv(src.shape[0], GATHER_SRC_ROWS)
    n_src = src.shape[0] // n_chunks
    assert n_src * n_chunks == src.shape[0]
    block = pl.BlockSpec((COPY_ROWS,) + src.shape[1:], lambda i, idx: (i, 0, 0))
    out = None
    for c in range(n_chunks):
        has_prev = c > 0
        out = pl.pallas_call(
            functools.partial(_gather_kernel, lo=c * n_src, n_src=n_src, has_prev=has_prev),
            grid_spec=pltpu.PrefetchScalarGridSpec(
                num_scalar_prefetch=1, grid=(n // COPY_ROWS,),
                in_specs=[pl.BlockSpec(memory_space=pl.ANY)] + ([block] if has_prev else []),
                out_specs=block,
                scratch_shapes=[pltpu.VMEM((n_src,) + src.shape[1:], src.dtype), pltpu.SemaphoreType.DMA(())]),
            out_shape=jax.ShapeDtypeStruct((n,) + src.shape[1:], src.dtype),
            compiler_params=_params("arbitrary"),
            name="gather",
        )(*((idx, src, out) if has_prev else (idx, src)))
    return out


def _expert_kernel(te_ref, act_ref, first_ref, nxt_ref, x_ref, wg_hbm, wu_hbm, wd_hbm, y_ref,
                   stage_g, stage_u, stage_d, wg_ref, wu_ref, wd_ref, sems, *, layer):
    i = pl.program_id(0)

    def fetch(e):
        return (pltpu.make_async_copy(wg_hbm.at[layer, e], stage_g, sems.at[0]),
                pltpu.make_async_copy(wu_hbm.at[layer, e], stage_u, sems.at[1]),
                pltpu.make_async_copy(wd_hbm.at[layer, e], stage_d, sems.at[2]))

    @pl.when(i == 0)
    def _():
        for cp in fetch(te_ref[0]):
            cp.start()

    @pl.when(first_ref[i] != 0)
    def _():
        for cp in fetch(te_ref[i]):
            cp.wait()
        for stage, work in ((stage_g, wg_ref), (stage_u, wu_ref), (stage_d, wd_ref)):
            rows = stage.shape[0]

            def convert(r, carry, stage=stage, work=work):
                sl = pl.ds(pl.multiple_of(r * CONVERT_ROWS, CONVERT_ROWS), CONVERT_ROWS)
                work[sl, :] = stage[sl, :].astype(BF16)
                return carry

            lax.fori_loop(0, rows // CONVERT_ROWS, convert, 0)

        @pl.when(nxt_ref[i] >= 0)
        def _():
            for cp in fetch(nxt_ref[i]):
                cp.start()

    @pl.when(act_ref[i] != 0)
    def _():
        x = _from_rows(x_ref[...])
        g = jnp.dot(x, wg_ref[...], preferred_element_type=F32)
        u = jnp.dot(x, wu_ref[...], preferred_element_type=F32)
        h = (g * _sigmoid(g)) * u
        y_ref[...] = _to_rows(jnp.dot(h.astype(BF16), wd_ref[...], preferred_element_type=F32).astype(BF16))

    @pl.when(act_ref[i] == 0)
    def _():
        y_ref[...] = jnp.zeros(y_ref.shape, y_ref.dtype)


def _experts(xg, plan, wg, wu, wd, layer):
    rows = xg.shape[0]
    d, f = wg.shape[2], wg.shape[3]
    tm = EXPERT_TILE
    tile = pl.BlockSpec((tm,) + _row_tile(d), lambda i, *_: (i, 0, 0))
    hbm = pl.BlockSpec(memory_space=pl.ANY)
    return pl.pallas_call(
        functools.partial(_expert_kernel, layer=layer),
        grid_spec=pltpu.PrefetchScalarGridSpec(
            num_scalar_prefetch=4, grid=(rows // tm,),
            in_specs=[tile, hbm, hbm, hbm],
            out_specs=tile,
            scratch_shapes=[pltpu.VMEM((d, f), F32), pltpu.VMEM((d, f), F32), pltpu.VMEM((f, d), F32),
                            pltpu.VMEM((d, f), BF16), pltpu.VMEM((d, f), BF16), pltpu.VMEM((f, d), BF16),
                            pltpu.SemaphoreType.DMA((3,))]),
        out_shape=jax.ShapeDtypeStruct((rows,) + _row_tile(d), BF16),
        compiler_params=_params("arbitrary"),
        name="experts",
    )(*plan, xg, wg, wu, wd)


def _combine_kernel(x_ref, z_ref, gc_ref, g2_ref, lng_ref, lnb_ref, o_ref, *, alpha):
    gc = gc_ref[0]
    ffn = (gc[:, 0:1] * _from_rows(z_ref[0, 0]).astype(F32)
           + gc[:, 1:2] * _from_rows(z_ref[1, 0]).astype(F32))
    o_ref[0] = _layer_norm(alpha * x_ref[0] + (1.0 + g2_ref[0]) * ffn, lng_ref[...], lnb_ref[...])


def _combine(x1, z, gates_col, g2, lng, lnb, alpha):
    bsz, seq, d = x1.shape
    tm = min(ROW_TILE, seq)
    tile = pl.BlockSpec((1, tm, d), lambda b, i: (b, i, 0))
    return pl.pallas_call(
        functools.partial(_combine_kernel, alpha=alpha),
        grid=(bsz, seq // tm),
        in_specs=[tile, pl.BlockSpec((2, 1, tm) + _row_tile(d), lambda b, i: (0, b, i, 0, 0)),
                  pl.BlockSpec((1, tm, 2), lambda b, i: (b, i, 0)),
                  pl.BlockSpec((1, 1, d), lambda b, i: (b, 0, 0)),
                  pl.BlockSpec(lng.shape, lambda b, i: (0, 0)), pl.BlockSpec(lnb.shape, lambda b, i: (0, 0))],
        out_specs=tile,
        out_shape=jax.ShapeDtypeStruct((bsz, seq, d), F32),
        compiler_params=_params("parallel", "arbitrary"),
        name="combine",
    )(x1, z, gates_col, g2, lng, lnb)


def _dispatch_plan(ids, n_tok):
    tm = EXPERT_TILE
    e_pair = jnp.concatenate([ids[:, 0, :].reshape(n_tok), ids[:, 1, :].reshape(n_tok)])
    tok_pair = jnp.concatenate([jnp.arange(n_tok, dtype=jnp.int32)] * 2)
    onehot = (e_pair[:, None] == jnp.arange(N_EXPERTS, dtype=jnp.int32)[None, :]).astype(jnp.int32)
    csum = jnp.cumsum(onehot, axis=0)
    rank = jnp.sum(onehot * csum, axis=1) - 1
    counts = csum[-1]
    padded = ((counts + tm - 1) // tm) * tm
    ends = jnp.cumsum(padded)
    offs = ends - padded
    pos = jnp.sum(onehot * offs[None, :], axis=1) + rank
    rows = 2 * n_tok + N_EXPERTS * tm
    src_token = jnp.zeros((rows,), jnp.int32).at[pos].set(tok_pair)
    tile_start = jnp.arange(rows // tm, dtype=jnp.int32) * tm
    tile_expert = jnp.minimum(jnp.sum((tile_start[:, None] >= ends[None, :]).astype(jnp.int32), axis=1),
                              N_EXPERTS - 1)
    tile_active = (tile_start < ends[-1]).astype(jnp.int32)
    prev_expert = jnp.concatenate([jnp.full((1,), -1, jnp.int32), tile_expert[:-1]])
    tile_first = tile_active * (tile_expert != prev_expert).astype(jnp.int32)
    eid = jnp.arange(N_EXPERTS, dtype=jnp.int32)
    later = (padded > 0)[None, :] & (eid[None, :] > eid[:, None])
    next_expert = jnp.min(jnp.where(later, eid[None, :], N_EXPERTS), axis=1)
    next_expert = jnp.where(next_expert == N_EXPERTS, -1, next_expert).astype(jnp.int32)
    tile_next = next_expert[tile_expert]
    return pos.astype(jnp.int32), src_token, (tile_expert, tile_active, tile_first, tile_next)


def _rope_tables(positions, dim):
    rot = dim // ROPE_FRAC
    half = rot // 2
    inv_freq = ROPE_THETA ** (-jnp.arange(half, dtype=F32) * 2.0 / rot)
    ang = positions.astype(F32)[..., None] * inv_freq
    cos, sin = jnp.cos(ang), jnp.sin(ang)
    shape = ang.shape[:-1]
    one = jnp.ones(shape + (dim - rot,), F32)
    zero = jnp.zeros(shape + (dim - rot,), F32)
    zh = jnp.zeros(shape + (half,), F32)
    c = jnp.concatenate([cos, cos, one], axis=-1)
    s1 = jnp.concatenate([-sin, zh, zero], axis=-1)
    s2 = jnp.concatenate([zh, sin, zero], axis=-1)
    rep = LANES // dim
    return jnp.concatenate([jnp.tile(a, (1, 1, rep)) for a in (c, s1, s2)], axis=-1)


def kernel(x, c, positions, w_ada, b_ada, w_in, w_out, ssm_lam_re, ssm_lam_im, ssm_log_step, ssm_b_re, ssm_b_im, ssm_c_re, ssm_c_im, ssm_d, ssm_w_glu, ssm_b_glu, pool_w, pool_scale, ln1_g, ln1_b, ln2_g, ln2_b, w_router, e_gate, e_up, e_down):
    bsz, seq, d = x.shape
    depth = w_ada.shape[0]
    n_tok = bsz * seq
    alpha = (2.0 * depth) ** 0.25
    sw = ssm_d.shape[1]
    pw = pool_scale.shape[1]
    aw = w_out.shape[1] - sw - pw
    kw = KV_HEADS * HEAD_DIM
    iqw = IDX_HEADS * IDX_DIM
    widths = (aw, kw, iqw, sw, pw)
    top_k = min(MAX_TOPK, seq // 4)

    c_pad = jnp.concatenate([c, jnp.zeros((8 - bsz % 8, d), F32)], axis=0) if bsz % 8 else c
    ada = _ada_all(c_pad, w_ada, b_ada)
    rq = _rope_tables(positions, HEAD_DIM)
    ri = _rope_tables(positions, IDX_DIM)
    wr_pad = jnp.concatenate(
        [w_router.reshape(d, N_EXPERT_GROUPS, EXPERTS_PER_GROUP).transpose(0, 2, 1).reshape(d, N_EXPERTS),
         jnp.zeros((d, LANES - N_EXPERTS), F32)], axis=1)

    o_q, o_k, o_v, o_iq, o_ik, o_iw, o_us, o_up = (0, aw, aw + kw, aw + 2 * kw, aw + 2 * kw + iqw,
                                                   aw + 2 * kw + iqw + IDX_DIM,
                                                   aw + 2 * kw + iqw + IDX_DIM + IDX_HEADS,
                                                   aw + 2 * kw + iqw + IDX_DIM + IDX_HEADS + sw)
    for l in range(depth):
        sh1, sc1, g1, sh2, sc2, g2 = [ada[l, :bsz, j * d:(j + 1) * d].reshape(bsz, 1, d) for j in range(6)]
        wl = w_in[l]
        w_r = jnp.concatenate([wl[:, o_q:o_v], wl[:, o_iq:o_ik], wl[:, o_us:], wl[:, o_ik:o_iw],
                               jnp.zeros((d, LANES - IDX_DIM), F32)], axis=1).astype(BF16)
        wvt = wl[:, o_v:o_iq].T.astype(BF16)
        wiwt = wl[:, o_iw:o_us].T.astype(BF16)
        q, k, vt, iq, ik, iwt, us, up = _inproj(x, sc1, sh1, w_r, wvt, wiwt, rq, ri, widths)
        ya = _attention(q, k, vt, iq, ik, iwt, top_k)
        prep = _ssm_prepare(ssm_lam_re[l], ssm_lam_im[l], ssm_log_step[l], ssm_b_re[l], ssm_b_im[l],
                            ssm_c_re[l], ssm_c_im[l], ssm_d[l], SSM_CHUNK)
        ys = _ssm(us, prep, SSM_CHUNK)
        x1, u2, ids, gates = _mix(x, ya, ys, up, g1, sc2, sh2, ln1_g[l][None], ln1_b[l][None],
                                  w_out[l].astype(BF16), ssm_w_glu[l].astype(BF16), ssm_b_glu[l][None],
                                  pool_w[l].astype(BF16), pool_scale[l][None], wr_pad, alpha)
        pos, src_token, tile_plan = _dispatch_plan(ids, n_tok)
        xg = _gather_rows(u2.reshape((n_tok,) + _row_tile(d)), src_token)
        yg = _experts(xg, tile_plan, e_gate, e_up, e_down, l)
        z = _gather_rows(yg, pos).reshape((2, bsz, seq) + _row_tile(d))
        x = _combine(x1, z, gates.transpose(0, 2, 1), g2, ln2_g[l][None], ln2_b[l][None], alpha)
    return x
```

```python
import functools
import math

import jax
import jax.numpy as jnp
from jax import lax
from jax.experimental import pallas as pl
from jax.experimental.pallas import tpu as pltpu

F32 = jnp.float32
BF16 = jnp.bfloat16

HEAD_DIM = 128
KV_HEADS = 2
IDX_HEADS = 8
IDX_DIM = 64
MAX_TOPK = 256
SSM_GROUP_CH = 16
SSM_STATE = 64
POOL_WINDOWS = (2, 4, 8, 16)
ROPE_THETA = 500000.0
ROPE_FRAC = 4
LN_EPS = 1e-5
N_EXPERTS = 16
N_EXPERT_GROUPS = 4
EXPERTS_PER_GROUP = N_EXPERTS // N_EXPERT_GROUPS

LANES = 128
VMEM_LIMIT = 56 * 1024 * 1024
ROW_TILE = 256
ATTN_Q_TILE = 256
ATTN_KEY_STEP = 512
BISECT_STEPS = 14
REDUCE_SLAB = 64
ROW_SUBLANES = 16
GATHER_SRC_ROWS = 10240
SSM_CHUNK = 16
EXPERT_TILE = 256
CONVERT_ROWS = 128
COPY_ROWS = 256
ADA_COLS = 1024


def _params(*sem):
    return pltpu.CompilerParams(dimension_semantics=sem, vmem_limit_bytes=VMEM_LIMIT)


def _sigmoid(x):
    return 1.0 / (1.0 + jnp.exp(-x))


def _ada_kernel(c_ref, w_ref, b_ref, o_ref):
    c = c_ref[...]
    cond = (c * _sigmoid(c)).astype(BF16)
    o_ref[0] = jnp.dot(cond, w_ref[0].astype(BF16), preferred_element_type=F32) + b_ref[0]


def _ada_all(c_pad, w_ada, b_ada):
    depth, d, n6 = w_ada.shape
    rows = c_pad.shape[0]
    return pl.pallas_call(
        _ada_kernel,
        grid=(depth, n6 // ADA_COLS),
        in_specs=[pl.BlockSpec((rows, d), lambda l, j: (0, 0)),
                  pl.BlockSpec((1, d, ADA_COLS), lambda l, j: (l, 0, j)),
                  pl.BlockSpec((1, 1, ADA_COLS), lambda l, j: (l, 0, j))],
        out_specs=pl.BlockSpec((1, rows, ADA_COLS), lambda l, j: (l, 0, j)),
        out_shape=jax.ShapeDtypeStruct((depth, rows, n6), F32),
        compiler_params=_params("arbitrary", "arbitrary"),
        name="ada",
    )(c_pad, w_ada, b_ada.reshape(depth, 1, n6))


def _rope(xv, tab, half):
    w = xv.shape[1]
    rep = w // LANES
    c = jnp.tile(tab[:, 0:LANES], (1, rep))
    s1 = jnp.tile(tab[:, LANES:2 * LANES], (1, rep))
    s2 = jnp.tile(tab[:, 2 * LANES:3 * LANES], (1, rep))
    return xv * c + pltpu.roll(xv, w - half, 1) * s1 + pltpu.roll(xv, half, 1) * s2


def _inproj_kernel(x_ref, sc_ref, sh_ref, w_ref, wvt_ref, wiwt_ref, rq_ref, ri_ref,
                   q_ref, k_ref, vt_ref, iq_ref, ik_ref, iwt_ref, us_ref, up_ref, *, widths, idx_scale):
    aw, kw, iqw, sw, pw = widths
    u = (x_ref[0] * (1.0 + sc_ref[0]) + sh_ref[0]).astype(BF16)
    rq = rq_ref[0]
    ri = ri_ref[0]

    def mm(lo, n):
        return jnp.dot(u, w_ref[:, lo:lo + n], preferred_element_type=F32)

    def mm_t(wt_ref):
        return lax.dot_general(wt_ref[...], u, (((1,), (1,)), ((), ())), preferred_element_type=F32)

    qhalf = HEAD_DIM // ROPE_FRAC // 2
    ihalf = IDX_DIM // ROPE_FRAC // 2
    o = 0
    q_ref[0] = (_rope(mm(o, aw), rq, qhalf) * (HEAD_DIM ** -0.5)).astype(BF16)
    o += aw
    k_ref[0] = _rope(mm(o, kw), rq, qhalf).astype(BF16)
    o += kw
    iq_ref[0] = _rope(mm(o, iqw), ri, ihalf).astype(BF16)
    o += iqw
    us_ref[0] = mm(o, sw).astype(BF16)
    o += sw
    up_ref[0] = mm(o, pw)
    o += pw
    ik_ref[0] = _rope(mm(o, LANES), ri, ihalf)[:, 0:IDX_DIM].astype(BF16)
    vt_ref[0] = mm_t(wvt_ref).astype(BF16)
    iwt_ref[0] = mm_t(wiwt_ref) * idx_scale


def _inproj(x, sc, sh, w_r, wvt, wiwt, rq, ri, widths):
    bsz, seq, d = x.shape
    aw, kw, iqw, sw, pw = widths
    tm = min(ROW_TILE, seq)
    tile = lambda n: pl.BlockSpec((1, tm, n), lambda b, i: (b, i, 0))
    tile_t = lambda n: pl.BlockSpec((1, n, tm), lambda b, i: (b, 0, i))
    vec = pl.BlockSpec((1, 1, d), lambda b, i: (b, 0, 0))
    full = lambda a: pl.BlockSpec(a.shape, lambda b, i: (0, 0))
    sds = jax.ShapeDtypeStruct
    return pl.pallas_call(
        functools.partial(_inproj_kernel, widths=widths, idx_scale=(IDX_DIM ** -0.5) * (IDX_HEADS ** -0.5)),
        grid=(bsz, seq // tm),
        in_specs=[tile(d), vec, vec, full(w_r), full(wvt), full(wiwt), tile(3 * LANES), tile(3 * LANES)],
        out_specs=[tile(aw), tile(kw), tile_t(kw), tile(iqw), tile(IDX_DIM), tile_t(IDX_HEADS), tile(sw), tile(pw)],
        out_shape=[sds((bsz, seq, aw), BF16), sds((bsz, seq, kw), BF16), sds((bsz, kw, seq), BF16),
                   sds((bsz, seq, iqw), BF16), sds((bsz, seq, IDX_DIM), BF16), sds((bsz, IDX_HEADS, seq), F32),
                   sds((bsz, seq, sw), BF16), sds((bsz, seq, pw), F32)],
        compiler_params=_params("parallel", "arbitrary"),
        name="inproj",
    )(x, sc, sh, w_r, wvt, wiwt, rq, ri)


def _col_reduce(op, v):
    rows, n = v.shape
    slab = op(v.reshape(rows // REDUCE_SLAB, REDUCE_SLAB, n), axis=0)
    return op(slab, axis=0, keepdims=True)


def _attn_kernel(q_ref, iq_ref, iwt_ref, k_ref, vt_ref, ik_ref, o_ref, mask_ref, x_ref, *, q0, tq, keys, top_k):
    i = pl.program_id(1)
    t = q0 + i * tq + lax.broadcasted_iota(jnp.int32, (1, tq), 1)
    kpos = lax.broadcasted_iota(jnp.int32, (keys, 1), 0)
    causal = kpos <= t

    iq = iq_ref[0]
    ik = ik_ref[0]
    iwt = iwt_ref[0]
    isc = jnp.zeros((keys, tq), F32)
    for h in range(IDX_HEADS):
        r = lax.dot_general(ik, iq[:, h * IDX_DIM:(h + 1) * IDX_DIM], (((1,), (1,)), ((), ())),
                            preferred_element_type=F32)
        isc = isc + jnp.maximum(r, 0.0) * iwt[h:h + 1, :]
    x = jnp.where(causal, isc, -jnp.inf)
    x_ref[...] = x

    kf = float(top_k)
    n_valid = (t + 1).astype(F32)
    need = n_valid > kf

    def fold(step_fn, init):
        def body(s, acc):
            row = pl.multiple_of(s * REDUCE_SLAB, REDUCE_SLAB)
            return step_fn(acc, x_ref[pl.ds(row, REDUCE_SLAB), :])
        acc0 = jax.tree.map(lambda v: jnp.full((REDUCE_SLAB, tq), v, F32), init)
        return lax.fori_loop(0, keys // REDUCE_SLAB, body, acc0, unroll=4)

    def count_ge(theta):
        acc = fold(lambda a, xs: a + jnp.where(xs >= theta, 1.0, 0.0), 0.0)
        return jnp.sum(acc, axis=0, keepdims=True)

    rmax = _col_reduce(jnp.max, x)
    rmin = _col_reduce(jnp.min, jnp.where(causal, isc, jnp.inf))
    c_max = count_ge(rmax)
    top_tie = c_max >= kf
    lo0 = jnp.where(need, jnp.where(top_tie, rmax, rmin), -jnp.inf)
    clo0 = jnp.where(top_tie, c_max, n_valid)
    done0 = jnp.where(need & jnp.logical_not(top_tie) & (clo0 != kf), 0.0, 1.0)

    def bisect(lo, hi, clo, chi, done):
        mid = 0.5 * lo + 0.5 * hi
        c = count_ge(mid)
        live = done == 0.0
        ge = c >= kf
        up = live & ge
        dn = live & jnp.logical_not(ge)
        lo = jnp.where(up, mid, lo)
        clo = jnp.where(up, c, clo)
        hi = jnp.where(dn, mid, hi)
        chi = jnp.where(dn, c, chi)
        done = jnp.where(clo == kf, 1.0, done)
        return lo, hi, clo, chi, done

    def n_active(done):
        return jnp.sum(1.0 - done).astype(jnp.int32)

    def body2(s):
        _, lo, hi, clo, chi, done = s
        lo, hi, clo, chi, done = bisect(lo, hi, clo, chi, done)

        def min_max(acc, xs):
            inside = (xs >= lo) & (xs < hi)
            return (jnp.minimum(acc[0], jnp.where(inside, xs, jnp.inf)),
                    jnp.maximum(acc[1], jnp.where(inside, xs, -jnp.inf)))

        mins, maxs = fold(min_max, (jnp.inf, -jnp.inf))
        vmin = jnp.min(mins, axis=0, keepdims=True)
        vmax = jnp.max(maxs, axis=0, keepdims=True)
        c2 = count_ge(vmax)
        live = done == 0.0
        single = vmin == vmax
        top_ok = c2 >= kf
        take_top = live & jnp.logical_not(single) & top_ok
        drop_top = live & jnp.logical_not(single) & jnp.logical_not(top_ok)
        lo = jnp.where(live, jnp.where(take_top, vmax, vmin), lo)
        clo = jnp.where(take_top, c2, clo)
        hi = jnp.where(drop_top, vmax, hi)
        chi = jnp.where(drop_top, c2, chi)
        done = jnp.where(live & (single | top_ok), 1.0, done)
        return n_active(done), lo, hi, clo, chi, done

    state = lax.fori_loop(0, BISECT_STEPS, lambda _, s: bisect(*s), (lo0, rmax, clo0, c_max, done0))
    state = lax.while_loop(lambda s: s[0] > 0, body2, (n_active(state[4]),) + state)
    _, lo, _, clo, _, _ = state

    mask_ref[...] = jnp.where(causal & (x_ref[...] >= lo), 0.0, -1e30)

    tie_q = jnp.sum(jnp.where(need & (clo > kf), 1.0, 0.0)).astype(jnp.int32)

    @pl.when(tie_q > 0)
    def _():
        room = kf - _col_reduce(jnp.sum, jnp.where(x_ref[...] > lo, 1.0, 0.0))
        step = 256
        rr = lax.broadcasted_iota(jnp.int32, (step, step), 0)
        cc = lax.broadcasted_iota(jnp.int32, (step, step), 1)
        before = jnp.where(cc < rr, 1.0, 0.0).astype(BF16)
        carry = jnp.zeros((1, tq), F32)
        for c0 in range(0, keys, step):
            xs = x_ref[c0:c0 + step, :]
            e = jnp.where(xs == lo, 1.0, 0.0)
            rank = carry + jnp.dot(before, e.astype(BF16), preferred_element_type=F32)
            keep = (xs > lo) | ((xs == lo) & (rank < room))
            mask_ref[c0:c0 + step, :] = jnp.where(keep & (xs > -jnp.inf), 0.0, -1e30)
            carry = carry + jnp.sum(e, axis=0, keepdims=True)

    bias = mask_ref[...]
    q = q_ref[0]
    group = q.shape[1] // HEAD_DIM // KV_HEADS
    bias_g = jnp.concatenate([bias] * group, axis=1)
    for g in range(KV_HEADS):
        qg = jnp.concatenate([q[:, (g * group + j) * HEAD_DIM:(g * group + j + 1) * HEAD_DIM]
                              for j in range(group)], axis=0)
        kg = k_ref[0, :, g * HEAD_DIM:(g + 1) * HEAD_DIM]
        vtg = vt_ref[0, g * HEAD_DIM:(g + 1) * HEAD_DIM, :]
        s = lax.dot_general(kg, qg, (((1,), (1,)), ((), ())), preferred_element_type=F32) + bias_g
        m = _col_reduce(jnp.max, s)
        p = jnp.exp(s - m)
        den = _col_reduce(jnp.sum, p)
        og = jnp.dot(vtg, p.astype(BF16), preferred_element_type=F32) / den
        for j in range(group):
            h = g * group + j
            o_ref[0, :, h * HEAD_DIM:(h + 1) * HEAD_DIM] = og[:, j * tq:(j + 1) * tq].T.astype(BF16)


def _attention(q, k, vt, iq, ik, iwt, top_k):
    bsz, seq, aw = q.shape
    tq = min(ATTN_Q_TILE, seq)
    step = min(ATTN_KEY_STEP, seq)
    outs = []
    for q0 in range(0, seq, step):
        keys = q0 + step
        nq = step // tq
        qtile = lambda n, q0=q0: pl.BlockSpec((1, tq, n), lambda b, i: (b, q0 // tq + i, 0))
        ktile = lambda n, keys=keys: pl.BlockSpec((1, keys, n), lambda b, i: (b, 0, 0))
        outs.append(pl.pallas_call(
            functools.partial(_attn_kernel, q0=q0, tq=tq, keys=keys, top_k=top_k),
            grid=(bsz, nq),
            in_specs=[qtile(aw), qtile(iq.shape[2]),
                      pl.BlockSpec((1, iwt.shape[1], tq), lambda b, i, q0=q0: (b, 0, q0 // tq + i)),
                      ktile(k.shape[2]),
                      pl.BlockSpec((1, vt.shape[1], keys), lambda b, i: (b, 0, 0)),
                      ktile(ik.shape[2])],
            out_specs=pl.BlockSpec((1, tq, aw), lambda b, i: (b, i, 0)),
            out_shape=jax.ShapeDtypeStruct((bsz, step, aw), BF16),
            scratch_shapes=[pltpu.VMEM((keys, tq), F32), pltpu.VMEM((keys, tq), F32)],
            compiler_params=_params("parallel", "arbitrary"),
            name=f"attn_k{keys}",
        )(q, iq, iwt, k, vt, ik))
    return jnp.concatenate(outs, axis=1)


def _ssm_kernel(u_ref, kin_ref, wre_ref, wim_ref, vre_ref, vim_ref, are_ref, aim_ref, y_ref,
                sre_ref, sim_ref, xre_ref, xim_ref, *, bsz, n_chunks):
    pair = u_ref.shape[0]
    p = wre_ref.shape[2]
    for j in range(pair):
        u = u_ref[j]
        sre_ref[:, j * p:(j + 1) * p] = jnp.dot(u, wre_ref[j], preferred_element_type=F32)
        sim_ref[:, j * p:(j + 1) * p] = jnp.dot(u, wim_ref[j], preferred_element_type=F32)
    a_re = are_ref[0]
    a_im = aim_ref[0]
    x_re = jnp.zeros((bsz, pair * p), F32)
    x_im = jnp.zeros((bsz, pair * p), F32)
    for c in range(n_chunks):
        rows = slice(c * bsz, (c + 1) * bsz)
        xre_ref[rows, :] = x_re
        xim_ref[rows, :] = x_im
        n_re = a_re * x_re - a_im * x_im + sre_ref[rows, :]
        n_im = a_re * x_im + a_im * x_re + sim_ref[rows, :]
        x_re, x_im = n_re, n_im
    for j in range(pair):
        xr = xre_ref[:, j * p:(j + 1) * p].astype(BF16)
        xi = xim_ref[:, j * p:(j + 1) * p].astype(BF16)
        y_ref[j] = (jnp.dot(u_ref[j], kin_ref[j], preferred_element_type=F32)
                    + jnp.dot(xr, vre_ref[j], preferred_element_type=F32)
                    + jnp.dot(xi, vim_ref[j], preferred_element_type=F32)).astype(BF16)


def _ssm_prepare(lam_re, lam_im, log_step, b_re, b_im, c_re, c_im, d_skip, chunk):
    g, p = lam_re.shape
    ch = b_re.shape[2]
    hp = lax.Precision.HIGHEST
    step = jnp.exp(log_step)[:, None]
    lsr, lsi = lam_re * step, lam_im * step
    er = jnp.exp(lsr)
    nr, ni = er * jnp.cos(lsi) - 1.0, er * jnp.sin(lsi)
    den = lam_re * lam_re + lam_im * lam_im
    fr, fi = (nr * lam_re + ni * lam_im) / den, (ni * lam_re - nr * lam_im) / den
    bbr = fr[..., None] * b_re - fi[..., None] * b_im
    bbi = fr[..., None] * b_im + fi[..., None] * b_re
    n = jnp.arange(chunk + 1, dtype=F32)
    mag = jnp.exp(lsr[..., None] * n)
    pr, pi = mag * jnp.cos(lsi[..., None] * n), mag * jnp.sin(lsi[..., None] * n)
    prt, pit = pr[:, :, :chunk, None], pi[:, :, :chunk, None]
    wr = prt * bbr[:, :, None, :] - pit * bbi[:, :, None, :]
    wi = prt * bbi[:, :, None, :] + pit * bbr[:, :, None, :]
    ker = (jnp.einsum("gcp,gptd->gtcd", c_re, wr, precision=hp)
           - jnp.einsum("gcp,gptd->gtcd", c_im, wi, precision=hp))
    lag = jnp.arange(chunk)[None, :] - jnp.arange(chunk)[:, None]
    kexp = jnp.where((lag >= 0)[None, :, :, None, None], ker[:, jnp.clip(lag, 0, chunk - 1)], 0.0)
    kin = kexp.transpose(0, 1, 4, 2, 3)
    eye_t = jnp.eye(chunk, dtype=F32)
    eye_c = jnp.eye(ch, dtype=F32)
    skip = d_skip.reshape(g, ch)
    kin = kin + (eye_t[None, :, None, :, None] * eye_c[None, None, :, None, :] * skip[:, None, :, None, None])
    kin = kin.reshape(g, chunk * ch, chunk * ch)
    to_in = lambda w: w[:, :, ::-1, :].transpose(0, 2, 3, 1).reshape(g, chunk * ch, p)
    pr1, pi1 = pr[:, None, :, 1:], pi[:, None, :, 1:]
    vr = c_re[..., None] * pr1 - c_im[..., None] * pi1
    vi = c_re[..., None] * pi1 + c_im[..., None] * pr1
    to_out = lambda v: v.transpose(0, 2, 3, 1).reshape(g, p, chunk * ch)
    pair = 2
    a_re = pr[..., chunk].reshape(g // pair, 1, pair * p)
    a_im = pi[..., chunk].reshape(g // pair, 1, pair * p)
    return (kin.astype(BF16), to_in(wr).astype(BF16), to_in(wi).astype(BF16),
            to_out(vr).astype(BF16), to_out(-vi).astype(BF16), a_re, a_im)


def _ssm(us, prep, chunk):
    bsz, seq, w = us.shape
    kin, wre, wim, vre, vim, a_re, a_im = prep
    g, p = wre.shape[0], wre.shape[2]
    ch = w // g
    nc = seq // chunk
    rows = nc * bsz
    cols = chunk * ch
    pair = 2
    u = us.reshape(bsz, nc, chunk, g, ch).transpose(3, 1, 0, 2, 4).reshape(g, rows, cols)
    blk = lambda a, b: pl.BlockSpec((pair, a, b), lambda i: (i, 0, 0))
    y = pl.pallas_call(
        functools.partial(_ssm_kernel, bsz=bsz, n_chunks=nc),
        grid=(g // pair,),
        in_specs=[blk(rows, cols), blk(cols, cols), blk(cols, p), blk(cols, p), blk(p, cols), blk(p, cols),
                  pl.BlockSpec((1, 1, pair * p), lambda i: (i, 0, 0)),
                  pl.BlockSpec((1, 1, pair * p), lambda i: (i, 0, 0))],
        out_specs=blk(rows, cols),
        out_shape=jax.ShapeDtypeStruct((g, rows, cols), BF16),
        scratch_shapes=[pltpu.VMEM((rows, pair * p), F32)] * 4,
        compiler_params=_params("parallel"),
        name="ssm",
    )(u, kin, wre, wim, vre, vim, a_re, a_im)
    return y.reshape(g, nc, bsz, chunk, ch).transpose(2, 1, 3, 0, 4).reshape(bsz, seq, w)


def _row_tile(d):
    return (ROW_SUBLANES, d // ROW_SUBLANES)


def _to_rows(v):
    return v.reshape((v.shape[0],) + _row_tile(v.shape[1]))


def _from_rows(v):
    return v.reshape(v.shape[0], v.shape[1] * v.shape[2])


def _layer_norm(h, g, b):
    mu = jnp.mean(h, axis=1, keepdims=True)
    hc = h - mu
    var = jnp.mean(hc * hc, axis=1, keepdims=True)
    return hc * lax.rsqrt(var + LN_EPS) * g + b


def _route(logits_t):
    ng, per = N_EXPERT_GROUPS, EXPERTS_PER_GROUP
    m = jnp.max(logits_t, axis=0, keepdims=True)
    e = jnp.exp(logits_t - m)
    prob = e / jnp.sum(e, axis=0, keepdims=True)
    v = [prob[j * ng:(j + 1) * ng, :] for j in range(per)]

    def top(vals):
        best = functools.reduce(jnp.maximum, vals)
        idx = jnp.full(best.shape, per - 1, jnp.int32)
        for j in range(per - 2, -1, -1):
            idx = jnp.where(vals[j] == best, j, idx)
        return best, idx

    m1, i1 = top(v)
    m2, i2 = top([jnp.where(i1 == j, -1.0, v[j]) for j in range(per)])
    score = m1 + m2
    gid = lax.broadcasted_iota(jnp.int32, score.shape, 0)
    best_g = jnp.min(jnp.where(score == jnp.max(score, axis=0, keepdims=True), gid, ng), axis=0, keepdims=True)
    sel = gid == best_g
    pick_f = lambda a: jnp.sum(jnp.where(sel, a, 0.0), axis=0, keepdims=True)
    pick_i = lambda a: jnp.sum(jnp.where(sel, a, 0), axis=0, keepdims=True)
    p1, p2 = pick_f(m1), pick_f(m2)
    e1, e2 = best_g * per + pick_i(i1), best_g * per + pick_i(i2)
    den = p1 + p2
    return jnp.concatenate([e1, e2], axis=0), jnp.concatenate([p1 / den, p2 / den], axis=0)


def _mix_kernel(x_ref, ya_ref, ys_ref, upc_ref, upp_ref, g1_ref, sc2_ref, sh2_ref, lng_ref, lnb_ref,
                wout_ref, wglu_ref, bglu_ref, wpool_ref, pscale_ref, wr_ref,
                x1_ref, u2_ref, ids_ref, gates_ref, *, alpha):
    i = pl.program_id(1)
    tm = x_ref.shape[1]
    aw = ya_ref.shape[2]
    sw = ys_ref.shape[2]
    y = ys_ref[0].astype(F32)
    y = 0.5 * y * (1.0 + jnp.tanh(math.sqrt(2.0 / math.pi) * (y + 0.044715 * (y * y * y))))
    z = jnp.dot(y.astype(BF16), wglu_ref[...], preferred_element_type=F32) + bglu_ref[...]
    y = y * _sigmoid(z)
    upc = upc_ref[0]
    upp = jnp.where(i > 0, upp_ref[0], 0.0)
    cat = jnp.concatenate([upp, upc], axis=0).astype(BF16)
    r = lax.broadcasted_iota(jnp.int32, (tm, 1), 0)
    lagm = (r + tm) - lax.broadcasted_iota(jnp.int32, (1, 2 * tm), 1)
    tpos = (i * tm + r + 1).astype(F32)
    gc = upc.shape[1] // len(POOL_WINDOWS)
    pooled = []
    for g, win in enumerate(POOL_WINDOWS):
        band = jnp.where((lagm >= 0) & (lagm < win), 1.0, 0.0).astype(BF16)
        ws = jnp.dot(band, cat[:, g * gc:(g + 1) * gc], preferred_element_type=F32)
        pg = ws / jnp.minimum(tpos, float(win)) - upc[:, g * gc:(g + 1) * gc]
        pooled.append(jnp.dot(pg.astype(BF16), wpool_ref[g], preferred_element_type=F32))
    yp = jnp.concatenate(pooled, axis=1) * pscale_ref[...]
    mixed = jnp.concatenate([ya_ref[0], y.astype(BF16), yp.astype(BF16)], axis=1)
    mix = jnp.dot(mixed, wout_ref[...], preferred_element_type=F32)
    x1 = _layer_norm(alpha * x_ref[0] + (1.0 + g1_ref[0]) * mix, lng_ref[...], lnb_ref[...])
    x1_ref[0] = x1
    u2 = x1 * (1.0 + sc2_ref[0]) + sh2_ref[0]
    u2_ref[0] = _to_rows(u2.astype(BF16))
    u_hi = u2.astype(BF16)
    u_lo = (u2 - u_hi.astype(F32)).astype(BF16)
    logits = jnp.dot(jnp.concatenate([u_hi, u_hi, u_lo], axis=1), wr_ref[...], preferred_element_type=F32)
    ids, gates = _route(logits.T[0:N_EXPERTS, :])
    ids_ref[0] = ids
    gates_ref[0] = gates


def _mix(x, ya, ys, up, g1, sc2, sh2, lng, lnb, wout, wglu, bglu, wpool, pscale, wr_pad, alpha):
    bsz, seq, d = x.shape
    tm = min(ROW_TILE, seq)
    tile = lambda n: pl.BlockSpec((1, tm, n), lambda b, i: (b, i, 0))
    prev = pl.BlockSpec((1, tm, up.shape[2]), lambda b, i: (b, jnp.maximum(i - 1, 0), 0))
    vec = pl.BlockSpec((1, 1, d), lambda b, i: (b, 0, 0))
    full = lambda a: pl.BlockSpec(a.shape, lambda b, i: (0,) * a.ndim)
    lane_rows = pl.BlockSpec((1, 2, tm), lambda b, i: (b, 0, i))
    return pl.pallas_call(
        functools.partial(_mix_kernel, alpha=alpha),
        grid=(bsz, seq // tm),
        in_specs=[tile(d), tile(ya.shape[2]), tile(ys.shape[2]), tile(up.shape[2]), prev, vec, vec, vec,
                  full(lng), full(lnb), full(wout), full(wglu), full(bglu), full(wpool), full(pscale), full(wr_pad)],
        out_specs=[tile(d), pl.BlockSpec((1, tm) + _row_tile(d), lambda b, i: (b, i, 0, 0)), lane_rows, lane_rows],
        out_shape=[jax.ShapeDtypeStruct((bsz, seq, d), F32), jax.ShapeDtypeStruct((bsz, seq) + _row_tile(d), BF16),
                   jax.ShapeDtypeStruct((bsz, 2, seq), jnp.int32), jax.ShapeDtypeStruct((bsz, 2, seq), F32)],
        compiler_params=_params("parallel", "arbitrary"),
        name="mix",
    )(x, ya, ys, up, up, g1, sc2, sh2, lng, lnb, wout, wglu, bglu, wpool, pscale, wr_pad)


def _gather_kernel(idx_ref, src_ref, *rest, lo, n_src, partial_src, has_prev):
    if has_prev:
        prev_ref, out_ref, buf, sem = rest
    else:
        out_ref, buf, sem = rest

    @pl.when(pl.program_id(0) == 0)
    def _():
        cp = pltpu.make_async_copy(src_ref.at[pl.ds(lo, n_src)], buf, sem)
        cp.start()
        cp.wait()

    base = pl.program_id(0) * COPY_ROWS

    def move(r, carry):
        s = idx_ref[base + r]
        row = buf[jnp.maximum(s, 0)] if partial_src else buf[s]
        if has_prev:
            row = jnp.where(s >= 0, row, prev_ref[r])
        out_ref[r] = row
        return carry

    lax.fori_loop(0, COPY_ROWS, move, 0, unroll=8)


def _gather_rows(src, idx):
    n = idx.shape[0]
    n_chunks = pl.cdiv(src.shape[0], GATHER_SRC_ROWS)
    n_src = src.shape[0] // n_chunks
    assert n_src * n_chunks == src.shape[0]
    block = pl.BlockSpec((COPY_ROWS,) + src.shape[1:], lambda i, idx: (i, 0, 0))
    out = None
    for c in range(n_chunks):
        has_prev = c > 0
        local = idx - c * n_src
        local = idx if n_chunks == 1 else jnp.where((local >= 0) & (local < n_src), local, -1)
        out = pl.pallas_call(
            functools.partial(_gather_kernel, lo=c * n_src, n_src=n_src, partial_src=n_chunks > 1,
                              has_prev=has_prev),
            grid_spec=pltpu.PrefetchScalarGridSpec(
                num_scalar_prefetch=1, grid=(n // COPY_ROWS,),
                in_specs=[pl.BlockSpec(memory_space=pl.ANY)] + ([block] if has_prev else []),
                out_specs=block,
                scratch_shapes=[pltpu.VMEM((n_src,) + src.shape[1:], src.dtype), pltpu.SemaphoreType.DMA(())]),
            out_shape=jax.ShapeDtypeStruct((n,) + src.shape[1:], src.dtype),
            compiler_params=_params("arbitrary"),
            name="gather",
        )(*((local, src, out) if has_prev else (local, src)))
    return out


def _expert_kernel(te_ref, act_ref, first_ref, nxt_ref, x_ref, wg_hbm, wu_hbm, wd_hbm, y_ref,
                   stage_g, stage_u, stage_d, wg_ref, wu_ref, wd_ref, sems, *, layer):
    i = pl.program_id(0)

    def fetch(e):
        return (pltpu.make_async_copy(wg_hbm.at[layer, e], stage_g, sems.at[0]),
                pltpu.make_async_copy(wu_hbm.at[layer, e], stage_u, sems.at[1]),
                pltpu.make_async_copy(wd_hbm.at[layer, e], stage_d, sems.at[2]))

    @pl.when(i == 0)
    def _():
        for cp in fetch(te_ref[0]):
            cp.start()

    @pl.when(first_ref[i] != 0)
    def _():
        for cp in fetch(te_ref[i]):
            cp.wait()
        for stage, work in ((stage_g, wg_ref), (stage_u, wu_ref), (stage_d, wd_ref)):
            rows = stage.shape[0]

            def convert(r, carry, stage=stage, work=work):
                sl = pl.ds(pl.multiple_of(r * CONVERT_ROWS, CONVERT_ROWS), CONVERT_ROWS)
                work[sl, :] = stage[sl, :].astype(BF16)
                return carry

            lax.fori_loop(0, rows // CONVERT_ROWS, convert, 0)

        @pl.when(nxt_ref[i] >= 0)
        def _():
            for cp in fetch(nxt_ref[i]):
                cp.start()

    @pl.when(act_ref[i] != 0)
    def _():
        x = _from_rows(x_ref[...])
        g = jnp.dot(x, wg_ref[...], preferred_element_type=F32)
        u = jnp.dot(x, wu_ref[...], preferred_element_type=F32)
        h = (g * _sigmoid(g)) * u
        y_ref[...] = _to_rows(jnp.dot(h.astype(BF16), wd_ref[...], preferred_element_type=F32).astype(BF16))

    @pl.when(act_ref[i] == 0)
    def _():
        y_ref[...] = jnp.zeros(y_ref.shape, y_ref.dtype)


def _experts(xg, plan, wg, wu, wd, layer):
    rows = xg.shape[0]
    d, f = wg.shape[2], wg.shape[3]
    tm = EXPERT_TILE
    tile = pl.BlockSpec((tm,) + _row_tile(d), lambda i, *_: (i, 0, 0))
    hbm = pl.BlockSpec(memory_space=pl.ANY)
    return pl.pallas_call(
        functools.partial(_expert_kernel, layer=layer),
        grid_spec=pltpu.PrefetchScalarGridSpec(
            num_scalar_prefetch=4, grid=(rows // tm,),
            in_specs=[tile, hbm, hbm, hbm],
            out_specs=tile,
            scratch_shapes=[pltpu.VMEM((d, f), F32), pltpu.VMEM((d, f), F32), pltpu.VMEM((f, d), F32),
                            pltpu.VMEM((d, f), BF16), pltpu.VMEM((d, f), BF16), pltpu.VMEM((f, d), BF16),
                            pltpu.SemaphoreType.DMA((3,))]),
        out_shape=jax.ShapeDtypeStruct((rows,) + _row_tile(d), BF16),
        compiler_params=_params("arbitrary"),
        name="experts",
    )(*plan, xg, wg, wu, wd)


def _combine_kernel(x_ref, z_ref, gc_ref, g2_ref, lng_ref, lnb_ref, o_ref, *, alpha):
    gc = gc_ref[0]
    ffn = (gc[:, 0:1] * _from_rows(z_ref[0, 0]).astype(F32)
           + gc[:, 1:2] * _from_rows(z_ref[1, 0]).astype(F32))
    o_ref[0] = _layer_norm(alpha * x_ref[0] + (1.0 + g2_ref[0]) * ffn, lng_ref[...], lnb_ref[...])


def _combine(x1, z, gates_col, g2, lng, lnb, alpha):
    bsz, seq, d = x1.shape
    tm = min(ROW_TILE, seq)
    tile = pl.BlockSpec((1, tm, d), lambda b, i: (b, i, 0))
    return pl.pallas_call(
        functools.partial(_combine_kernel, alpha=alpha),
        grid=(bsz, seq // tm),
        in_specs=[tile, pl.BlockSpec((2, 1, tm) + _row_tile(d), lambda b, i: (0, b, i, 0, 0)),
                  pl.BlockSpec((1, tm, 2), lambda b, i: (b, i, 0)),
                  pl.BlockSpec((1, 1, d), lambda b, i: (b, 0, 0)),
                  pl.BlockSpec(lng.shape, lambda b, i: (0, 0)), pl.BlockSpec(lnb.shape, lambda b, i: (0, 0))],
        out_specs=tile,
        out_shape=jax.ShapeDtypeStruct((bsz, seq, d), F32),
        compiler_params=_params("parallel", "arbitrary"),
        name="combine",
    )(x1, z, gates_col, g2, lng, lnb)


def _dispatch_plan(ids, n_tok):
    tm = EXPERT_TILE
    e_pair = jnp.concatenate([ids[:, 0, :].reshape(n_tok), ids[:, 1, :].reshape(n_tok)])
    tok_pair = jnp.concatenate([jnp.arange(n_tok, dtype=jnp.int32)] * 2)
    onehot = (e_pair[:, None] == jnp.arange(N_EXPERTS, dtype=jnp.int32)[None, :]).astype(jnp.int32)
    csum = jnp.cumsum(onehot, axis=0)
    rank = jnp.sum(onehot * csum, axis=1) - 1
    counts = csum[-1]
    padded = ((counts + tm - 1) // tm) * tm
    ends = jnp.cumsum(padded)
    offs = ends - padded
    pos = jnp.sum(onehot * offs[None, :], axis=1) + rank
    rows = 2 * n_tok + N_EXPERTS * tm
    src_token = jnp.zeros((rows,), jnp.int32).at[pos].set(tok_pair)
    tile_start = jnp.arange(rows // tm, dtype=jnp.int32) * tm
    tile_expert = jnp.minimum(jnp.sum((tile_start[:, None] >= ends[None, :]).astype(jnp.int32), axis=1),
                              N_EXPERTS - 1)
    tile_active = (tile_start < ends[-1]).astype(jnp.int32)
    prev_expert = jnp.concatenate([jnp.full((1,), -1, jnp.int32), tile_expert[:-1]])
    tile_first = tile_active * (tile_expert != prev_expert).astype(jnp.int32)
    eid = jnp.arange(N_EXPERTS, dtype=jnp.int32)
    later = (padded > 0)[None, :] & (eid[None, :] > eid[:, None])
    next_expert = jnp.min(jnp.where(later, eid[None, :], N_EXPERTS), axis=1)
    next_expert = jnp.where(next_expert == N_EXPERTS, -1, next_expert).astype(jnp.int32)
    tile_next = next_expert[tile_expert]
    return pos.astype(jnp.int32), src_token, (tile_expert, tile_active, tile_first, tile_next)


def _rope_tables(positions, dim):
    rot = dim // ROPE_FRAC
    half = rot // 2
    inv_freq = ROPE_THETA ** (-jnp.arange(half, dtype=F32) * 2.0 / rot)
    ang = positions.astype(F32)[..., None] * inv_freq
    cos, sin = jnp.cos(ang), jnp.sin(ang)
    shape = ang.shape[:-1]
    one = jnp.ones(shape + (dim - rot,), F32)
    zero = jnp.zeros(shape + (dim - rot,), F32)
    zh = jnp.zeros(shape + (half,), F32)
    c = jnp.concatenate([cos, cos, one], axis=-1)
    s1 = jnp.concatenate([-sin, zh, zero], axis=-1)
    s2 = jnp.concatenate([zh, sin, zero], axis=-1)
    rep = LANES // dim
    return jnp.concatenate([jnp.tile(a, (1, 1, rep)) for a in (c, s1, s2)], axis=-1)


def kernel(x, c, positions, w_ada, b_ada, w_in, w_out, ssm_lam_re, ssm_lam_im, ssm_log_step, ssm_b_re, ssm_b_im, ssm_c_re, ssm_c_im, ssm_d, ssm_w_glu, ssm_b_glu, pool_w, pool_scale, ln1_g, ln1_b, ln2_g, ln2_b, w_router, e_gate, e_up, e_down):
    bsz, seq, d = x.shape
    depth = w_ada.shape[0]
    n_tok = bsz * seq
    alpha = (2.0 * depth) ** 0.25
    sw = ssm_d.shape[1]
    pw = pool_scale.shape[1]
    aw = w_out.shape[1] - sw - pw
    kw = KV_HEADS * HEAD_DIM
    iqw = IDX_HEADS * IDX_DIM
    widths = (aw, kw, iqw, sw, pw)
    top_k = min(MAX_TOPK, seq // 4)

    c_pad = jnp.concatenate([c, jnp.zeros((8 - bsz % 8, d), F32)], axis=0) if bsz % 8 else c
    ada = _ada_all(c_pad, w_ada, b_ada)
    rq = _rope_tables(positions, HEAD_DIM)
    ri = _rope_tables(positions, IDX_DIM)
    wr_pad = jnp.concatenate(
        [w_router.reshape(d, N_EXPERT_GROUPS, EXPERTS_PER_GROUP).transpose(0, 2, 1).reshape(d, N_EXPERTS),
         jnp.zeros((d, LANES - N_EXPERTS), F32)], axis=1)
    wr_hi = wr_pad.astype(BF16)
    wr_lo = (wr_pad - wr_hi.astype(F32)).astype(BF16)
    wr_pad = jnp.concatenate([wr_hi, wr_lo, wr_hi], axis=0)

    o_q, o_k, o_v, o_iq, o_ik, o_iw, o_us, o_up = (0, aw, aw + kw, aw + 2 * kw, aw + 2 * kw + iqw,
                                                   aw + 2 * kw + iqw + IDX_DIM,
                                                   aw + 2 * kw + iqw + IDX_DIM + IDX_HEADS,
                                                   aw + 2 * kw + iqw + IDX_DIM + IDX_HEADS + sw)
    for l in range(depth):
        sh1, sc1, g1, sh2, sc2, g2 = [ada[l, :bsz, j * d:(j + 1) * d].reshape(bsz, 1, d) for j in range(6)]
        wl = w_in[l]
        w_r = jnp.concatenate([wl[:, o_q:o_v], wl[:, o_iq:o_ik], wl[:, o_us:], wl[:, o_ik:o_iw],
                               jnp.zeros((d, LANES - IDX_DIM), F32)], axis=1).astype(BF16)
        wvt = wl[:, o_v:o_iq].T.astype(BF16)
        wiwt = wl[:, o_iw:o_us].T.astype(BF16)
        q, k, vt, iq, ik, iwt, us, up = _inproj(x, sc1, sh1, w_r, wvt, wiwt, rq, ri, widths)
        ya = _attention(q, k, vt, iq, ik, iwt, top_k)
        prep = _ssm_prepare(ssm_lam_re[l], ssm_lam_im[l], ssm_log_step[l], ssm_b_re[l], ssm_b_im[l],
                            ssm_c_re[l], ssm_c_im[l], ssm_d[l], SSM_CHUNK)
        ys = _ssm(us, prep, SSM_CHUNK)
        x1, u2, ids, gates = _mix(x, ya, ys, up, g1, sc2, sh2, ln1_g[l][None], ln1_b[l][None],
                                  w_out[l].astype(BF16), ssm_w_glu[l].astype(BF16), ssm_b_glu[l][None],
                                  pool_w[l].astype(BF16), pool_scale[l][None], wr_pad, alpha)
        pos, src_token, tile_plan = _dispatch_plan(ids, n_tok)
        xg = _gather_rows(u2.reshape((n_tok,) + _row_tile(d)), src_token)
        yg = _experts(xg, tile_plan, e_gate, e_up, e_down, l)
        z = _gather_rows(yg, pos).reshape((2, bsz, seq) + _row_tile(d))
        x = _combine(x1, z, gates.transpose(0, 2, 1), g2, ln2_g[l][None], ln2_b[l][None], alpha)
    return x
```

```python
import functools
import math

import jax
import jax.numpy as jnp
from jax import lax
from jax.experimental import pallas as pl
from jax.experimental.pallas import tpu as pltpu

F32 = jnp.float32
BF16 = jnp.bfloat16

HEAD_DIM = 128
KV_HEADS = 2
IDX_HEADS = 8
IDX_DIM = 64
MAX_TOPK = 256
SSM_GROUP_CH = 16
SSM_STATE = 64
POOL_WINDOWS = (2, 4, 8, 16)
ROPE_THETA = 500000.0
ROPE_FRAC = 4
LN_EPS = 1e-5
N_EXPERTS = 16
N_EXPERT_GROUPS = 4
EXPERTS_PER_GROUP = N_EXPERTS // N_EXPERT_GROUPS

LANES = 128
VMEM_LIMIT = 56 * 1024 * 1024
ROW_TILE = 256
ATTN_Q_TILE = 256
ATTN_KEY_STEP = 512
BISECT_STEPS = 14
REDUCE_SLAB = 64
ROW_SUBLANES = 16
GATHER_SRC_ROWS = 10240
SSM_CHUNK = 8
EXPERT_TILE = 256
CONVERT_ROWS = 128
COPY_ROWS = 256
ADA_COLS = 1024


def _params(*sem):
    return pltpu.CompilerParams(dimension_semantics=sem, vmem_limit_bytes=VMEM_LIMIT)


def _sigmoid(x):
    return 1.0 / (1.0 + jnp.exp(-x))


def _ada_kernel(c_ref, w_ref, b_ref, o_ref):
    c = c_ref[...]
    cond = (c * _sigmoid(c)).astype(BF16)
    o_ref[0] = jnp.dot(cond, w_ref[0].astype(BF16), preferred_element_type=F32) + b_ref[0]


def _ada_all(c_pad, w_ada, b_ada):
    depth, d, n6 = w_ada.shape
    rows = c_pad.shape[0]
    return pl.pallas_call(
        _ada_kernel,
        grid=(depth, n6 // ADA_COLS),
        in_specs=[pl.BlockSpec((rows, d), lambda l, j: (0, 0)),
                  pl.BlockSpec((1, d, ADA_COLS), lambda l, j: (l, 0, j)),
                  pl.BlockSpec((1, 1, ADA_COLS), lambda l, j: (l, 0, j))],
        out_specs=pl.BlockSpec((1, rows, ADA_COLS), lambda l, j: (l, 0, j)),
        out_shape=jax.ShapeDtypeStruct((depth, rows, n6), F32),
        compiler_params=_params("arbitrary", "arbitrary"),
        name="ada",
    )(c_pad, w_ada, b_ada.reshape(depth, 1, n6))


def _rope(xv, tab, half):
    w = xv.shape[1]
    rep = w // LANES
    c = jnp.tile(tab[:, 0:LANES], (1, rep))
    s1 = jnp.tile(tab[:, LANES:2 * LANES], (1, rep))
    s2 = jnp.tile(tab[:, 2 * LANES:3 * LANES], (1, rep))
    return xv * c + pltpu.roll(xv, w - half, 1) * s1 + pltpu.roll(xv, half, 1) * s2


def _inproj_kernel(x_ref, sc_ref, sh_ref, w_ref, wvt_ref, wiwt_ref, rq_ref, ri_ref,
                   q_ref, k_ref, vt_ref, iq_ref, ik_ref, iwt_ref, us_ref, up_ref, *, widths, idx_scale):
    aw, kw, iqw, sw, pw = widths
    u = (x_ref[0] * (1.0 + sc_ref[0]) + sh_ref[0]).astype(BF16)
    rq = rq_ref[0]
    ri = ri_ref[0]

    def mm(lo, n):
        return jnp.dot(u, w_ref[:, lo:lo + n], preferred_element_type=F32)

    def mm_t(wt_ref):
        return lax.dot_general(wt_ref[...], u, (((1,), (1,)), ((), ())), preferred_element_type=F32)

    qhalf = HEAD_DIM // ROPE_FRAC // 2
    ihalf = IDX_DIM // ROPE_FRAC // 2
    o = 0
    q_ref[0] = (_rope(mm(o, aw), rq, qhalf) * (HEAD_DIM ** -0.5)).astype(BF16)
    o += aw
    k_ref[0] = _rope(mm(o, kw), rq, qhalf).astype(BF16)
    o += kw
    iq_ref[0] = _rope(mm(o, iqw), ri, ihalf).astype(BF16)
    o += iqw
    us_ref[0] = mm(o, sw)
    o += sw
    up_ref[0] = mm(o, pw)
    o += pw
    ik_ref[0] = _rope(mm(o, LANES), ri, ihalf)[:, 0:IDX_DIM].astype(BF16)
    vt_ref[0] = mm_t(wvt_ref).astype(BF16)
    iwt_ref[0] = mm_t(wiwt_ref) * idx_scale


def _inproj(x, sc, sh, w_r, wvt, wiwt, rq, ri, widths):
    bsz, seq, d = x.shape
    aw, kw, iqw, sw, pw = widths
    tm = min(ROW_TILE, seq)
    tile = lambda n: pl.BlockSpec((1, tm, n), lambda b, i: (b, i, 0))
    tile_t = lambda n: pl.BlockSpec((1, n, tm), lambda b, i: (b, 0, i))
    vec = pl.BlockSpec((1, 1, d), lambda b, i: (b, 0, 0))
    full = lambda a: pl.BlockSpec(a.shape, lambda b, i: (0, 0))
    sds = jax.ShapeDtypeStruct
    return pl.pallas_call(
        functools.partial(_inproj_kernel, widths=widths, idx_scale=(IDX_DIM ** -0.5) * (IDX_HEADS ** -0.5)),
        grid=(bsz, seq // tm),
        in_specs=[tile(d), vec, vec, full(w_r), full(wvt), full(wiwt), tile(3 * LANES), tile(3 * LANES)],
        out_specs=[tile(aw), tile(kw), tile_t(kw), tile(iqw), tile(IDX_DIM), tile_t(IDX_HEADS), tile(sw), tile(pw)],
        out_shape=[sds((bsz, seq, aw), BF16), sds((bsz, seq, kw), BF16), sds((bsz, kw, seq), BF16),
                   sds((bsz, seq, iqw), BF16), sds((bsz, seq, IDX_DIM), BF16), sds((bsz, IDX_HEADS, seq), F32),
                   sds((bsz, seq, sw), F32), sds((bsz, seq, pw), F32)],
        compiler_params=_params("parallel", "arbitrary"),
        name="inproj",
    )(x, sc, sh, w_r, wvt, wiwt, rq, ri)


def _col_reduce(op, v):
    rows, n = v.shape
    slab = op(v.reshape(rows // REDUCE_SLAB, REDUCE_SLAB, n), axis=0)
    return op(slab, axis=0, keepdims=True)


def _attn_kernel(q_ref, iq_ref, iwt_ref, k_ref, vt_ref, ik_ref, o_ref, mask_ref, x_ref, *, q0, tq, keys, top_k):
    i = pl.program_id(1)
    t = q0 + i * tq + lax.broadcasted_iota(jnp.int32, (1, tq), 1)
    kpos = lax.broadcasted_iota(jnp.int32, (keys, 1), 0)
    causal = kpos <= t

    iq = iq_ref[0]
    ik = ik_ref[0]
    iwt = iwt_ref[0]
    isc = jnp.zeros((keys, tq), F32)
    for h in range(IDX_HEADS):
        r = lax.dot_general(ik, iq[:, h * IDX_DIM:(h + 1) * IDX_DIM], (((1,), (1,)), ((), ())),
                            preferred_element_type=F32)
        isc = isc + jnp.maximum(r, 0.0) * iwt[h:h + 1, :]
    x = jnp.where(causal, isc, -jnp.inf)
    x_ref[...] = x

    kf = float(top_k)
    n_valid = (t + 1).astype(F32)
    need = n_valid > kf

    def fold(step_fn, init):
        def body(s, acc):
            row = pl.multiple_of(s * REDUCE_SLAB, REDUCE_SLAB)
            return step_fn(acc, x_ref[pl.ds(row, REDUCE_SLAB), :])
        acc0 = jax.tree.map(lambda v: jnp.full((REDUCE_SLAB, tq), v, F32), init)
        return lax.fori_loop(0, keys // REDUCE_SLAB, body, acc0, unroll=4)

    def count_ge(theta):
        acc = fold(lambda a, xs: a + jnp.where(xs >= theta, 1.0, 0.0), 0.0)
        return jnp.sum(acc, axis=0, keepdims=True)

    rmax = _col_reduce(jnp.max, x)
    rmin = _col_reduce(jnp.min, jnp.where(causal, isc, jnp.inf))
    c_max = count_ge(rmax)
    top_tie = c_max >= kf
    lo0 = jnp.where(need, jnp.where(top_tie, rmax, rmin), -jnp.inf)
    clo0 = jnp.where(top_tie, c_max, n_valid)
    done0 = jnp.where(need & jnp.logical_not(top_tie) & (clo0 != kf), 0.0, 1.0)

    def bisect(lo, hi, clo, chi, done):
        mid = 0.5 * lo + 0.5 * hi
        c = count_ge(mid)
        live = done == 0.0
        ge = c >= kf
        up = live & ge
        dn = live & jnp.logical_not(ge)
        lo = jnp.where(up, mid, lo)
        clo = jnp.where(up, c, clo)
        hi = jnp.where(dn, mid, hi)
        chi = jnp.where(dn, c, chi)
        done = jnp.where(clo == kf, 1.0, done)
        return lo, hi, clo, chi, done

    def n_active(done):
        return jnp.sum(1.0 - done).astype(jnp.int32)

    def body2(s):
        _, lo, hi, clo, chi, done = s
        lo, hi, clo, chi, done = bisect(lo, hi, clo, chi, done)

        def min_max(acc, xs):
            inside = (xs >= lo) & (xs < hi)
            return (jnp.minimum(acc[0], jnp.where(inside, xs, jnp.inf)),
                    jnp.maximum(acc[1], jnp.where(inside, xs, -jnp.inf)))

        mins, maxs = fold(min_max, (jnp.inf, -jnp.inf))
        vmin = jnp.min(mins, axis=0, keepdims=True)
        vmax = jnp.max(maxs, axis=0, keepdims=True)
        c2 = count_ge(vmax)
        live = done == 0.0
        single = vmin == vmax
        top_ok = c2 >= kf
        take_top = live & jnp.logical_not(single) & top_ok
        drop_top = live & jnp.logical_not(single) & jnp.logical_not(top_ok)
        lo = jnp.where(live, jnp.where(take_top, vmax, vmin), lo)
        clo = jnp.where(take_top, c2, clo)
        hi = jnp.where(drop_top, vmax, hi)
        chi = jnp.where(drop_top, c2, chi)
        done = jnp.where(live & (single | top_ok), 1.0, done)
        return n_active(done), lo, hi, clo, chi, done

    state = lax.fori_loop(0, BISECT_STEPS, lambda _, s: bisect(*s), (lo0, rmax, clo0, c_max, done0))
    state = lax.while_loop(lambda s: s[0] > 0, body2, (n_active(state[4]),) + state)
    _, lo, _, clo, _, _ = state

    mask_ref[...] = jnp.where(causal & (x_ref[...] >= lo), 0.0, -1e30)

    tie_q = jnp.sum(jnp.where(need & (clo > kf), 1.0, 0.0)).astype(jnp.int32)

    @pl.when(tie_q > 0)
    def _():
        room = kf - _col_reduce(jnp.sum, jnp.where(x_ref[...] > lo, 1.0, 0.0))
        step = 256
        rr = lax.broadcasted_iota(jnp.int32, (step, step), 0)
        cc = lax.broadcasted_iota(jnp.int32, (step, step), 1)
        before = jnp.where(cc < rr, 1.0, 0.0).astype(BF16)
        carry = jnp.zeros((1, tq), F32)
        for c0 in range(0, keys, step):
            xs = x_ref[c0:c0 + step, :]
            e = jnp.where(xs == lo, 1.0, 0.0)
            rank = carry + jnp.dot(before, e.astype(BF16), preferred_element_type=F32)
            keep = (xs > lo) | ((xs == lo) & (rank < room))
            mask_ref[c0:c0 + step, :] = jnp.where(keep & (xs > -jnp.inf), 0.0, -1e30)
            carry = carry + jnp.sum(e, axis=0, keepdims=True)

    bias = mask_ref[...]
    q = q_ref[0]
    group = q.shape[1] // HEAD_DIM // KV_HEADS
    bias_g = jnp.concatenate([bias] * group, axis=1)
    for g in range(KV_HEADS):
        qg = jnp.concatenate([q[:, (g * group + j) * HEAD_DIM:(g * group + j + 1) * HEAD_DIM]
                              for j in range(group)], axis=0)
        kg = k_ref[0, :, g * HEAD_DIM:(g + 1) * HEAD_DIM]
        vtg = vt_ref[0, g * HEAD_DIM:(g + 1) * HEAD_DIM, :]
        s = lax.dot_general(kg, qg, (((1,), (1,)), ((), ())), preferred_element_type=F32) + bias_g
        m = _col_reduce(jnp.max, s)
        p = jnp.exp(s - m)
        den = _col_reduce(jnp.sum, p)
        og = jnp.dot(vtg, p.astype(BF16), preferred_element_type=F32) / den
        for j in range(group):
            h = g * group + j
            o_ref[0, :, h * HEAD_DIM:(h + 1) * HEAD_DIM] = og[:, j * tq:(j + 1) * tq].T.astype(BF16)


def _attention(q, k, vt, iq, ik, iwt, top_k):
    bsz, seq, aw = q.shape
    tq = min(ATTN_Q_TILE, seq)
    step = min(ATTN_KEY_STEP, seq)
    outs = []
    for q0 in range(0, seq, step):
        keys = q0 + step
        nq = step // tq
        qtile = lambda n, q0=q0: pl.BlockSpec((1, tq, n), lambda b, i: (b, q0 // tq + i, 0))
        ktile = lambda n, keys=keys: pl.BlockSpec((1, keys, n), lambda b, i: (b, 0, 0))
        outs.append(pl.pallas_call(
            functools.partial(_attn_kernel, q0=q0, tq=tq, keys=keys, top_k=top_k),
            grid=(bsz, nq),
            in_specs=[qtile(aw), qtile(iq.shape[2]),
                      pl.BlockSpec((1, iwt.shape[1], tq), lambda b, i, q0=q0: (b, 0, q0 // tq + i)),
                      ktile(k.shape[2]),
                      pl.BlockSpec((1, vt.shape[1], keys), lambda b, i: (b, 0, 0)),
                      ktile(ik.shape[2])],
            out_specs=pl.BlockSpec((1, tq, aw), lambda b, i: (b, i, 0)),
            out_shape=jax.ShapeDtypeStruct((bsz, step, aw), BF16),
            scratch_shapes=[pltpu.VMEM((keys, tq), F32), pltpu.VMEM((keys, tq), F32)],
            compiler_params=_params("parallel", "arbitrary"),
            name=f"attn_k{keys}",
        )(q, iq, iwt, k, vt, ik))
    return jnp.concatenate(outs, axis=1)


def _ssm_kernel(u_ref, kin_ref, wre_ref, wim_ref, vre_ref, vim_ref, are_ref, aim_ref, y_ref,
                sre_ref, sim_ref, xre_ref, xim_ref, *, chunk):
    bsz, seq, lanes = u_ref.shape
    nc = seq // chunk
    rows = bsz * nc
    ns = wre_ref.shape[2]
    u = u_ref[...].reshape(rows, chunk, lanes).reshape(rows, chunk * lanes).astype(BF16)
    sre_ref[...] = jnp.dot(u, wre_ref[0], preferred_element_type=F32).reshape(bsz, nc, ns)
    sim_ref[...] = jnp.dot(u, wim_ref[0], preferred_element_type=F32).reshape(bsz, nc, ns)
    a_re = are_ref[0]
    a_im = aim_ref[0]
    x_re = jnp.zeros((bsz, ns), F32)
    x_im = jnp.zeros((bsz, ns), F32)
    for c in range(nc):
        xre_ref[:, c, :] = x_re
        xim_ref[:, c, :] = x_im
        n_re = a_re * x_re - a_im * x_im + sre_ref[:, c, :]
        n_im = a_re * x_im + a_im * x_re + sim_ref[:, c, :]
        x_re, x_im = n_re, n_im
    xr = xre_ref[...].reshape(rows, ns).astype(BF16)
    xi = xim_ref[...].reshape(rows, ns).astype(BF16)
    y = (jnp.dot(u, kin_ref[0], preferred_element_type=F32)
         + jnp.dot(xr, vre_ref[0], preferred_element_type=F32)
         + jnp.dot(xi, vim_ref[0], preferred_element_type=F32))
    y_ref[...] = y.reshape(rows, chunk, lanes).reshape(bsz, seq, lanes)


def _ssm_prepare(lam_re, lam_im, log_step, b_re, b_im, c_re, c_im, d_skip, chunk):
    g, p = lam_re.shape
    ch = b_re.shape[2]
    hp = lax.Precision.HIGHEST
    step = jnp.exp(log_step)[:, None]
    lsr, lsi = lam_re * step, lam_im * step
    er = jnp.exp(lsr)
    nr, ni = er * jnp.cos(lsi) - 1.0, er * jnp.sin(lsi)
    den = lam_re * lam_re + lam_im * lam_im
    fr, fi = (nr * lam_re + ni * lam_im) / den, (ni * lam_re - nr * lam_im) / den
    bbr = fr[..., None] * b_re - fi[..., None] * b_im
    bbi = fr[..., None] * b_im + fi[..., None] * b_re
    n = jnp.arange(chunk + 1, dtype=F32)
    mag = jnp.exp(lsr[..., None] * n)
    pr, pi = mag * jnp.cos(lsi[..., None] * n), mag * jnp.sin(lsi[..., None] * n)
    prt, pit = pr[:, :, :chunk, None], pi[:, :, :chunk, None]
    wr = prt * bbr[:, :, None, :] - pit * bbi[:, :, None, :]
    wi = prt * bbi[:, :, None, :] + pit * bbr[:, :, None, :]
    ker = (jnp.einsum("gcp,gptd->gtcd", c_re, wr, precision=hp)
           - jnp.einsum("gcp,gptd->gtcd", c_im, wi, precision=hp))
    lag = jnp.arange(chunk)[None, :] - jnp.arange(chunk)[:, None]
    kexp = jnp.where((lag >= 0)[None, :, :, None, None], ker[:, jnp.clip(lag, 0, chunk - 1)], 0.0)
    kin = kexp.transpose(0, 1, 4, 2, 3)
    eye_t = jnp.eye(chunk, dtype=F32)
    eye_c = jnp.eye(ch, dtype=F32)
    skip = d_skip.reshape(g, ch)
    kin = kin + (eye_t[None, :, None, :, None] * eye_c[None, None, :, None, :] * skip[:, None, :, None, None])
    pr1, pi1 = pr[:, None, :, 1:], pi[:, None, :, 1:]
    vr = c_re[..., None] * pr1 - c_im[..., None] * pi1
    vi = c_re[..., None] * pi1 + c_im[..., None] * pr1
    gs = LANES // ch
    ns = g // gs
    eye = jnp.eye(gs, dtype=F32)
    lanes = chunk * gs * ch
    kin8 = jnp.einsum("saidjc,ab->siadjbc", kin.reshape(ns, gs, chunk, ch, chunk, ch), eye).reshape(ns, lanes, lanes)
    to_in = lambda w: jnp.einsum("sapid,ab->siadbp", w[:, :, ::-1, :].reshape(ns, gs, p, chunk, ch),
                                 eye).reshape(ns, lanes, gs * p)
    to_out = lambda v: jnp.einsum("sacpj,ab->sapjbc", v.reshape(ns, gs, ch, p, chunk),
                                  eye).reshape(ns, gs * p, lanes)
    a_re = pr[..., chunk].reshape(ns, 1, gs * p)
    a_im = pi[..., chunk].reshape(ns, 1, gs * p)
    return (kin8.astype(BF16), to_in(wr).astype(BF16), to_in(wi).astype(BF16),
            to_out(vr).astype(BF16), to_out(-vi).astype(BF16), a_re, a_im)


def _ssm(us, prep, chunk):
    bsz, seq, w = us.shape
    kin, wre, wim, vre, vim, a_re, a_im = prep
    ns, lanes, states = wre.shape
    nc = seq // chunk
    act = pl.BlockSpec((bsz, seq, LANES), lambda i: (0, 0, i))
    mat = lambda a, b: pl.BlockSpec((1, a, b), lambda i: (i, 0, 0))
    return pl.pallas_call(
        functools.partial(_ssm_kernel, chunk=chunk),
        grid=(ns,),
        in_specs=[act, mat(lanes, lanes), mat(lanes, states), mat(lanes, states), mat(states, lanes),
                  mat(states, lanes), mat(1, states), mat(1, states)],
        out_specs=act,
        out_shape=jax.ShapeDtypeStruct((bsz, seq, w), F32),
        scratch_shapes=[pltpu.VMEM((bsz, nc, states), F32)] * 4,
        compiler_params=_params("parallel"),
        name="ssm",
    )(us, kin, wre, wim, vre, vim, a_re, a_im)


def _row_tile(d):
    return (ROW_SUBLANES, d // ROW_SUBLANES)


def _to_rows(v):
    return v.reshape((v.shape[0],) + _row_tile(v.shape[1]))


def _from_rows(v):
    return v.reshape(v.shape[0], v.shape[1] * v.shape[2])


def _layer_norm(h, g, b):
    mu = jnp.mean(h, axis=1, keepdims=True)
    hc = h - mu
    var = jnp.mean(hc * hc, axis=1, keepdims=True)
    return hc * lax.rsqrt(var + LN_EPS) * g + b


def _route(logits_t):
    ng, per = N_EXPERT_GROUPS, EXPERTS_PER_GROUP
    m = jnp.max(logits_t, axis=0, keepdims=True)
    e = jnp.exp(logits_t - m)
    prob = e / jnp.sum(e, axis=0, keepdims=True)
    v = [prob[j * ng:(j + 1) * ng, :] for j in range(per)]

    def top(vals):
        best = functools.reduce(jnp.maximum, vals)
        idx = jnp.full(best.shape, per - 1, jnp.int32)
        for j in range(per - 2, -1, -1):
            idx = jnp.where(vals[j] == best, j, idx)
        return best, idx

    m1, i1 = top(v)
    m2, i2 = top([jnp.where(i1 == j, -1.0, v[j]) for j in range(per)])
    score = m1 + m2
    gid = lax.broadcasted_iota(jnp.int32, score.shape, 0)
    best_g = jnp.min(jnp.where(score == jnp.max(score, axis=0, keepdims=True), gid, ng), axis=0, keepdims=True)
    sel = gid == best_g
    pick_f = lambda a: jnp.sum(jnp.where(sel, a, 0.0), axis=0, keepdims=True)
    pick_i = lambda a: jnp.sum(jnp.where(sel, a, 0), axis=0, keepdims=True)
    p1, p2 = pick_f(m1), pick_f(m2)
    e1, e2 = best_g * per + pick_i(i1), best_g * per + pick_i(i2)
    den = p1 + p2
    return jnp.concatenate([e1, e2], axis=0), jnp.concatenate([p1 / den, p2 / den], axis=0)


def _mix_kernel(x_ref, ya_ref, ys_ref, upc_ref, upp_ref, g1_ref, sc2_ref, sh2_ref, lng_ref, lnb_ref,
                wout_ref, wglu_ref, bglu_ref, wpool_ref, pscale_ref, wr_ref,
                x1_ref, u2_ref, ids_ref, gates_ref, *, alpha):
    i = pl.program_id(1)
    tm = x_ref.shape[1]
    aw = ya_ref.shape[2]
    sw = ys_ref.shape[2]
    y = ys_ref[0]
    y = 0.5 * y * (1.0 + jnp.tanh(math.sqrt(2.0 / math.pi) * (y + 0.044715 * (y * y * y))))
    z = jnp.dot(y.astype(BF16), wglu_ref[...], preferred_element_type=F32) + bglu_ref[...]
    y = y * _sigmoid(z)
    upc = upc_ref[0]
    upp = jnp.where(i > 0, upp_ref[0], 0.0)
    cat = jnp.concatenate([upp, upc], axis=0).astype(BF16)
    r = lax.broadcasted_iota(jnp.int32, (tm, 1), 0)
    lagm = (r + tm) - lax.broadcasted_iota(jnp.int32, (1, 2 * tm), 1)
    tpos = (i * tm + r + 1).astype(F32)
    gc = upc.shape[1] // len(POOL_WINDOWS)
    pooled = []
    for g, win in enumerate(POOL_WINDOWS):
        band = jnp.where((lagm >= 0) & (lagm < win), 1.0, 0.0).astype(BF16)
        ws = jnp.dot(band, cat[:, g * gc:(g + 1) * gc], preferred_element_type=F32)
        pg = ws / jnp.minimum(tpos, float(win)) - upc[:, g * gc:(g + 1) * gc]
        pooled.append(jnp.dot(pg.astype(BF16), wpool_ref[g], preferred_element_type=F32))
    yp = jnp.concatenate(pooled, axis=1) * pscale_ref[...]
    mixed = jnp.concatenate([ya_ref[0], y.astype(BF16), yp.astype(BF16)], axis=1)
    mix = jnp.dot(mixed, wout_ref[...], preferred_element_type=F32)
    x1 = _layer_norm(alpha * x_ref[0] + (1.0 + g1_ref[0]) * mix, lng_ref[...], lnb_ref[...])
    x1_ref[0] = x1
    u2 = x1 * (1.0 + sc2_ref[0]) + sh2_ref[0]
    u2_ref[0] = _to_rows(u2.astype(BF16))
    u_hi = u2.astype(BF16)
    u_lo = (u2 - u_hi.astype(F32)).astype(BF16)
    logits = jnp.dot(jnp.concatenate([u_hi, u_hi, u_lo], axis=1), wr_ref[...], preferred_element_type=F32)
    ids, gates = _route(logits.T[0:N_EXPERTS, :])
    ids_ref[0] = ids
    gates_ref[0] = gates


def _mix(x, ya, ys, up, g1, sc2, sh2, lng, lnb, wout, wglu, bglu, wpool, pscale, wr_pad, alpha):
    bsz, seq, d = x.shape
    tm = min(ROW_TILE, seq)
    tile = lambda n: pl.BlockSpec((1, tm, n), lambda b, i: (b, i, 0))
    prev = pl.BlockSpec((1, tm, up.shape[2]), lambda b, i: (b, jnp.maximum(i - 1, 0), 0))
    vec = pl.BlockSpec((1, 1, d), lambda b, i: (b, 0, 0))
    full = lambda a: pl.BlockSpec(a.shape, lambda b, i: (0,) * a.ndim)
    lane_rows = pl.BlockSpec((1, 2, tm), lambda b, i: (b, 0, i))
    return pl.pallas_call(
        functools.partial(_mix_kernel, alpha=alpha),
        grid=(bsz, seq // tm),
        in_specs=[tile(d), tile(ya.shape[2]), tile(ys.shape[2]), tile(up.shape[2]), prev, vec, vec, vec,
                  full(lng), full(lnb), full(wout), full(wglu), full(bglu), full(wpool), full(pscale), full(wr_pad)],
        out_specs=[tile(d), pl.BlockSpec((1, tm) + _row_tile(d), lambda b, i: (b, i, 0, 0)), lane_rows, lane_rows],
        out_shape=[jax.ShapeDtypeStruct((bsz, seq, d), F32), jax.ShapeDtypeStruct((bsz, seq) + _row_tile(d), BF16),
                   jax.ShapeDtypeStruct((bsz, 2, seq), jnp.int32), jax.ShapeDtypeStruct((bsz, 2, seq), F32)],
        compiler_params=_params("parallel", "arbitrary"),
        name="mix",
    )(x, ya, ys, up, up, g1, sc2, sh2, lng, lnb, wout, wglu, bglu, wpool, pscale, wr_pad)


def _gather_kernel(idx_ref, src_ref, *rest, lo, n_src, partial_src, has_prev):
    if has_prev:
        prev_ref, out_ref, buf, sem = rest
    else:
        out_ref, buf, sem = rest

    @pl.when(pl.program_id(0) == 0)
    def _():
        cp = pltpu.make_async_copy(src_ref.at[pl.ds(lo, n_src)], buf, sem)
        cp.start()
        cp.wait()

    base = pl.program_id(0) * COPY_ROWS

    def move(r, carry):
        s = idx_ref[base + r]
        row = buf[jnp.maximum(s, 0)] if partial_src else buf[s]
        if has_prev:
            row = jnp.where(s >= 0, row, prev_ref[r])
        out_ref[r] = row
        return carry

    lax.fori_loop(0, COPY_ROWS, move, 0, unroll=8)


def _gather_rows(src, idx):
    n = idx.shape[0]
    n_chunks = pl.cdiv(src.shape[0], GATHER_SRC_ROWS)
    n_src = src.shape[0] // n_chunks
    assert n_src * n_chunks == src.shape[0]
    block = pl.BlockSpec((COPY_ROWS,) + src.shape[1:], lambda i, idx: (i, 0, 0))
    out = None
    for c in range(n_chunks):
        has_prev = c > 0
        local = idx - c * n_src
        local = idx if n_chunks == 1 else jnp.where((local >= 0) & (local < n_src), local, -1)
        out = pl.pallas_call(
            functools.partial(_gather_kernel, lo=c * n_src, n_src=n_src, partial_src=n_chunks > 1,
                              has_prev=has_prev),
            grid_spec=pltpu.PrefetchScalarGridSpec(
                num_scalar_prefetch=1, grid=(n // COPY_ROWS,),
                in_specs=[pl.BlockSpec(memory_space=pl.ANY)] + ([block] if has_prev else []),
                out_specs=block,
                scratch_shapes=[pltpu.VMEM((n_src,) + src.shape[1:], src.dtype), pltpu.SemaphoreType.DMA(())]),
            out_shape=jax.ShapeDtypeStruct((n,) + src.shape[1:], src.dtype),
            compiler_params=_params("arbitrary"),
            name="gather",
        )(*((local, src, out) if has_prev else (local, src)))
    return out


def _expert_kernel(te_ref, act_ref, first_ref, nxt_ref, x_ref, wg_hbm, wu_hbm, wd_hbm, y_ref,
                   stage_g, stage_u, stage_d, wg_ref, wu_ref, wd_ref, sems, *, layer):
    i = pl.program_id(0)

    def fetch(e):
        return (pltpu.make_async_copy(wg_hbm.at[layer, e], stage_g, sems.at[0]),
                pltpu.make_async_copy(wu_hbm.at[layer, e], stage_u, sems.at[1]),
                pltpu.make_async_copy(wd_hbm.at[layer, e], stage_d, sems.at[2]))

    @pl.when(i == 0)
    def _():
        for cp in fetch(te_ref[0]):
            cp.start()

    @pl.when(first_ref[i] != 0)
    def _():
        for cp in fetch(te_ref[i]):
            cp.wait()
        for stage, work in ((stage_g, wg_ref), (stage_u, wu_ref), (stage_d, wd_ref)):
            rows = stage.shape[0]

            def convert(r, carry, stage=stage, work=work):
                sl = pl.ds(pl.multiple_of(r * CONVERT_ROWS, CONVERT_ROWS), CONVERT_ROWS)
                work[sl, :] = stage[sl, :].astype(BF16)
                return carry

            lax.fori_loop(0, rows // CONVERT_ROWS, convert, 0)

        @pl.when(nxt_ref[i] >= 0)
        def _():
            for cp in fetch(nxt_ref[i]):
                cp.start()

    @pl.when(act_ref[i] != 0)
    def _():
        x = _from_rows(x_ref[...])
        g = jnp.dot(x, wg_ref[...], preferred_element_type=F32)
        u = jnp.dot(x, wu_ref[...], preferred_element_type=F32)
        h = (g * _sigmoid(g)) * u
        y_ref[...] = _to_rows(jnp.dot(h.astype(BF16), wd_ref[...], preferred_element_type=F32).astype(BF16))

    @pl.when(act_ref[i] == 0)
    def _():
        y_ref[...] = jnp.zeros(y_ref.shape, y_ref.dtype)


def _experts(xg, plan, wg, wu, wd, layer):
    rows = xg.shape[0]
    d, f = wg.shape[2], wg.shape[3]
    tm = EXPERT_TILE
    tile = pl.BlockSpec((tm,) + _row_tile(d), lambda i, *_: (i, 0, 0))
    hbm = pl.BlockSpec(memory_space=pl.ANY)
    return pl.pallas_call(
        functools.partial(_expert_kernel, layer=layer),
        grid_spec=pltpu.PrefetchScalarGridSpec(
            num_scalar_prefetch=4, grid=(rows // tm,),
            in_specs=[tile, hbm, hbm, hbm],
            out_specs=tile,
            scratch_shapes=[pltpu.VMEM((d, f), F32), pltpu.VMEM((d, f), F32), pltpu.VMEM((f, d), F32),
                            pltpu.VMEM((d, f), BF16), pltpu.VMEM((d, f), BF16), pltpu.VMEM((f, d), BF16),
                            pltpu.SemaphoreType.DMA((3,))]),
        out_shape=jax.ShapeDtypeStruct((rows,) + _row_tile(d), BF16),
        compiler_params=_params("arbitrary"),
        name="experts",
    )(*plan, xg, wg, wu, wd)


def _combine_kernel(x_ref, z_ref, gc_ref, g2_ref, lng_ref, lnb_ref, o_ref, *, alpha):
    gc = gc_ref[0]
    ffn = (gc[:, 0:1] * _from_rows(z_ref[0, 0]).astype(F32)
           + gc[:, 1:2] * _from_rows(z_ref[1, 0]).astype(F32))
    o_ref[0] = _layer_norm(alpha * x_ref[0] + (1.0 + g2_ref[0]) * ffn, lng_ref[...], lnb_ref[...])


def _combine(x1, z, gates_col, g2, lng, lnb, alpha):
    bsz, seq, d = x1.shape
    tm = min(ROW_TILE, seq)
    tile = pl.BlockSpec((1, tm, d), lambda b, i: (b, i, 0))
    return pl.pallas_call(
        functools.partial(_combine_kernel, alpha=alpha),
        grid=(bsz, seq // tm),
        in_specs=[tile, pl.BlockSpec((2, 1, tm) + _row_tile(d), lambda b, i: (0, b, i, 0, 0)),
                  pl.BlockSpec((1, tm, 2), lambda b, i: (b, i, 0)),
                  pl.BlockSpec((1, 1, d), lambda b, i: (b, 0, 0)),
                  pl.BlockSpec(lng.shape, lambda b, i: (0, 0)), pl.BlockSpec(lnb.shape, lambda b, i: (0, 0))],
        out_specs=tile,
        out_shape=jax.ShapeDtypeStruct((bsz, seq, d), F32),
        compiler_params=_params("parallel", "arbitrary"),
        name="combine",
    )(x1, z, gates_col, g2, lng, lnb)


def _dispatch_plan(ids, n_tok):
    tm = EXPERT_TILE
    e_pair = jnp.concatenate([ids[:, 0, :].reshape(n_tok), ids[:, 1, :].reshape(n_tok)])
    tok_pair = jnp.concatenate([jnp.arange(n_tok, dtype=jnp.int32)] * 2)
    onehot = (e_pair[:, None] == jnp.arange(N_EXPERTS, dtype=jnp.int32)[None, :]).astype(jnp.int32)
    csum = jnp.cumsum(onehot, axis=0)
    rank = jnp.sum(onehot * csum, axis=1) - 1
    counts = csum[-1]
    padded = ((counts + tm - 1) // tm) * tm
    ends = jnp.cumsum(padded)
    offs = ends - padded
    pos = jnp.sum(onehot * offs[None, :], axis=1) + rank
    rows = 2 * n_tok + N_EXPERTS * tm
    src_token = jnp.zeros((rows,), jnp.int32).at[pos].set(tok_pair)
    tile_start = jnp.arange(rows // tm, dtype=jnp.int32) * tm
    tile_expert = jnp.minimum(jnp.sum((tile_start[:, None] >= ends[None, :]).astype(jnp.int32), axis=1),
                              N_EXPERTS - 1)
    tile_active = (tile_start < ends[-1]).astype(jnp.int32)
    prev_expert = jnp.concatenate([jnp.full((1,), -1, jnp.int32), tile_expert[:-1]])
    tile_first = tile_active * (tile_expert != prev_expert).astype(jnp.int32)
    eid = jnp.arange(N_EXPERTS, dtype=jnp.int32)
    later = (padded > 0)[None, :] & (eid[None, :] > eid[:, None])
    next_expert = jnp.min(jnp.where(later, eid[None, :], N_EXPERTS), axis=1)
    next_expert = jnp.where(next_expert == N_EXPERTS, -1, next_expert).astype(jnp.int32)
    tile_next = next_expert[tile_expert]
    return pos.astype(jnp.int32), src_token, (tile_expert, tile_active, tile_first, tile_next)


def _rope_tables(positions, dim):
    rot = dim // ROPE_FRAC
    half = rot // 2
    inv_freq = ROPE_THETA ** (-jnp.arange(half, dtype=F32) * 2.0 / rot)
    ang = positions.astype(F32)[..., None] * inv_freq
    cos, sin = jnp.cos(ang), jnp.sin(ang)
    shape = ang.shape[:-1]
    one = jnp.ones(shape + (dim - rot,), F32)
    zero = jnp.zeros(shape + (dim - rot,), F32)
    zh = jnp.zeros(shape + (half,), F32)
    c = jnp.concatenate([cos, cos, one], axis=-1)
    s1 = jnp.concatenate([-sin, zh, zero], axis=-1)
    s2 = jnp.concatenate([zh, sin, zero], axis=-1)
    rep = LANES // dim
    return jnp.concatenate([jnp.tile(a, (1, 1, rep)) for a in (c, s1, s2)], axis=-1)


def kernel(x, c, positions, w_ada, b_ada, w_in, w_out, ssm_lam_re, ssm_lam_im, ssm_log_step, ssm_b_re, ssm_b_im, ssm_c_re, ssm_c_im, ssm_d, ssm_w_glu, ssm_b_glu, pool_w, pool_scale, ln1_g, ln1_b, ln2_g, ln2_b, w_router, e_gate, e_up, e_down):
    bsz, seq, d = x.shape
    depth = w_ada.shape[0]
    n_tok = bsz * seq
    alpha = (2.0 * depth) ** 0.25
    sw = ssm_d.shape[1]
    pw = pool_scale.shape[1]
    aw = w_out.shape[1] - sw - pw
    kw = KV_HEADS * HEAD_DIM
    iqw = IDX_HEADS * IDX_DIM
    widths = (aw, kw, iqw, sw, pw)
    top_k = min(MAX_TOPK, seq // 4)

    c_pad = jnp.concatenate([c, jnp.zeros((8 - bsz % 8, d), F32)], axis=0) if bsz % 8 else c
    ada = _ada_all(c_pad, w_ada, b_ada)
    rq = _rope_tables(positions, HEAD_DIM)
    ri = _rope_tables(positions, IDX_DIM)
    wr_pad = jnp.concatenate(
        [w_router.reshape(d, N_EXPERT_GROUPS, EXPERTS_PER_GROUP).transpose(0, 2, 1).reshape(d, N_EXPERTS),
         jnp.zeros((d, LANES - N_EXPERTS), F32)], axis=1)
    wr_hi = wr_pad.astype(BF16)
    wr_lo = (wr_pad - wr_hi.astype(F32)).astype(BF16)
    wr_pad = jnp.concatenate([wr_hi, wr_lo, wr_hi], axis=0)

    o_q, o_k, o_v, o_iq, o_ik, o_iw, o_us, o_up = (0, aw, aw + kw, aw + 2 * kw, aw + 2 * kw + iqw,
                                                   aw + 2 * kw + iqw + IDX_DIM,
                                                   aw + 2 * kw + iqw + IDX_DIM + IDX_HEADS,
                                                   aw + 2 * kw + iqw + IDX_DIM + IDX_HEADS + sw)
    for l in range(depth):
        sh1, sc1, g1, sh2, sc2, g2 = [ada[l, :bsz, j * d:(j + 1) * d].reshape(bsz, 1, d) for j in range(6)]
        wl = w_in[l]
        w_r = jnp.concatenate([wl[:, o_q:o_v], wl[:, o_iq:o_ik], wl[:, o_us:], wl[:, o_ik:o_iw],
                               jnp.zeros((d, LANES - IDX_DIM), F32)], axis=1).astype(BF16)
        wvt = wl[:, o_v:o_iq].T.astype(BF16)
        wiwt = wl[:, o_iw:o_us].T.astype(BF16)
        q, k, vt, iq, ik, iwt, us, up = _inproj(x, sc1, sh1, w_r, wvt, wiwt, rq, ri, widths)
        ya = _attention(q, k, vt, iq, ik, iwt, top_k)
        prep = _ssm_prepare(ssm_lam_re[l], ssm_lam_im[l], ssm_log_step[l], ssm_b_re[l], ssm_b_im[l],
                            ssm_c_re[l], ssm_c_im[l], ssm_d[l], SSM_CHUNK)
        ys = _ssm(us, prep, SSM_CHUNK)
        x1, u2, ids, gates = _mix(x, ya, ys, up, g1, sc2, sh2, ln1_g[l][None], ln1_b[l][None],
                                  w_out[l].astype(BF16), ssm_w_glu[l].astype(BF16), ssm_b_glu[l][None],
                                  pool_w[l].astype(BF16), pool_scale[l][None], wr_pad, alpha)
        pos, src_token, tile_plan = _dispatch_plan(ids, n_tok)
        xg = _gather_rows(u2.reshape((n_tok,) + _row_tile(d)), src_token)
        yg = _experts(xg, tile_plan, e_gate, e_up, e_down, l)
        z = _gather_rows(yg, pos).reshape((2, bsz, seq) + _row_tile(d))
        x = _combine(x1, z, gates.transpose(0, 2, 1), g2, ln2_g[l][None], ln2_b[l][None], alpha)
    return x
```

```python
import functools
import math

import jax
import jax.numpy as jnp
import numpy as np
from jax import lax
from jax.experimental import pallas as pl
from jax.experimental.pallas import tpu as pltpu

F32 = jnp.float32
BF16 = jnp.bfloat16

HEAD_DIM = 128
KV_HEADS = 2
IDX_HEADS = 8
IDX_DIM = 64
MAX_TOPK = 256
SSM_GROUP_CH = 16
SSM_STATE = 64
POOL_WINDOWS = (2, 4, 8, 16)
ROPE_THETA = 500000.0
ROPE_FRAC = 4
LN_EPS = 1e-5
N_EXPERTS = 16
N_EXPERT_GROUPS = 4
EXPERTS_PER_GROUP = N_EXPERTS // N_EXPERT_GROUPS

LANES = 128
VMEM_LIMIT = 56 * 1024 * 1024
ROW_TILE = 256
ATTN_Q_TILE = 256
ATTN_KEY_STEP = 512
BISECT_STEPS = 14
REDUCE_SLAB = 64
ROW_SUBLANES = 16
GATHER_SRC_ROWS = 10240
SSM_CHUNK = 8
EXPERT_TILE = 256
CONVERT_ROWS = 128
COPY_ROWS = 256
ADA_COLS = 1024


def _params(*sem):
    return pltpu.CompilerParams(dimension_semantics=sem, vmem_limit_bytes=VMEM_LIMIT)


def _sigmoid(x):
    return 1.0 / (1.0 + jnp.exp(-x))


def _ada_kernel(c_ref, w_ref, b_ref, o_ref):
    c = c_ref[...]
    cond = (c * _sigmoid(c)).astype(BF16)
    o_ref[0] = jnp.dot(cond, w_ref[0].astype(BF16), preferred_element_type=F32) + b_ref[0]


def _ada_all(c_pad, w_ada, b_ada):
    depth, d, n6 = w_ada.shape
    rows = c_pad.shape[0]
    return pl.pallas_call(
        _ada_kernel,
        grid=(depth, n6 // ADA_COLS),
        in_specs=[pl.BlockSpec((rows, d), lambda l, j: (0, 0)),
                  pl.BlockSpec((1, d, ADA_COLS), lambda l, j: (l, 0, j)),
                  pl.BlockSpec((1, 1, ADA_COLS), lambda l, j: (l, 0, j))],
        out_specs=pl.BlockSpec((1, rows, ADA_COLS), lambda l, j: (l, 0, j)),
        out_shape=jax.ShapeDtypeStruct((depth, rows, n6), F32),
        compiler_params=_params("arbitrary", "arbitrary"),
        name="ada",
    )(c_pad, w_ada, b_ada.reshape(depth, 1, n6))


def _rope(xv, tab, half):
    w = xv.shape[1]
    rep = w // LANES
    c = jnp.tile(tab[:, 0:LANES], (1, rep))
    s1 = jnp.tile(tab[:, LANES:2 * LANES], (1, rep))
    s2 = jnp.tile(tab[:, 2 * LANES:3 * LANES], (1, rep))
    return xv * c + pltpu.roll(xv, w - half, 1) * s1 + pltpu.roll(xv, half, 1) * s2


def _inproj_kernel(x_ref, sc_ref, sh_ref, w_ref, wvt_ref, wiwt_ref, rq_ref, ri_ref,
                   q_ref, k_ref, vt_ref, iq_ref, ik_ref, iwt_ref, us_ref, up_ref, *, widths, idx_scale):
    aw, kw, iqw, sw, pw = widths
    u = (x_ref[0] * (1.0 + sc_ref[0]) + sh_ref[0]).astype(BF16)
    rq = rq_ref[0]
    ri = ri_ref[0]

    def mm(lo, n):
        return jnp.dot(u, w_ref[:, lo:lo + n], preferred_element_type=F32)

    def mm_t(wt_ref):
        return lax.dot_general(wt_ref[...], u, (((1,), (1,)), ((), ())), preferred_element_type=F32)

    qhalf = HEAD_DIM // ROPE_FRAC // 2
    ihalf = IDX_DIM // ROPE_FRAC // 2
    o = 0
    q_ref[0] = (_rope(mm(o, aw), rq, qhalf) * (HEAD_DIM ** -0.5)).astype(BF16)
    o += aw
    k_ref[0] = _rope(mm(o, kw), rq, qhalf).astype(BF16)
    o += kw
    iq_ref[0] = _rope(mm(o, iqw), ri, ihalf).astype(BF16)
    o += iqw
    us_ref[0] = mm(o, sw)
    o += sw
    up_ref[0] = mm(o, pw)
    o += pw
    ik_ref[0] = _rope(mm(o, LANES), ri, ihalf)[:, 0:IDX_DIM].astype(BF16)
    vt_ref[0] = mm_t(wvt_ref).astype(BF16)
    iwt_ref[0] = mm_t(wiwt_ref) * idx_scale


def _inproj(x, sc, sh, w_r, wvt, wiwt, rq, ri, widths):
    bsz, seq, d = x.shape
    aw, kw, iqw, sw, pw = widths
    tm = min(ROW_TILE, seq)
    tile = lambda n: pl.BlockSpec((1, tm, n), lambda b, i: (b, i, 0))
    tile_t = lambda n: pl.BlockSpec((1, n, tm), lambda b, i: (b, 0, i))
    vec = pl.BlockSpec((1, 1, d), lambda b, i: (b, 0, 0))
    full = lambda a: pl.BlockSpec(a.shape, lambda b, i: (0, 0))
    sds = jax.ShapeDtypeStruct
    return pl.pallas_call(
        functools.partial(_inproj_kernel, widths=widths, idx_scale=(IDX_DIM ** -0.5) * (IDX_HEADS ** -0.5)),
        grid=(bsz, seq // tm),
        in_specs=[tile(d), vec, vec, full(w_r), full(wvt), full(wiwt), tile(3 * LANES), tile(3 * LANES)],
        out_specs=[tile(aw), tile(kw), tile_t(kw), tile(iqw), tile(IDX_DIM), tile_t(IDX_HEADS), tile(sw), tile(pw)],
        out_shape=[sds((bsz, seq, aw), BF16), sds((bsz, seq, kw), BF16), sds((bsz, kw, seq), BF16),
                   sds((bsz, seq, iqw), BF16), sds((bsz, seq, IDX_DIM), BF16), sds((bsz, IDX_HEADS, seq), F32),
                   sds((bsz, seq, sw), F32), sds((bsz, seq, pw), F32)],
        compiler_params=_params("parallel", "arbitrary"),
        name="inproj",
    )(x, sc, sh, w_r, wvt, wiwt, rq, ri)


def _col_reduce(op, v):
    rows, n = v.shape
    slab = op(v.reshape(rows // REDUCE_SLAB, REDUCE_SLAB, n), axis=0)
    return op(slab, axis=0, keepdims=True)


def _attn_kernel(q_ref, iq_ref, iwt_ref, k_ref, vt_ref, ik_ref, o_ref, mask_ref, x_ref, *, q0, tq, keys, top_k):
    i = pl.program_id(1)
    t = q0 + i * tq + lax.broadcasted_iota(jnp.int32, (1, tq), 1)
    kpos = lax.broadcasted_iota(jnp.int32, (keys, 1), 0)
    causal = kpos <= t

    iq = iq_ref[0]
    ik = ik_ref[0]
    iwt = iwt_ref[0]
    isc = jnp.zeros((keys, tq), F32)
    for h in range(IDX_HEADS):
        r = lax.dot_general(ik, iq[:, h * IDX_DIM:(h + 1) * IDX_DIM], (((1,), (1,)), ((), ())),
                            preferred_element_type=F32)
        isc = isc + jnp.maximum(r, 0.0) * iwt[h:h + 1, :]
    x = jnp.where(causal, isc, -jnp.inf)
    x_ref[...] = x

    kf = float(top_k)
    n_valid = (t + 1).astype(F32)
    need = n_valid > kf

    def fold(step_fn, init):
        def body(s, acc):
            row = pl.multiple_of(s * REDUCE_SLAB, REDUCE_SLAB)
            return step_fn(acc, x_ref[pl.ds(row, REDUCE_SLAB), :])
        acc0 = jax.tree.map(lambda v: jnp.full((REDUCE_SLAB, tq), v, F32), init)
        return lax.fori_loop(0, keys // REDUCE_SLAB, body, acc0, unroll=4)

    def count_ge(theta):
        acc = fold(lambda a, xs: a + jnp.where(xs >= theta, 1.0, 0.0), 0.0)
        return jnp.sum(acc, axis=0, keepdims=True)

    rmax = _col_reduce(jnp.max, x)
    rmin = _col_reduce(jnp.min, jnp.where(causal, isc, jnp.inf))
    c_max = count_ge(rmax)
    top_tie = c_max >= kf
    lo0 = jnp.where(need, jnp.where(top_tie, rmax, rmin), -jnp.inf)
    clo0 = jnp.where(top_tie, c_max, n_valid)
    done0 = jnp.where(need & jnp.logical_not(top_tie) & (clo0 != kf), 0.0, 1.0)

    def bisect(lo, hi, clo, chi, done):
        mid = 0.5 * lo + 0.5 * hi
        c = count_ge(mid)
        live = done == 0.0
        ge = c >= kf
        up = live & ge
        dn = live & jnp.logical_not(ge)
        lo = jnp.where(up, mid, lo)
        clo = jnp.where(up, c, clo)
        hi = jnp.where(dn, mid, hi)
        chi = jnp.where(dn, c, chi)
        done = jnp.where(clo == kf, 1.0, done)
        return lo, hi, clo, chi, done

    def n_active(done):
        return jnp.sum(1.0 - done).astype(jnp.int32)

    def body2(s):
        _, lo, hi, clo, chi, done = s
        lo, hi, clo, chi, done = bisect(lo, hi, clo, chi, done)

        def min_max(acc, xs):
            inside = (xs >= lo) & (xs < hi)
            return (jnp.minimum(acc[0], jnp.where(inside, xs, jnp.inf)),
                    jnp.maximum(acc[1], jnp.where(inside, xs, -jnp.inf)))

        mins, maxs = fold(min_max, (jnp.inf, -jnp.inf))
        vmin = jnp.min(mins, axis=0, keepdims=True)
        vmax = jnp.max(maxs, axis=0, keepdims=True)
        c2 = count_ge(vmax)
        live = done == 0.0
        single = vmin == vmax
        top_ok = c2 >= kf
        take_top = live & jnp.logical_not(single) & top_ok
        drop_top = live & jnp.logical_not(single) & jnp.logical_not(top_ok)
        lo = jnp.where(live, jnp.where(take_top, vmax, vmin), lo)
        clo = jnp.where(take_top, c2, clo)
        hi = jnp.where(drop_top, vmax, hi)
        chi = jnp.where(drop_top, c2, chi)
        done = jnp.where(live & (single | top_ok), 1.0, done)
        return n_active(done), lo, hi, clo, chi, done

    state = lax.fori_loop(0, BISECT_STEPS, lambda _, s: bisect(*s), (lo0, rmax, clo0, c_max, done0))
    state = lax.while_loop(lambda s: s[0] > 0, body2, (n_active(state[4]),) + state)
    _, lo, _, clo, _, _ = state

    mask_ref[...] = jnp.where(causal & (x_ref[...] >= lo), 0.0, -1e30)

    tie_q = jnp.sum(jnp.where(need & (clo > kf), 1.0, 0.0)).astype(jnp.int32)

    @pl.when(tie_q > 0)
    def _():
        room = kf - _col_reduce(jnp.sum, jnp.where(x_ref[...] > lo, 1.0, 0.0))
        step = 256
        rr = lax.broadcasted_iota(jnp.int32, (step, step), 0)
        cc = lax.broadcasted_iota(jnp.int32, (step, step), 1)
        before = jnp.where(cc < rr, 1.0, 0.0).astype(BF16)
        carry = jnp.zeros((1, tq), F32)
        for c0 in range(0, keys, step):
            xs = x_ref[c0:c0 + step, :]
            e = jnp.where(xs == lo, 1.0, 0.0)
            rank = carry + jnp.dot(before, e.astype(BF16), preferred_element_type=F32)
            keep = (xs > lo) | ((xs == lo) & (rank < room))
            mask_ref[c0:c0 + step, :] = jnp.where(keep & (xs > -jnp.inf), 0.0, -1e30)
            carry = carry + jnp.sum(e, axis=0, keepdims=True)

    bias = mask_ref[...]
    q = q_ref[0]
    group = q.shape[1] // HEAD_DIM // KV_HEADS
    bias_g = jnp.concatenate([bias] * group, axis=1)
    for g in range(KV_HEADS):
        qg = jnp.concatenate([q[:, (g * group + j) * HEAD_DIM:(g * group + j + 1) * HEAD_DIM]
                              for j in range(group)], axis=0)
        kg = k_ref[0, :, g * HEAD_DIM:(g + 1) * HEAD_DIM]
        vtg = vt_ref[0, g * HEAD_DIM:(g + 1) * HEAD_DIM, :]
        s = lax.dot_general(kg, qg, (((1,), (1,)), ((), ())), preferred_element_type=F32) + bias_g
        m = _col_reduce(jnp.max, s)
        p = jnp.exp(s - m)
        den = _col_reduce(jnp.sum, p)
        og = jnp.dot(vtg, p.astype(BF16), preferred_element_type=F32) / den
        for j in range(group):
            h = g * group + j
            o_ref[0, :, h * HEAD_DIM:(h + 1) * HEAD_DIM] = og[:, j * tq:(j + 1) * tq].T.astype(BF16)


def _attention(q, k, vt, iq, ik, iwt, top_k):
    bsz, seq, aw = q.shape
    tq = min(ATTN_Q_TILE, seq)
    step = min(ATTN_KEY_STEP, seq)
    outs = []
    for q0 in range(0, seq, step):
        keys = q0 + step
        nq = step // tq
        qtile = lambda n, q0=q0: pl.BlockSpec((1, tq, n), lambda b, i: (b, q0 // tq + i, 0))
        ktile = lambda n, keys=keys: pl.BlockSpec((1, keys, n), lambda b, i: (b, 0, 0))
        outs.append(pl.pallas_call(
            functools.partial(_attn_kernel, q0=q0, tq=tq, keys=keys, top_k=top_k),
            grid=(bsz, nq),
            in_specs=[qtile(aw), qtile(iq.shape[2]),
                      pl.BlockSpec((1, iwt.shape[1], tq), lambda b, i, q0=q0: (b, 0, q0 // tq + i)),
                      ktile(k.shape[2]),
                      pl.BlockSpec((1, vt.shape[1], keys), lambda b, i: (b, 0, 0)),
                      ktile(ik.shape[2])],
            out_specs=pl.BlockSpec((1, tq, aw), lambda b, i: (b, i, 0)),
            out_shape=jax.ShapeDtypeStruct((bsz, step, aw), BF16),
            scratch_shapes=[pltpu.VMEM((keys, tq), F32), pltpu.VMEM((keys, tq), F32)],
            compiler_params=_params("parallel", "arbitrary"),
            name=f"attn_k{keys}",
        )(q, iq, iwt, k, vt, ik))
    return jnp.concatenate(outs, axis=1)


def _ssm_kernel(u_ref, kin_ref, wre_ref, wim_ref, vre_ref, vim_ref, are_ref, aim_ref, y_ref,
                sre_ref, sim_ref, xre_ref, xim_ref, *, chunk):
    bsz, seq, lanes = u_ref.shape
    nc = seq // chunk
    rows = bsz * nc
    ns = wre_ref.shape[2]
    u = u_ref[...].reshape(rows, chunk, lanes).reshape(rows, chunk * lanes).astype(BF16)
    sre_ref[...] = jnp.dot(u, wre_ref[0], preferred_element_type=F32).reshape(bsz, nc, ns)
    sim_ref[...] = jnp.dot(u, wim_ref[0], preferred_element_type=F32).reshape(bsz, nc, ns)
    a_re = are_ref[0]
    a_im = aim_ref[0]
    x_re = jnp.zeros((bsz, ns), F32)
    x_im = jnp.zeros((bsz, ns), F32)
    for c in range(nc):
        xre_ref[:, c, :] = x_re
        xim_ref[:, c, :] = x_im
        n_re = a_re * x_re - a_im * x_im + sre_ref[:, c, :]
        n_im = a_re * x_im + a_im * x_re + sim_ref[:, c, :]
        x_re, x_im = n_re, n_im
    xr = xre_ref[...].reshape(rows, ns).astype(BF16)
    xi = xim_ref[...].reshape(rows, ns).astype(BF16)
    y = (jnp.dot(u, kin_ref[0], preferred_element_type=F32)
         + jnp.dot(xr, vre_ref[0], preferred_element_type=F32)
         + jnp.dot(xi, vim_ref[0], preferred_element_type=F32))
    y_ref[...] = y.reshape(rows, chunk, lanes).reshape(bsz, seq, lanes)


def _ssm_placement(gs, chunk, ch, p):
    r = np.arange(chunk * gs * ch)
    x = np.arange(chunk * ch)
    a = np.arange(gs)[:, None, None]
    place = (((r // ch) % gs)[None, :, None] == a) & ((r // (gs * ch))[None, :, None] == (x // ch)[None, None, :]) \
        & ((r % ch)[None, :, None] == (x % ch)[None, None, :])
    q = np.arange(gs * p)
    place_state = ((q // p)[None, None, :] == a) & ((q % p)[None, None, :] == np.arange(p)[None, :, None])
    return jnp.asarray(place, F32), jnp.asarray(place_state, F32)


def _ssm_prepare(lam_re, lam_im, log_step, b_re, b_im, c_re, c_im, d_skip, chunk):
    g, p = lam_re.shape
    ch = b_re.shape[2]
    hp = lax.Precision.HIGHEST
    step = jnp.exp(log_step)[:, None]
    lsr, lsi = lam_re * step, lam_im * step
    er = jnp.exp(lsr)
    nr, ni = er * jnp.cos(lsi) - 1.0, er * jnp.sin(lsi)
    den = lam_re * lam_re + lam_im * lam_im
    fr, fi = (nr * lam_re + ni * lam_im) / den, (ni * lam_re - nr * lam_im) / den
    bbr = fr[..., None] * b_re - fi[..., None] * b_im
    bbi = fr[..., None] * b_im + fi[..., None] * b_re
    n = jnp.arange(chunk + 1, dtype=F32)
    mag = jnp.exp(lsr[..., None] * n)
    pr, pi = mag * jnp.cos(lsi[..., None] * n), mag * jnp.sin(lsi[..., None] * n)
    prt, pit = pr[:, :, :chunk, None], pi[:, :, :chunk, None]
    wr = prt * bbr[:, :, None, :] - pit * bbi[:, :, None, :]
    wi = prt * bbi[:, :, None, :] + pit * bbr[:, :, None, :]
    ker = (jnp.einsum("gcp,gptd->gtcd", c_re, wr, precision=hp)
           - jnp.einsum("gcp,gptd->gtcd", c_im, wi, precision=hp))
    lag = jnp.arange(chunk)[None, :] - jnp.arange(chunk)[:, None]
    kexp = jnp.where((lag >= 0)[None, :, :, None, None], ker[:, jnp.clip(lag, 0, chunk - 1)], 0.0)
    kin = kexp.transpose(0, 1, 4, 2, 3)
    eye_t = jnp.eye(chunk, dtype=F32)
    eye_c = jnp.eye(ch, dtype=F32)
    skip = d_skip.reshape(g, ch)
    kin = kin + (eye_t[None, :, None, :, None] * eye_c[None, None, :, None, :] * skip[:, None, :, None, None])
    pr1, pi1 = pr[:, None, :, 1:], pi[:, None, :, 1:]
    vr = c_re[..., None] * pr1 - c_im[..., None] * pi1
    vi = c_re[..., None] * pi1 + c_im[..., None] * pr1
    gs = LANES // ch
    ns = g // gs
    lanes = chunk * gs * ch
    place, place_state = _ssm_placement(gs, chunk, ch, p)
    bf16_exact = lambda m: m.astype(BF16).astype(F32)
    spread = lambda m: jnp.einsum("arx,saxc->src", place, m).astype(BF16)
    k2 = bf16_exact(kin.reshape(ns, gs, chunk * ch, chunk * ch))
    kin8 = spread(jnp.einsum("saxy,acy->saxc", k2, place))

    def to_in(w):
        w2 = bf16_exact(w[:, :, ::-1, :].transpose(0, 2, 3, 1).reshape(ns, gs, chunk * ch, p))
        return spread(jnp.einsum("saxp,apq->saxq", w2, place_state))

    def to_out(v):
        v2 = bf16_exact(v.transpose(0, 2, 3, 1).reshape(ns, gs, p, chunk * ch))
        return jnp.einsum("sapy,acy->sapc", v2, place).astype(BF16).reshape(ns, gs * p, lanes)

    a_re = pr[..., chunk].reshape(ns, 1, gs * p)
    a_im = pi[..., chunk].reshape(ns, 1, gs * p)
    return kin8, to_in(wr), to_in(wi), to_out(vr), to_out(-vi), a_re, a_im


def _ssm(us, prep, chunk):
    bsz, seq, w = us.shape
    kin, wre, wim, vre, vim, a_re, a_im = prep
    ns, lanes, states = wre.shape
    nc = seq // chunk
    act = pl.BlockSpec((bsz, seq, LANES), lambda i: (0, 0, i))
    mat = lambda a, b: pl.BlockSpec((1, a, b), lambda i: (i, 0, 0))
    return pl.pallas_call(
        functools.partial(_ssm_kernel, chunk=chunk),
        grid=(ns,),
        in_specs=[act, mat(lanes, lanes), mat(lanes, states), mat(lanes, states), mat(states, lanes),
                  mat(states, lanes), mat(1, states), mat(1, states)],
        out_specs=act,
        out_shape=jax.ShapeDtypeStruct((bsz, seq, w), F32),
        scratch_shapes=[pltpu.VMEM((bsz, nc, states), F32)] * 4,
        compiler_params=_params("parallel"),
        name="ssm",
    )(us, kin, wre, wim, vre, vim, a_re, a_im)


def _row_tile(d):
    return (ROW_SUBLANES, d // ROW_SUBLANES)


def _to_rows(v):
    return v.reshape((v.shape[0],) + _row_tile(v.shape[1]))


def _from_rows(v):
    return v.reshape(v.shape[0], v.shape[1] * v.shape[2])


def _layer_norm(h, g, b):
    mu = jnp.mean(h, axis=1, keepdims=True)
    hc = h - mu
    var = jnp.mean(hc * hc, axis=1, keepdims=True)
    return hc * lax.rsqrt(var + LN_EPS) * g + b


def _route(logits_t):
    ng, per = N_EXPERT_GROUPS, EXPERTS_PER_GROUP
    m = jnp.max(logits_t, axis=0, keepdims=True)
    e = jnp.exp(logits_t - m)
    prob = e / jnp.sum(e, axis=0, keepdims=True)
    v = [prob[j * ng:(j + 1) * ng, :] for j in range(per)]

    def top(vals):
        best = functools.reduce(jnp.maximum, vals)
        idx = jnp.full(best.shape, per - 1, jnp.int32)
        for j in range(per - 2, -1, -1):
            idx = jnp.where(vals[j] == best, j, idx)
        return best, idx

    m1, i1 = top(v)
    m2, i2 = top([jnp.where(i1 == j, -1.0, v[j]) for j in range(per)])
    score = m1 + m2
    gid = lax.broadcasted_iota(jnp.int32, score.shape, 0)
    best_g = jnp.min(jnp.where(score == jnp.max(score, axis=0, keepdims=True), gid, ng), axis=0, keepdims=True)
    sel = gid == best_g
    pick_f = lambda a: jnp.sum(jnp.where(sel, a, 0.0), axis=0, keepdims=True)
    pick_i = lambda a: jnp.sum(jnp.where(sel, a, 0), axis=0, keepdims=True)
    p1, p2 = pick_f(m1), pick_f(m2)
    e1, e2 = best_g * per + pick_i(i1), best_g * per + pick_i(i2)
    den = p1 + p2
    return jnp.concatenate([e1, e2], axis=0), jnp.concatenate([p1 / den, p2 / den], axis=0)


def _mix_kernel(x_ref, ya_ref, ys_ref, upc_ref, upp_ref, g1_ref, sc2_ref, sh2_ref, lng_ref, lnb_ref,
                wout_ref, wglu_ref, bglu_ref, wpool_ref, pscale_ref, wr_ref,
                x1_ref, u2_ref, ids_ref, gates_ref, *, alpha):
    i = pl.program_id(1)
    tm = x_ref.shape[1]
    aw = ya_ref.shape[2]
    sw = ys_ref.shape[2]
    y = ys_ref[0]
    y = 0.5 * y * (1.0 + jnp.tanh(math.sqrt(2.0 / math.pi) * (y + 0.044715 * (y * y * y))))
    z = jnp.dot(y.astype(BF16), wglu_ref[...], preferred_element_type=F32) + bglu_ref[...]
    y = y * _sigmoid(z)
    upc = upc_ref[0]
    upp = jnp.where(i > 0, upp_ref[0], 0.0)
    cat = jnp.concatenate([upp, upc], axis=0).astype(BF16)
    r = lax.broadcasted_iota(jnp.int32, (tm, 1), 0)
    lagm = (r + tm) - lax.broadcasted_iota(jnp.int32, (1, 2 * tm), 1)
    tpos = (i * tm + r + 1).astype(F32)
    gc = upc.shape[1] // len(POOL_WINDOWS)
    pooled = []
    for g, win in enumerate(POOL_WINDOWS):
        band = jnp.where((lagm >= 0) & (lagm < win), 1.0, 0.0).astype(BF16)
        ws = jnp.dot(band, cat[:, g * gc:(g + 1) * gc], preferred_element_type=F32)
        pg = ws / jnp.minimum(tpos, float(win)) - upc[:, g * gc:(g + 1) * gc]
        pooled.append(jnp.dot(pg.astype(BF16), wpool_ref[g], preferred_element_type=F32))
    yp = jnp.concatenate(pooled, axis=1) * pscale_ref[...]
    mixed = jnp.concatenate([ya_ref[0], y.astype(BF16), yp.astype(BF16)], axis=1)
    mix = jnp.dot(mixed, wout_ref[...], preferred_element_type=F32)
    x1 = _layer_norm(alpha * x_ref[0] + (1.0 + g1_ref[0]) * mix, lng_ref[...], lnb_ref[...])
    x1_ref[0] = x1
    u2 = x1 * (1.0 + sc2_ref[0]) + sh2_ref[0]
    u2_ref[0] = _to_rows(u2.astype(BF16))
    u_hi = u2.astype(BF16)
    u_lo = (u2 - u_hi.astype(F32)).astype(BF16)
    logits = jnp.dot(jnp.concatenate([u_hi, u_hi, u_lo], axis=1), wr_ref[...], preferred_element_type=F32)
    ids, gates = _route(logits.T[0:N_EXPERTS, :])
    ids_ref[0] = ids
    gates_ref[0] = gates


def _mix(x, ya, ys, up, g1, sc2, sh2, lng, lnb, wout, wglu, bglu, wpool, pscale, wr_pad, alpha):
    bsz, seq, d = x.shape
    tm = min(ROW_TILE, seq)
    tile = lambda n: pl.BlockSpec((1, tm, n), lambda b, i: (b, i, 0))
    prev = pl.BlockSpec((1, tm, up.shape[2]), lambda b, i: (b, jnp.maximum(i - 1, 0), 0))
    vec = pl.BlockSpec((1, 1, d), lambda b, i: (b, 0, 0))
    full = lambda a: pl.BlockSpec(a.shape, lambda b, i: (0,) * a.ndim)
    lane_rows = pl.BlockSpec((1, 2, tm), lambda b, i: (b, 0, i))
    return pl.pallas_call(
        functools.partial(_mix_kernel, alpha=alpha),
        grid=(bsz, seq // tm),
        in_specs=[tile(d), tile(ya.shape[2]), tile(ys.shape[2]), tile(up.shape[2]), prev, vec, vec, vec,
                  full(lng), full(lnb), full(wout), full(wglu), full(bglu), full(wpool), full(pscale), full(wr_pad)],
        out_specs=[tile(d), pl.BlockSpec((1, tm) + _row_tile(d), lambda b, i: (b, i, 0, 0)), lane_rows, lane_rows],
        out_shape=[jax.ShapeDtypeStruct((bsz, seq, d), F32), jax.ShapeDtypeStruct((bsz, seq) + _row_tile(d), BF16),
                   jax.ShapeDtypeStruct((bsz, 2, seq), jnp.int32), jax.ShapeDtypeStruct((bsz, 2, seq), F32)],
        compiler_params=_params("parallel", "arbitrary"),
        name="mix",
    )(x, ya, ys, up, up, g1, sc2, sh2, lng, lnb, wout, wglu, bglu, wpool, pscale, wr_pad)


def _gather_kernel(idx_ref, src_ref, *rest, lo, n_src, partial_src, has_prev):
    if has_prev:
        prev_ref, out_ref, buf, sem = rest
    else:
        out_ref, buf, sem = rest

    @pl.when(pl.program_id(0) == 0)
    def _():
        cp = pltpu.make_async_copy(src_ref.at[pl.ds(lo, n_src)], buf, sem)
        cp.start()
        cp.wait()

    base = pl.program_id(0) * COPY_ROWS

    def move(r, carry):
        s = idx_ref[base + r]
        row = buf[jnp.maximum(s, 0)] if partial_src else buf[s]
        if has_prev:
            row = jnp.where(s >= 0, row, prev_ref[r])
        out_ref[r] = row
        return carry

    lax.fori_loop(0, COPY_ROWS, move, 0, unroll=8)


def _gather_rows(src, idx):
    n = idx.shape[0]
    n_chunks = pl.cdiv(src.shape[0], GATHER_SRC_ROWS)
    n_src = src.shape[0] // n_chunks
    assert n_src * n_chunks == src.shape[0]
    block = pl.BlockSpec((COPY_ROWS,) + src.shape[1:], lambda i, idx: (i, 0, 0))
    out = None
    for c in range(n_chunks):
        has_prev = c > 0
        local = idx - c * n_src
        local = idx if n_chunks == 1 else jnp.where((local >= 0) & (local < n_src), local, -1)
        out = pl.pallas_call(
            functools.partial(_gather_kernel, lo=c * n_src, n_src=n_src, partial_src=n_chunks > 1,
                              has_prev=has_prev),
            grid_spec=pltpu.PrefetchScalarGridSpec(
                num_scalar_prefetch=1, grid=(n // COPY_ROWS,),
                in_specs=[pl.BlockSpec(memory_space=pl.ANY)] + ([block] if has_prev else []),
                out_specs=block,
                scratch_shapes=[pltpu.VMEM((n_src,) + src.shape[1:], src.dtype), pltpu.SemaphoreType.DMA(())]),
            out_shape=jax.ShapeDtypeStruct((n,) + src.shape[1:], src.dtype),
            compiler_params=_params("arbitrary"),
            name="gather",
        )(*((local, src, out) if has_prev else (local, src)))
    return out


def _expert_kernel(te_ref, act_ref, first_ref, nxt_ref, x_ref, wg_hbm, wu_hbm, wd_hbm, y_ref,
                   stage_g, stage_u, stage_d, wg_ref, wu_ref, wd_ref, sems, *, layer):
    i = pl.program_id(0)

    def fetch(e):
        return (pltpu.make_async_copy(wg_hbm.at[layer, e], stage_g, sems.at[0]),
                pltpu.make_async_copy(wu_hbm.at[layer, e], stage_u, sems.at[1]),
                pltpu.make_async_copy(wd_hbm.at[layer, e], stage_d, sems.at[2]))

    @pl.when(i == 0)
    def _():
        for cp in fetch(te_ref[0]):
            cp.start()

    @pl.when(first_ref[i] != 0)
    def _():
        for cp in fetch(te_ref[i]):
            cp.wait()
        for stage, work in ((stage_g, wg_ref), (stage_u, wu_ref), (stage_d, wd_ref)):
            rows = stage.shape[0]

            def convert(r, carry, stage=stage, work=work):
                sl = pl.ds(pl.multiple_of(r * CONVERT_ROWS, CONVERT_ROWS), CONVERT_ROWS)
                work[sl, :] = stage[sl, :].astype(BF16)
                return carry

            lax.fori_loop(0, rows // CONVERT_ROWS, convert, 0)

        @pl.when(nxt_ref[i] >= 0)
        def _():
            for cp in fetch(nxt_ref[i]):
                cp.start()

    @pl.when(act_ref[i] != 0)
    def _():
        x = _from_rows(x_ref[...])
        g = jnp.dot(x, wg_ref[...], preferred_element_type=F32)
        u = jnp.dot(x, wu_ref[...], preferred_element_type=F32)
        h = (g * _sigmoid(g)) * u
        y_ref[...] = _to_rows(jnp.dot(h.astype(BF16), wd_ref[...], preferred_element_type=F32).astype(BF16))

    @pl.when(act_ref[i] == 0)
    def _():
        y_ref[...] = jnp.zeros(y_ref.shape, y_ref.dtype)


def _experts(xg, plan, wg, wu, wd, layer):
    rows = xg.shape[0]
    d, f = wg.shape[2], wg.shape[3]
    tm = EXPERT_TILE
    tile = pl.BlockSpec((tm,) + _row_tile(d), lambda i, *_: (i, 0, 0))
    hbm = pl.BlockSpec(memory_space=pl.ANY)
    return pl.pallas_call(
        functools.partial(_expert_kernel, layer=layer),
        grid_spec=pltpu.PrefetchScalarGridSpec(
            num_scalar_prefetch=4, grid=(rows // tm,),
            in_specs=[tile, hbm, hbm, hbm],
            out_specs=tile,
            scratch_shapes=[pltpu.VMEM((d, f), F32), pltpu.VMEM((d, f), F32), pltpu.VMEM((f, d), F32),
                            pltpu.VMEM((d, f), BF16), pltpu.VMEM((d, f), BF16), pltpu.VMEM((f, d), BF16),
                            pltpu.SemaphoreType.DMA((3,))]),
        out_shape=jax.ShapeDtypeStruct((rows,) + _row_tile(d), BF16),
        compiler_params=_params("arbitrary"),
        name="experts",
    )(*plan, xg, wg, wu, wd)


def _combine_kernel(x_ref, z_ref, gc_ref, g2_ref, lng_ref, lnb_ref, o_ref, *, alpha):
    gc = gc_ref[0]
    ffn = (gc[:, 0:1] * _from_rows(z_ref[0, 0]).astype(F32)
           + gc[:, 1:2] * _from_rows(z_ref[1, 0]).astype(F32))
    o_ref[0] = _layer_norm(alpha * x_ref[0] + (1.0 + g2_ref[0]) * ffn, lng_ref[...], lnb_ref[...])


def _combine(x1, z, gates_col, g2, lng, lnb, alpha):
    bsz, seq, d = x1.shape
    tm = min(ROW_TILE, seq)
    tile = pl.BlockSpec((1, tm, d), lambda b, i: (b, i, 0))
    return pl.pallas_call(
        functools.partial(_combine_kernel, alpha=alpha),
        grid=(bsz, seq // tm),
        in_specs=[tile, pl.BlockSpec((2, 1, tm) + _row_tile(d), lambda b, i: (0, b, i, 0, 0)),
                  pl.BlockSpec((1, tm, 2), lambda b, i: (b, i, 0)),
                  pl.BlockSpec((1, 1, d), lambda b, i: (b, 0, 0)),
                  pl.BlockSpec(lng.shape, lambda b, i: (0, 0)), pl.BlockSpec(lnb.shape, lambda b, i: (0, 0))],
        out_specs=tile,
        out_shape=jax.ShapeDtypeStruct((bsz, seq, d), F32),
        compiler_params=_params("parallel", "arbitrary"),
        name="combine",
    )(x1, z, gates_col, g2, lng, lnb)


def _dispatch_plan(ids, n_tok):
    tm = EXPERT_TILE
    e_pair = jnp.concatenate([ids[:, 0, :].reshape(n_tok), ids[:, 1, :].reshape(n_tok)])
    tok_pair = jnp.concatenate([jnp.arange(n_tok, dtype=jnp.int32)] * 2)
    onehot = (e_pair[:, None] == jnp.arange(N_EXPERTS, dtype=jnp.int32)[None, :]).astype(jnp.int32)
    csum = jnp.cumsum(onehot, axis=0)
    rank = jnp.sum(onehot * csum, axis=1) - 1
    counts = csum[-1]
    padded = ((counts + tm - 1) // tm) * tm
    ends = jnp.cumsum(padded)
    offs = ends - padded
    pos = jnp.sum(onehot * offs[None, :], axis=1) + rank
    rows = 2 * n_tok + N_EXPERTS * tm
    src_token = jnp.zeros((rows,), jnp.int32).at[pos].set(tok_pair)
    tile_start = jnp.arange(rows // tm, dtype=jnp.int32) * tm
    tile_expert = jnp.minimum(jnp.sum((tile_start[:, None] >= ends[None, :]).astype(jnp.int32), axis=1),
                              N_EXPERTS - 1)
    tile_active = (tile_start < ends[-1]).astype(jnp.int32)
    prev_expert = jnp.concatenate([jnp.full((1,), -1, jnp.int32), tile_expert[:-1]])
    tile_first = tile_active * (tile_expert != prev_expert).astype(jnp.int32)
    eid = jnp.arange(N_EXPERTS, dtype=jnp.int32)
    later = (padded > 0)[None, :] & (eid[None, :] > eid[:, None])
    next_expert = jnp.min(jnp.where(later, eid[None, :], N_EXPERTS), axis=1)
    next_expert = jnp.where(next_expert == N_EXPERTS, -1, next_expert).astype(jnp.int32)
    tile_next = next_expert[tile_expert]
    return pos.astype(jnp.int32), src_token, (tile_expert, tile_active, tile_first, tile_next)


def _rope_tables(positions, dim):
    rot = dim // ROPE_FRAC
    half = rot // 2
    inv_freq = ROPE_THETA ** (-jnp.arange(half, dtype=F32) * 2.0 / rot)
    ang = positions.astype(F32)[..., None] * inv_freq
    cos, sin = jnp.cos(ang), jnp.sin(ang)
    shape = ang.shape[:-1]
    one = jnp.ones(shape + (dim - rot,), F32)
    zero = jnp.zeros(shape + (dim - rot,), F32)
    zh = jnp.zeros(shape + (half,), F32)
    c = jnp.concatenate([cos, cos, one], axis=-1)
    s1 = jnp.concatenate([-sin, zh, zero], axis=-1)
    s2 = jnp.concatenate([zh, sin, zero], axis=-1)
    rep = LANES // dim
    return jnp.concatenate([jnp.tile(a, (1, 1, rep)) for a in (c, s1, s2)], axis=-1)


def kernel(x, c, positions, w_ada, b_ada, w_in, w_out, ssm_lam_re, ssm_lam_im, ssm_log_step, ssm_b_re, ssm_b_im, ssm_c_re, ssm_c_im, ssm_d, ssm_w_glu, ssm_b_glu, pool_w, pool_scale, ln1_g, ln1_b, ln2_g, ln2_b, w_router, e_gate, e_up, e_down):
    bsz, seq, d = x.shape
    depth = w_ada.shape[0]
    n_tok = bsz * seq
    alpha = (2.0 * depth) ** 0.25
    sw = ssm_d.shape[1]
    pw = pool_scale.shape[1]
    aw = w_out.shape[1] - sw - pw
    kw = KV_HEADS * HEAD_DIM
    iqw = IDX_HEADS * IDX_DIM
    widths = (aw, kw, iqw, sw, pw)
    top_k = min(MAX_TOPK, seq // 4)

    c_pad = jnp.concatenate([c, jnp.zeros((8 - bsz % 8, d), F32)], axis=0) if bsz % 8 else c
    ada = _ada_all(c_pad, w_ada, b_ada)
    rq = _rope_tables(positions, HEAD_DIM)
    ri = _rope_tables(positions, IDX_DIM)
    wr_pad = jnp.concatenate(
        [w_router.reshape(d, N_EXPERT_GROUPS, EXPERTS_PER_GROUP).transpose(0, 2, 1).reshape(d, N_EXPERTS),
         jnp.zeros((d, LANES - N_EXPERTS), F32)], axis=1)
    wr_hi = wr_pad.astype(BF16)
    wr_lo = (wr_pad - wr_hi.astype(F32)).astype(BF16)
    wr_pad = jnp.concatenate([wr_hi, wr_lo, wr_hi], axis=0)

    o_q, o_k, o_v, o_iq, o_ik, o_iw, o_us, o_up = (0, aw, aw + kw, aw + 2 * kw, aw + 2 * kw + iqw,
                                                   aw + 2 * kw + iqw + IDX_DIM,
                                                   aw + 2 * kw + iqw + IDX_DIM + IDX_HEADS,
                                                   aw + 2 * kw + iqw + IDX_DIM + IDX_HEADS + sw)
    for l in range(depth):
        sh1, sc1, g1, sh2, sc2, g2 = [ada[l, :bsz, j * d:(j + 1) * d].reshape(bsz, 1, d) for j in range(6)]
        wl = w_in[l]
        w_r = jnp.concatenate([wl[:, o_q:o_v], wl[:, o_iq:o_ik], wl[:, o_us:], wl[:, o_ik:o_iw],
                               jnp.zeros((d, LANES - IDX_DIM), F32)], axis=1).astype(BF16)
        wvt = wl[:, o_v:o_iq].T.astype(BF16)
        wiwt = wl[:, o_iw:o_us].T.astype(BF16)
        q, k, vt, iq, ik, iwt, us, up = _inproj(x, sc1, sh1, w_r, wvt, wiwt, rq, ri, widths)
        ya = _attention(q, k, vt, iq, ik, iwt, top_k)
        prep = _ssm_prepare(ssm_lam_re[l], ssm_lam_im[l], ssm_log_step[l], ssm_b_re[l], ssm_b_im[l],
                            ssm_c_re[l], ssm_c_im[l], ssm_d[l], SSM_CHUNK)
        ys = _ssm(us, prep, SSM_CHUNK)
        x1, u2, ids, gates = _mix(x, ya, ys, up, g1, sc2, sh2, ln1_g[l][None], ln1_b[l][None],
                                  w_out[l].astype(BF16), ssm_w_glu[l].astype(BF16), ssm_b_glu[l][None],
                                  pool_w[l].astype(BF16), pool_scale[l][None], wr_pad, alpha)
        pos, src_token, tile_plan = _dispatch_plan(ids, n_tok)
        xg = _gather_rows(u2.reshape((n_tok,) + _row_tile(d)), src_token)
        yg = _experts(xg, tile_plan, e_gate, e_up, e_down, l)
        z = _gather_rows(yg, pos).reshape((2, bsz, seq) + _row_tile(d))
        x = _combine(x1, z, gates.transpose(0, 2, 1), g2, ln2_g[l][None], ln2_b[l][None], alpha)
    return x
```

```python
import functools
import math

import jax
import jax.numpy as jnp
import numpy as np
from jax import lax
from jax.experimental import pallas as pl
from jax.experimental.pallas import tpu as pltpu

F32 = jnp.float32
BF16 = jnp.bfloat16

HEAD_DIM = 128
KV_HEADS = 2
IDX_HEADS = 8
IDX_DIM = 64
MAX_TOPK = 256
SSM_GROUP_CH = 16
SSM_STATE = 64
POOL_WINDOWS = (2, 4, 8, 16)
ROPE_THETA = 500000.0
ROPE_FRAC = 4
LN_EPS = 1e-5
N_EXPERTS = 16
N_EXPERT_GROUPS = 4
EXPERTS_PER_GROUP = N_EXPERTS // N_EXPERT_GROUPS

LANES = 128
VMEM_LIMIT = 56 * 1024 * 1024
ROW_TILE = 256
ATTN_Q_TILE = 256
ATTN_KEY_STEP = 512
BISECT_STEPS = 14
REDUCE_SLAB = 64
ROW_SUBLANES = 16
GATHER_SRC_ROWS = 10240
SSM_CHUNK = 8
EXPERT_TILE = 256
CONVERT_ROWS = 128
COPY_ROWS = 256
ADA_COLS = 1024


def _params(*sem):
    return pltpu.CompilerParams(dimension_semantics=sem, vmem_limit_bytes=VMEM_LIMIT)


def _sigmoid(x):
    return 1.0 / (1.0 + jnp.exp(-x))


def _ada_kernel(c_ref, w_ref, b_ref, o_ref):
    c = c_ref[...]
    cond = (c * _sigmoid(c)).astype(BF16)
    o_ref[0] = jnp.dot(cond, w_ref[0].astype(BF16), preferred_element_type=F32) + b_ref[0]


def _ada_all(c_pad, w_ada, b_ada):
    depth, d, n6 = w_ada.shape
    rows = c_pad.shape[0]
    return pl.pallas_call(
        _ada_kernel,
        grid=(depth, n6 // ADA_COLS),
        in_specs=[pl.BlockSpec((rows, d), lambda l, j: (0, 0)),
                  pl.BlockSpec((1, d, ADA_COLS), lambda l, j: (l, 0, j)),
                  pl.BlockSpec((1, 1, ADA_COLS), lambda l, j: (l, 0, j))],
        out_specs=pl.BlockSpec((1, rows, ADA_COLS), lambda l, j: (l, 0, j)),
        out_shape=jax.ShapeDtypeStruct((depth, rows, n6), F32),
        compiler_params=_params("arbitrary", "arbitrary"),
        name="ada",
    )(c_pad, w_ada, b_ada.reshape(depth, 1, n6))


def _rope(xv, tab, half):
    w = xv.shape[1]
    rep = w // LANES
    c = jnp.tile(tab[:, 0:LANES], (1, rep))
    s1 = jnp.tile(tab[:, LANES:2 * LANES], (1, rep))
    s2 = jnp.tile(tab[:, 2 * LANES:3 * LANES], (1, rep))
    return xv * c + pltpu.roll(xv, w - half, 1) * s1 + pltpu.roll(xv, half, 1) * s2


def _inproj_kernel(x_ref, sc_ref, sh_ref, w_ref, wvt_ref, wiwt_ref, rq_ref, ri_ref,
                   q_ref, k_ref, vt_ref, iq_ref, ik_ref, iwt_ref, us_ref, up_ref, *, widths, idx_scale):
    aw, kw, iqw, sw, pw = widths
    u = (x_ref[0] * (1.0 + sc_ref[0]) + sh_ref[0]).astype(BF16)
    rq = rq_ref[0]
    ri = ri_ref[0]

    def mm(lo, n):
        return jnp.dot(u, w_ref[:, lo:lo + n], preferred_element_type=F32)

    def mm_t(wt_ref):
        return lax.dot_general(wt_ref[...], u, (((1,), (1,)), ((), ())), preferred_element_type=F32)

    qhalf = HEAD_DIM // ROPE_FRAC // 2
    ihalf = IDX_DIM // ROPE_FRAC // 2
    o = 0
    q_ref[0] = (_rope(mm(o, aw), rq, qhalf) * (HEAD_DIM ** -0.5)).astype(BF16)
    o += aw
    k_ref[0] = _rope(mm(o, kw), rq, qhalf).astype(BF16)
    o += kw
    iq_ref[0] = _rope(mm(o, iqw), ri, ihalf).astype(BF16)
    o += iqw
    us_ref[0] = mm(o, sw)
    o += sw
    up_ref[0] = mm(o, pw)
    o += pw
    ik_ref[0] = _rope(mm(o, LANES), ri, ihalf)[:, 0:IDX_DIM].astype(BF16)
    vt_ref[0] = mm_t(wvt_ref).astype(BF16)
    iwt_ref[0] = mm_t(wiwt_ref) * idx_scale


def _inproj(x, sc, sh, w_r, wvt, wiwt, rq, ri, widths):
    bsz, seq, d = x.shape
    aw, kw, iqw, sw, pw = widths
    tm = min(ROW_TILE, seq)
    tile = lambda n: pl.BlockSpec((1, tm, n), lambda b, i: (b, i, 0))
    tile_t = lambda n: pl.BlockSpec((1, n, tm), lambda b, i: (b, 0, i))
    vec = pl.BlockSpec((1, 1, d), lambda b, i: (b, 0, 0))
    full = lambda a: pl.BlockSpec(a.shape, lambda b, i: (0, 0))
    sds = jax.ShapeDtypeStruct
    return pl.pallas_call(
        functools.partial(_inproj_kernel, widths=widths, idx_scale=(IDX_DIM ** -0.5) * (IDX_HEADS ** -0.5)),
        grid=(bsz, seq // tm),
        in_specs=[tile(d), vec, vec, full(w_r), full(wvt), full(wiwt), tile(3 * LANES), tile(3 * LANES)],
        out_specs=[tile(aw), tile(kw), tile_t(kw), tile(iqw), tile(IDX_DIM), tile_t(IDX_HEADS), tile(sw), tile(pw)],
        out_shape=[sds((bsz, seq, aw), BF16), sds((bsz, seq, kw), BF16), sds((bsz, kw, seq), BF16),
                   sds((bsz, seq, iqw), BF16), sds((bsz, seq, IDX_DIM), BF16), sds((bsz, IDX_HEADS, seq), F32),
                   sds((bsz, seq, sw), F32), sds((bsz, seq, pw), F32)],
        compiler_params=_params("parallel", "arbitrary"),
        name="inproj",
    )(x, sc, sh, w_r, wvt, wiwt, rq, ri)


def _col_reduce(op, v):
    rows, n = v.shape
    slab = op(v.reshape(rows // REDUCE_SLAB, REDUCE_SLAB, n), axis=0)
    return op(slab, axis=0, keepdims=True)


def _attn_kernel(q_ref, iq_ref, iwt_ref, k_ref, vt_ref, ik_ref, o_ref, mask_ref, x_ref, *, q0, tq, keys, top_k):
    i = pl.program_id(1)
    t = q0 + i * tq + lax.broadcasted_iota(jnp.int32, (1, tq), 1)
    kpos = lax.broadcasted_iota(jnp.int32, (keys, 1), 0)
    causal = kpos <= t

    iq = iq_ref[0]
    ik = ik_ref[0]
    iwt = iwt_ref[0]
    isc = jnp.zeros((keys, tq), F32)
    for h in range(IDX_HEADS):
        r = lax.dot_general(ik, iq[:, h * IDX_DIM:(h + 1) * IDX_DIM], (((1,), (1,)), ((), ())),
                            preferred_element_type=F32)
        isc = isc + jnp.maximum(r, 0.0) * iwt[h:h + 1, :]
    x = jnp.where(causal, isc, -jnp.inf)
    x_ref[...] = x

    kf = float(top_k)
    n_valid = (t + 1).astype(F32)
    need = n_valid > kf

    def fold(step_fn, init):
        def body(s, acc):
            row = pl.multiple_of(s * REDUCE_SLAB, REDUCE_SLAB)
            return step_fn(acc, x_ref[pl.ds(row, REDUCE_SLAB), :])
        acc0 = jax.tree.map(lambda v: jnp.full((REDUCE_SLAB, tq), v, F32), init)
        return lax.fori_loop(0, keys // REDUCE_SLAB, body, acc0, unroll=4)

    def count_ge(theta):
        acc = fold(lambda a, xs: a + jnp.where(xs >= theta, 1.0, 0.0), 0.0)
        return jnp.sum(acc, axis=0, keepdims=True)

    rmax = _col_reduce(jnp.max, x)
    rmin = _col_reduce(jnp.min, jnp.where(causal, isc, jnp.inf))
    c_max = count_ge(rmax)
    top_tie = c_max >= kf
    lo0 = jnp.where(need, jnp.where(top_tie, rmax, rmin), -jnp.inf)
    clo0 = jnp.where(top_tie, c_max, n_valid)
    done0 = jnp.where(need & jnp.logical_not(top_tie) & (clo0 != kf), 0.0, 1.0)

    def bisect(lo, hi, clo, chi, done):
        mid = 0.5 * lo + 0.5 * hi
        c = count_ge(mid)
        live = done == 0.0
        ge = c >= kf
        up = live & ge
        dn = live & jnp.logical_not(ge)
        lo = jnp.where(up, mid, lo)
        clo = jnp.where(up, c, clo)
        hi = jnp.where(dn, mid, hi)
        chi = jnp.where(dn, c, chi)
        done = jnp.where(clo == kf, 1.0, done)
        return lo, hi, clo, chi, done

    def n_active(done):
        return jnp.sum(1.0 - done).astype(jnp.int32)

    def body2(s):
        _, lo, hi, clo, chi, done = s
        lo, hi, clo, chi, done = bisect(lo, hi, clo, chi, done)

        def min_max(acc, xs):
            inside = (xs >= lo) & (xs < hi)
            return (jnp.minimum(acc[0], jnp.where(inside, xs, jnp.inf)),
                    jnp.maximum(acc[1], jnp.where(inside, xs, -jnp.inf)))

        mins, maxs = fold(min_max, (jnp.inf, -jnp.inf))
        vmin = jnp.min(mins, axis=0, keepdims=True)
        vmax = jnp.max(maxs, axis=0, keepdims=True)
        c2 = count_ge(vmax)
        live = done == 0.0
        single = vmin == vmax
        top_ok = c2 >= kf
        take_top = live & jnp.logical_not(single) & top_ok
        drop_top = live & jnp.logical_not(single) & jnp.logical_not(top_ok)
        lo = jnp.where(live, jnp.where(take_top, vmax, vmin), lo)
        clo = jnp.where(take_top, c2, clo)
        hi = jnp.where(drop_top, vmax, hi)
        chi = jnp.where(drop_top, c2, chi)
        done = jnp.where(live & (single | top_ok), 1.0, done)
        return n_active(done), lo, hi, clo, chi, done

    state = lax.fori_loop(0, BISECT_STEPS, lambda _, s: bisect(*s), (lo0, rmax, clo0, c_max, done0))
    state = lax.while_loop(lambda s: s[0] > 0, body2, (n_active(state[4]),) + state)
    _, lo, _, clo, _, _ = state

    mask_ref[...] = jnp.where(causal & (x_ref[...] >= lo), 0.0, -1e30)

    tie_q = jnp.sum(jnp.where(need & (clo > kf), 1.0, 0.0)).astype(jnp.int32)

    @pl.when(tie_q > 0)
    def _():
        room = kf - _col_reduce(jnp.sum, jnp.where(x_ref[...] > lo, 1.0, 0.0))
        step = 256
        rr = lax.broadcasted_iota(jnp.int32, (step, step), 0)
        cc = lax.broadcasted_iota(jnp.int32, (step, step), 1)
        before = jnp.where(cc < rr, 1.0, 0.0).astype(BF16)
        carry = jnp.zeros((1, tq), F32)
        for c0 in range(0, keys, step):
            xs = x_ref[c0:c0 + step, :]
            e = jnp.where(xs == lo, 1.0, 0.0)
            rank = carry + jnp.dot(before, e.astype(BF16), preferred_element_type=F32)
            keep = (xs > lo) | ((xs == lo) & (rank < room))
            mask_ref[c0:c0 + step, :] = jnp.where(keep & (xs > -jnp.inf), 0.0, -1e30)
            carry = carry + jnp.sum(e, axis=0, keepdims=True)

    bias = mask_ref[...]
    q = q_ref[0]
    group = q.shape[1] // HEAD_DIM // KV_HEADS
    bias_g = jnp.concatenate([bias] * group, axis=1)
    for g in range(KV_HEADS):
        qg = jnp.concatenate([q[:, (g * group + j) * HEAD_DIM:(g * group + j + 1) * HEAD_DIM]
                              for j in range(group)], axis=0)
        kg = k_ref[0, :, g * HEAD_DIM:(g + 1) * HEAD_DIM]
        vtg = vt_ref[0, g * HEAD_DIM:(g + 1) * HEAD_DIM, :]
        s = lax.dot_general(kg, qg, (((1,), (1,)), ((), ())), preferred_element_type=F32) + bias_g
        m = _col_reduce(jnp.max, s)
        p = jnp.exp(s - m)
        den = _col_reduce(jnp.sum, p)
        og = jnp.dot(vtg, p.astype(BF16), preferred_element_type=F32) / den
        for j in range(group):
            h = g * group + j
            o_ref[0, :, h * HEAD_DIM:(h + 1) * HEAD_DIM] = og[:, j * tq:(j + 1) * tq].T.astype(BF16)


def _attention(q, k, vt, iq, ik, iwt, top_k):
    bsz, seq, aw = q.shape
    tq = min(ATTN_Q_TILE, seq)
    step = min(ATTN_KEY_STEP, seq)
    outs = []
    for q0 in range(0, seq, step):
        keys = q0 + step
        nq = step // tq
        qtile = lambda n, q0=q0: pl.BlockSpec((1, tq, n), lambda b, i: (b, q0 // tq + i, 0))
        ktile = lambda n, keys=keys: pl.BlockSpec((1, keys, n), lambda b, i: (b, 0, 0))
        outs.append(pl.pallas_call(
            functools.partial(_attn_kernel, q0=q0, tq=tq, keys=keys, top_k=top_k),
            grid=(bsz, nq),
            in_specs=[qtile(aw), qtile(iq.shape[2]),
                      pl.BlockSpec((1, iwt.shape[1], tq), lambda b, i, q0=q0: (b, 0, q0 // tq + i)),
                      ktile(k.shape[2]),
                      pl.BlockSpec((1, vt.shape[1], keys), lambda b, i: (b, 0, 0)),
                      ktile(ik.shape[2])],
            out_specs=pl.BlockSpec((1, tq, aw), lambda b, i: (b, i, 0)),
            out_shape=jax.ShapeDtypeStruct((bsz, step, aw), BF16),
            scratch_shapes=[pltpu.VMEM((keys, tq), F32), pltpu.VMEM((keys, tq), F32)],
            compiler_params=_params("parallel", "arbitrary"),
            name=f"attn_k{keys}",
        )(q, iq, iwt, k, vt, ik))
    return outs


def _ssm_kernel(u_ref, kin_ref, wre_ref, wim_ref, vre_ref, vim_ref, are_ref, aim_ref, y_ref,
                sre_ref, sim_ref, xre_ref, xim_ref, *, chunk):
    bsz, seq, lanes = u_ref.shape
    nc = seq // chunk
    rows = bsz * nc
    ns = wre_ref.shape[2]
    u = u_ref[...].reshape(rows, chunk, lanes).reshape(rows, chunk * lanes).astype(BF16)
    sre_ref[...] = jnp.dot(u, wre_ref[0], preferred_element_type=F32).reshape(bsz, nc, ns)
    sim_ref[...] = jnp.dot(u, wim_ref[0], preferred_element_type=F32).reshape(bsz, nc, ns)
    a_re = are_ref[0]
    a_im = aim_ref[0]
    x_re = jnp.zeros((bsz, ns), F32)
    x_im = jnp.zeros((bsz, ns), F32)
    for c in range(nc):
        xre_ref[:, c, :] = x_re
        xim_ref[:, c, :] = x_im
        n_re = a_re * x_re - a_im * x_im + sre_ref[:, c, :]
        n_im = a_re * x_im + a_im * x_re + sim_ref[:, c, :]
        x_re, x_im = n_re, n_im
    xr = xre_ref[...].reshape(rows, ns).astype(BF16)
    xi = xim_ref[...].reshape(rows, ns).astype(BF16)
    y = (jnp.dot(u, kin_ref[0], preferred_element_type=F32)
         + jnp.dot(xr, vre_ref[0], preferred_element_type=F32)
         + jnp.dot(xi, vim_ref[0], preferred_element_type=F32))
    y_ref[...] = y.reshape(rows, chunk, lanes).reshape(bsz, seq, lanes)


def _ssm_placement(gs, ch, p):
    a = np.arange(gs)[:, None, None]
    lane = np.arange(gs * ch)[None, None, :]
    on_lane = ((lane // ch) == a) & ((lane % ch) == np.arange(ch)[None, :, None])
    q = np.arange(gs * p)[None, None, :]
    on_state = ((q // p) == a) & ((q % p) == np.arange(p)[None, :, None])
    return jnp.asarray(on_lane, F32), jnp.asarray(on_state, F32)


def _ssm_prepare(lam_re, lam_im, log_step, b_re, b_im, c_re, c_im, d_skip, chunk):
    g, p = lam_re.shape
    ch = b_re.shape[2]
    hp = lax.Precision.HIGHEST
    step = jnp.exp(log_step)[:, None]
    lsr, lsi = lam_re * step, lam_im * step
    er = jnp.exp(lsr)
    nr, ni = er * jnp.cos(lsi) - 1.0, er * jnp.sin(lsi)
    den = lam_re * lam_re + lam_im * lam_im
    fr, fi = (nr * lam_re + ni * lam_im) / den, (ni * lam_re - nr * lam_im) / den
    bbr = fr[..., None] * b_re - fi[..., None] * b_im
    bbi = fr[..., None] * b_im + fi[..., None] * b_re
    n = jnp.arange(chunk + 1, dtype=F32)
    mag = jnp.exp(lsr[..., None] * n)
    pr, pi = mag * jnp.cos(lsi[..., None] * n), mag * jnp.sin(lsi[..., None] * n)
    prt, pit = pr[:, :, :chunk, None], pi[:, :, :chunk, None]
    wr = prt * bbr[:, :, None, :] - pit * bbi[:, :, None, :]
    wi = prt * bbi[:, :, None, :] + pit * bbr[:, :, None, :]
    ker = (jnp.einsum("gcp,gptd->gtcd", c_re, wr, precision=hp)
           - jnp.einsum("gcp,gptd->gtcd", c_im, wi, precision=hp))
    ker = ker.at[:, 0].add(jnp.eye(ch, dtype=F32) * d_skip.reshape(g, 1, ch))
    pr1, pi1 = pr[:, None, :, 1:], pi[:, None, :, 1:]
    vr = c_re[..., None] * pr1 - c_im[..., None] * pi1
    vi = c_re[..., None] * pi1 + c_im[..., None] * pr1
    gs = LANES // ch
    ns = g // gs
    lanes = chunk * LANES
    on_lane, on_state = _ssm_placement(gs, ch, p)
    bf16_exact = lambda m: m.astype(BF16).astype(F32)
    by_tile = lambda m: m.reshape(ns, gs, *m.shape[1:])
    lag_tile = jnp.einsum("satcd,acl->stadl", by_tile(bf16_exact(ker)), on_lane).reshape(ns, chunk, LANES, LANES)
    lag = jnp.arange(chunk)[None, :] - jnp.arange(chunk)[:, None]
    tiles = jnp.where((lag >= 0)[None, :, :, None, None], lag_tile[:, jnp.clip(lag, 0, chunk - 1)], 0.0)
    kin8 = tiles.transpose(0, 1, 3, 2, 4).reshape(ns, lanes, lanes).astype(BF16)

    def to_in(w):
        wide = jnp.einsum("sapid,apq->siadq", by_tile(bf16_exact(w[:, :, ::-1, :])), on_state)
        return wide.reshape(ns, lanes, gs * p).astype(BF16)

    def to_out(v):
        wide = jnp.einsum("sacpj,acl->sapjl", by_tile(bf16_exact(v)), on_lane)
        return wide.reshape(ns, gs * p, lanes).astype(BF16)

    a_re = pr[..., chunk].reshape(ns, 1, gs * p)
    a_im = pi[..., chunk].reshape(ns, 1, gs * p)
    return kin8, to_in(wr), to_in(wi), to_out(vr), to_out(-vi), a_re, a_im


def _ssm(us, prep, chunk):
    bsz, seq, w = us.shape
    kin, wre, wim, vre, vim, a_re, a_im = prep
    ns, lanes, states = wre.shape
    nc = seq // chunk
    act = pl.BlockSpec((bsz, seq, LANES), lambda i: (0, 0, i))
    mat = lambda a, b: pl.BlockSpec((1, a, b), lambda i: (i, 0, 0))
    return pl.pallas_call(
        functools.partial(_ssm_kernel, chunk=chunk),
        grid=(ns,),
        in_specs=[act, mat(lanes, lanes), mat(lanes, states), mat(lanes, states), mat(states, lanes),
                  mat(states, lanes), mat(1, states), mat(1, states)],
        out_specs=act,
        out_shape=jax.ShapeDtypeStruct((bsz, seq, w), F32),
        scratch_shapes=[pltpu.VMEM((bsz, nc, states), F32)] * 4,
        compiler_params=_params("parallel"),
        name="ssm",
    )(us, kin, wre, wim, vre, vim, a_re, a_im)


def _row_tile(d):
    return (ROW_SUBLANES, d // ROW_SUBLANES)


def _to_rows(v):
    return v.reshape((v.shape[0],) + _row_tile(v.shape[1]))


def _from_rows(v):
    return v.reshape(v.shape[0], v.shape[1] * v.shape[2])


def _layer_norm(h, g, b):
    mu = jnp.mean(h, axis=1, keepdims=True)
    hc = h - mu
    var = jnp.mean(hc * hc, axis=1, keepdims=True)
    return hc * lax.rsqrt(var + LN_EPS) * g + b


def _route(logits_t):
    ng, per = N_EXPERT_GROUPS, EXPERTS_PER_GROUP
    m = jnp.max(logits_t, axis=0, keepdims=True)
    e = jnp.exp(logits_t - m)
    prob = e / jnp.sum(e, axis=0, keepdims=True)
    v = [prob[j * ng:(j + 1) * ng, :] for j in range(per)]

    def top(vals):
        best = functools.reduce(jnp.maximum, vals)
        idx = jnp.full(best.shape, per - 1, jnp.int32)
        for j in range(per - 2, -1, -1):
            idx = jnp.where(vals[j] == best, j, idx)
        return best, idx

    m1, i1 = top(v)
    m2, i2 = top([jnp.where(i1 == j, -1.0, v[j]) for j in range(per)])
    score = m1 + m2
    gid = lax.broadcasted_iota(jnp.int32, score.shape, 0)
    best_g = jnp.min(jnp.where(score == jnp.max(score, axis=0, keepdims=True), gid, ng), axis=0, keepdims=True)
    sel = gid == best_g
    pick_f = lambda a: jnp.sum(jnp.where(sel, a, 0.0), axis=0, keepdims=True)
    pick_i = lambda a: jnp.sum(jnp.where(sel, a, 0), axis=0, keepdims=True)
    p1, p2 = pick_f(m1), pick_f(m2)
    e1, e2 = best_g * per + pick_i(i1), best_g * per + pick_i(i2)
    den = p1 + p2
    return jnp.concatenate([e1, e2], axis=0), jnp.concatenate([p1 / den, p2 / den], axis=0)


def _mix_kernel(x_ref, ys_ref, upc_ref, upp_ref, g1_ref, sc2_ref, sh2_ref, lng_ref, lnb_ref,
                wout_ref, wglu_ref, bglu_ref, wpool_ref, pscale_ref, wr_ref, *rest, alpha, tiles_per_variant):
    ya_refs, (x1_ref, u2_ref, ids_ref, gates_ref) = rest[:-4], rest[-4:]
    i = pl.program_id(1)
    tm = x_ref.shape[1]
    owner = i // tiles_per_variant
    ya = ya_refs[0][0]
    for k in range(1, len(ya_refs)):
        ya = jnp.where(owner == k, ya_refs[k][0], ya)
    y = ys_ref[0]
    y = 0.5 * y * (1.0 + jnp.tanh(math.sqrt(2.0 / math.pi) * (y + 0.044715 * (y * y * y))))
    z = jnp.dot(y.astype(BF16), wglu_ref[...], preferred_element_type=F32) + bglu_ref[...]
    y = y * _sigmoid(z)
    upc = upc_ref[0]
    upp = jnp.where(i > 0, upp_ref[0], 0.0)
    cat = jnp.concatenate([upp, upc], axis=0).astype(BF16)
    r = lax.broadcasted_iota(jnp.int32, (tm, 1), 0)
    lagm = (r + tm) - lax.broadcasted_iota(jnp.int32, (1, 2 * tm), 1)
    tpos = (i * tm + r + 1).astype(F32)
    gc = upc.shape[1] // len(POOL_WINDOWS)
    pooled = []
    for g, win in enumerate(POOL_WINDOWS):
        band = jnp.where((lagm >= 0) & (lagm < win), 1.0, 0.0).astype(BF16)
        ws = jnp.dot(band, cat[:, g * gc:(g + 1) * gc], preferred_element_type=F32)
        pg = ws / jnp.minimum(tpos, float(win)) - upc[:, g * gc:(g + 1) * gc]
        pooled.append(jnp.dot(pg.astype(BF16), wpool_ref[g], preferred_element_type=F32))
    yp = jnp.concatenate(pooled, axis=1) * pscale_ref[...]
    mixed = jnp.concatenate([ya, y.astype(BF16), yp.astype(BF16)], axis=1)
    mix = jnp.dot(mixed, wout_ref[...], preferred_element_type=F32)
    x1 = _layer_norm(alpha * x_ref[0] + (1.0 + g1_ref[0]) * mix, lng_ref[...], lnb_ref[...])
    x1_ref[0] = x1
    u2 = x1 * (1.0 + sc2_ref[0]) + sh2_ref[0]
    u2_ref[0] = _to_rows(u2.astype(BF16))
    u_hi = u2.astype(BF16)
    u_lo = (u2 - u_hi.astype(F32)).astype(BF16)
    logits = jnp.dot(jnp.concatenate([u_hi, u_hi, u_lo], axis=1), wr_ref[...], preferred_element_type=F32)
    ids, gates = _route(logits.T[0:N_EXPERTS, :])
    ids_ref[0] = ids
    gates_ref[0] = gates


def _mix(x, yas, ys, up, g1, sc2, sh2, lng, lnb, wout, wglu, bglu, wpool, pscale, wr_pad, alpha):
    bsz, seq, d = x.shape
    tm = min(ROW_TILE, seq)
    per = yas[0].shape[1] // tm
    att = [pl.BlockSpec((1, tm, ya.shape[2]), lambda b, i, k=k: (b, jnp.clip(i - k * per, 0, per - 1), 0))
           for k, ya in enumerate(yas)]
    tile = lambda n: pl.BlockSpec((1, tm, n), lambda b, i: (b, i, 0))
    prev = pl.BlockSpec((1, tm, up.shape[2]), lambda b, i: (b, jnp.maximum(i - 1, 0), 0))
    vec = pl.BlockSpec((1, 1, d), lambda b, i: (b, 0, 0))
    full = lambda a: pl.BlockSpec(a.shape, lambda b, i: (0,) * a.ndim)
    lane_rows = pl.BlockSpec((1, 2, tm), lambda b, i: (b, 0, i))
    return pl.pallas_call(
        functools.partial(_mix_kernel, alpha=alpha, tiles_per_variant=per),
        grid=(bsz, seq // tm),
        in_specs=[tile(d), tile(ys.shape[2]), tile(up.shape[2]), prev, vec, vec, vec,
                  full(lng), full(lnb), full(wout), full(wglu), full(bglu), full(wpool), full(pscale), full(wr_pad)]
        + att,
        out_specs=[tile(d), pl.BlockSpec((1, tm) + _row_tile(d), lambda b, i: (b, i, 0, 0)), lane_rows, lane_rows],
        out_shape=[jax.ShapeDtypeStruct((bsz, seq, d), F32), jax.ShapeDtypeStruct((bsz, seq) + _row_tile(d), BF16),
                   jax.ShapeDtypeStruct((bsz, 2, seq), jnp.int32), jax.ShapeDtypeStruct((bsz, 2, seq), F32)],
        compiler_params=_params("parallel", "arbitrary"),
        name="mix",
    )(x, ys, up, up, g1, sc2, sh2, lng, lnb, wout, wglu, bglu, wpool, pscale, wr_pad, *yas)


def _gather_kernel(idx_ref, src_ref, *rest, lo, n_src, partial_src, has_prev):
    if has_prev:
        prev_ref, out_ref, buf, sem = rest
    else:
        out_ref, buf, sem = rest

    @pl.when(pl.program_id(0) == 0)
    def _():
        cp = pltpu.make_async_copy(src_ref.at[pl.ds(lo, n_src)], buf, sem)
        cp.start()
        cp.wait()

    base = pl.program_id(0) * COPY_ROWS

    def move(r, carry):
        s = idx_ref[base + r]
        row = buf[jnp.maximum(s, 0)] if partial_src else buf[s]
        if has_prev:
            row = jnp.where(s >= 0, row, prev_ref[r])
        out_ref[r] = row
        return carry

    lax.fori_loop(0, COPY_ROWS, move, 0, unroll=8)


def _gather_rows(src, idx):
    n = idx.shape[0]
    n_chunks = pl.cdiv(src.shape[0], GATHER_SRC_ROWS)
    n_src = src.shape[0] // n_chunks
    assert n_src * n_chunks == src.shape[0]
    block = pl.BlockSpec((COPY_ROWS,) + src.shape[1:], lambda i, idx: (i, 0, 0))
    out = None
    for c in range(n_chunks):
        has_prev = c > 0
        local = idx - c * n_src
        local = idx if n_chunks == 1 else jnp.where((local >= 0) & (local < n_src), local, -1)
        out = pl.pallas_call(
            functools.partial(_gather_kernel, lo=c * n_src, n_src=n_src, partial_src=n_chunks > 1,
                              has_prev=has_prev),
            grid_spec=pltpu.PrefetchScalarGridSpec(
                num_scalar_prefetch=1, grid=(n // COPY_ROWS,),
                in_specs=[pl.BlockSpec(memory_space=pl.ANY)] + ([block] if has_prev else []),
                out_specs=block,
                scratch_shapes=[pltpu.VMEM((n_src,) + src.shape[1:], src.dtype), pltpu.SemaphoreType.DMA(())]),
            out_shape=jax.ShapeDtypeStruct((n,) + src.shape[1:], src.dtype),
            compiler_params=_params("arbitrary"),
            name="gather",
        )(*((local, src, out) if has_prev else (local, src)))
    return out


def _expert_kernel(te_ref, act_ref, first_ref, nxt_ref, x_ref, wg_hbm, wu_hbm, wd_hbm, y_ref,
                   stage_g, stage_u, stage_d, wg_ref, wu_ref, wd_ref, sems, *, layer):
    i = pl.program_id(0)

    def fetch(e):
        return (pltpu.make_async_copy(wg_hbm.at[layer, e], stage_g, sems.at[0]),
                pltpu.make_async_copy(wu_hbm.at[layer, e], stage_u, sems.at[1]),
                pltpu.make_async_copy(wd_hbm.at[layer, e], stage_d, sems.at[2]))

    @pl.when(i == 0)
    def _():
        for cp in fetch(te_ref[0]):
            cp.start()

    @pl.when(first_ref[i] != 0)
    def _():
        for cp in fetch(te_ref[i]):
            cp.wait()
        for stage, work in ((stage_g, wg_ref), (stage_u, wu_ref), (stage_d, wd_ref)):
            rows = stage.shape[0]

            def convert(r, carry, stage=stage, work=work):
                sl = pl.ds(pl.multiple_of(r * CONVERT_ROWS, CONVERT_ROWS), CONVERT_ROWS)
                work[sl, :] = stage[sl, :].astype(BF16)
                return carry

            lax.fori_loop(0, rows // CONVERT_ROWS, convert, 0)

        @pl.when(nxt_ref[i] >= 0)
        def _():
            for cp in fetch(nxt_ref[i]):
                cp.start()

    @pl.when(act_ref[i] != 0)
    def _():
        x = _from_rows(x_ref[...])
        g = jnp.dot(x, wg_ref[...], preferred_element_type=F32)
        u = jnp.dot(x, wu_ref[...], preferred_element_type=F32)
        h = (g * _sigmoid(g)) * u
        y_ref[...] = _to_rows(jnp.dot(h.astype(BF16), wd_ref[...], preferred_element_type=F32).astype(BF16))

    @pl.when(act_ref[i] == 0)
    def _():
        y_ref[...] = jnp.zeros(y_ref.shape, y_ref.dtype)


def _experts(xg, plan, wg, wu, wd, layer):
    rows = xg.shape[0]
    d, f = wg.shape[2], wg.shape[3]
    tm = EXPERT_TILE
    tile = pl.BlockSpec((tm,) + _row_tile(d), lambda i, *_: (i, 0, 0))
    hbm = pl.BlockSpec(memory_space=pl.ANY)
    return pl.pallas_call(
        functools.partial(_expert_kernel, layer=layer),
        grid_spec=pltpu.PrefetchScalarGridSpec(
            num_scalar_prefetch=4, grid=(rows // tm,),
            in_specs=[tile, hbm, hbm, hbm],
            out_specs=tile,
            scratch_shapes=[pltpu.VMEM((d, f), F32), pltpu.VMEM((d, f), F32), pltpu.VMEM((f, d), F32),
                            pltpu.VMEM((d, f), BF16), pltpu.VMEM((d, f), BF16), pltpu.VMEM((f, d), BF16),
                            pltpu.SemaphoreType.DMA((3,))]),
        out_shape=jax.ShapeDtypeStruct((rows,) + _row_tile(d), BF16),
        compiler_params=_params("arbitrary"),
        name="experts",
    )(*plan, xg, wg, wu, wd)


def _combine_kernel(x_ref, z_ref, gc_ref, g2_ref, lng_ref, lnb_ref, o_ref, *, alpha):
    gc = gc_ref[0]
    ffn = (gc[:, 0:1] * _from_rows(z_ref[0, 0]).astype(F32)
           + gc[:, 1:2] * _from_rows(z_ref[1, 0]).astype(F32))
    o_ref[0] = _layer_norm(alpha * x_ref[0] + (1.0 + g2_ref[0]) * ffn, lng_ref[...], lnb_ref[...])


def _combine(x1, z, gates_col, g2, lng, lnb, alpha):
    bsz, seq, d = x1.shape
    tm = min(ROW_TILE, seq)
    tile = pl.BlockSpec((1, tm, d), lambda b, i: (b, i, 0))
    return pl.pallas_call(
        functools.partial(_combine_kernel, alpha=alpha),
        grid=(bsz, seq // tm),
        in_specs=[tile, pl.BlockSpec((2, 1, tm) + _row_tile(d), lambda b, i: (0, b, i, 0, 0)),
                  pl.BlockSpec((1, tm, 2), lambda b, i: (b, i, 0)),
                  pl.BlockSpec((1, 1, d), lambda b, i: (b, 0, 0)),
                  pl.BlockSpec(lng.shape, lambda b, i: (0, 0)), pl.BlockSpec(lnb.shape, lambda b, i: (0, 0))],
        out_specs=tile,
        out_shape=jax.ShapeDtypeStruct((bsz, seq, d), F32),
        compiler_params=_params("parallel", "arbitrary"),
        name="combine",
    )(x1, z, gates_col, g2, lng, lnb)


def _dispatch_plan(ids, n_tok):
    tm = EXPERT_TILE
    e_pair = jnp.concatenate([ids[:, 0, :].reshape(n_tok), ids[:, 1, :].reshape(n_tok)])
    tok_pair = jnp.concatenate([jnp.arange(n_tok, dtype=jnp.int32)] * 2)
    onehot = (e_pair[:, None] == jnp.arange(N_EXPERTS, dtype=jnp.int32)[None, :]).astype(jnp.int32)
    csum = jnp.cumsum(onehot, axis=0)
    rank = jnp.sum(onehot * csum, axis=1) - 1
    counts = csum[-1]
    padded = ((counts + tm - 1) // tm) * tm
    ends = jnp.cumsum(padded)
    offs = ends - padded
    pos = jnp.sum(onehot * offs[None, :], axis=1) + rank
    rows = 2 * n_tok + N_EXPERTS * tm
    src_token = jnp.zeros((rows,), jnp.int32).at[pos].set(tok_pair)
    tile_start = jnp.arange(rows // tm, dtype=jnp.int32) * tm
    tile_expert = jnp.minimum(jnp.sum((tile_start[:, None] >= ends[None, :]).astype(jnp.int32), axis=1),
                              N_EXPERTS - 1)
    tile_active = (tile_start < ends[-1]).astype(jnp.int32)
    prev_expert = jnp.concatenate([jnp.full((1,), -1, jnp.int32), tile_expert[:-1]])
    tile_first = tile_active * (tile_expert != prev_expert).astype(jnp.int32)
    eid = jnp.arange(N_EXPERTS, dtype=jnp.int32)
    later = (padded > 0)[None, :] & (eid[None, :] > eid[:, None])
    next_expert = jnp.min(jnp.where(later, eid[None, :], N_EXPERTS), axis=1)
    next_expert = jnp.where(next_expert == N_EXPERTS, -1, next_expert).astype(jnp.int32)
    tile_next = next_expert[tile_expert]
    return pos.astype(jnp.int32), src_token, (tile_expert, tile_active, tile_first, tile_next)


def _rope_tables(positions, dim):
    rot = dim // ROPE_FRAC
    half = rot // 2
    inv_freq = ROPE_THETA ** (-jnp.arange(half, dtype=F32) * 2.0 / rot)
    ang = positions.astype(F32)[..., None] * inv_freq
    cos, sin = jnp.cos(ang), jnp.sin(ang)
    shape = ang.shape[:-1]
    one = jnp.ones(shape + (dim - rot,), F32)
    zero = jnp.zeros(shape + (dim - rot,), F32)
    zh = jnp.zeros(shape + (half,), F32)
    c = jnp.concatenate([cos, cos, one], axis=-1)
    s1 = jnp.concatenate([-sin, zh, zero], axis=-1)
    s2 = jnp.concatenate([zh, sin, zero], axis=-1)
    rep = LANES // dim
    return jnp.concatenate([jnp.tile(a, (1, 1, rep)) for a in (c, s1, s2)], axis=-1)


def kernel(x, c, positions, w_ada, b_ada, w_in, w_out, ssm_lam_re, ssm_lam_im, ssm_log_step, ssm_b_re, ssm_b_im, ssm_c_re, ssm_c_im, ssm_d, ssm_w_glu, ssm_b_glu, pool_w, pool_scale, ln1_g, ln1_b, ln2_g, ln2_b, w_router, e_gate, e_up, e_down):
    bsz, seq, d = x.shape
    depth = w_ada.shape[0]
    n_tok = bsz * seq
    alpha = (2.0 * depth) ** 0.25
    sw = ssm_d.shape[1]
    pw = pool_scale.shape[1]
    aw = w_out.shape[1] - sw - pw
    kw = KV_HEADS * HEAD_DIM
    iqw = IDX_HEADS * IDX_DIM
    widths = (aw, kw, iqw, sw, pw)
    top_k = min(MAX_TOPK, seq // 4)

    c_pad = jnp.concatenate([c, jnp.zeros((8 - bsz % 8, d), F32)], axis=0) if bsz % 8 else c
    ada = _ada_all(c_pad, w_ada, b_ada)
    rq = _rope_tables(positions, HEAD_DIM)
    ri = _rope_tables(positions, IDX_DIM)
    wr_pad = jnp.concatenate(
        [w_router.reshape(d, N_EXPERT_GROUPS, EXPERTS_PER_GROUP).transpose(0, 2, 1).reshape(d, N_EXPERTS),
         jnp.zeros((d, LANES - N_EXPERTS), F32)], axis=1)
    wr_hi = wr_pad.astype(BF16)
    wr_lo = (wr_pad - wr_hi.astype(F32)).astype(BF16)
    wr_pad = jnp.concatenate([wr_hi, wr_lo, wr_hi], axis=0)

    o_q, o_k, o_v, o_iq, o_ik, o_iw, o_us, o_up = (0, aw, aw + kw, aw + 2 * kw, aw + 2 * kw + iqw,
                                                   aw + 2 * kw + iqw + IDX_DIM,
                                                   aw + 2 * kw + iqw + IDX_DIM + IDX_HEADS,
                                                   aw + 2 * kw + iqw + IDX_DIM + IDX_HEADS + sw)
    for l in range(depth):
        sh1, sc1, g1, sh2, sc2, g2 = [ada[l, :bsz, j * d:(j + 1) * d].reshape(bsz, 1, d) for j in range(6)]
        wl = w_in[l]
        w_r = jnp.concatenate([wl[:, o_q:o_v], wl[:, o_iq:o_ik], wl[:, o_us:], wl[:, o_ik:o_iw],
                               jnp.zeros((d, LANES - IDX_DIM), F32)], axis=1).astype(BF16)
        wvt = wl[:, o_v:o_iq].T.astype(BF16)
        wiwt = wl[:, o_iw:o_us].T.astype(BF16)
        q, k, vt, iq, ik, iwt, us, up = _inproj(x, sc1, sh1, w_r, wvt, wiwt, rq, ri, widths)
        ya = _attention(q, k, vt, iq, ik, iwt, top_k)
        prep = _ssm_prepare(ssm_lam_re[l], ssm_lam_im[l], ssm_log_step[l], ssm_b_re[l], ssm_b_im[l],
                            ssm_c_re[l], ssm_c_im[l], ssm_d[l], SSM_CHUNK)
        ys = _ssm(us, prep, SSM_CHUNK)
        x1, u2, ids, gates = _mix(x, ya, ys, up, g1, sc2, sh2, ln1_g[l][None], ln1_b[l][None],
                                  w_out[l].astype(BF16), ssm_w_glu[l].astype(BF16), ssm_b_glu[l][None],
                                  pool_w[l].astype(BF16), pool_scale[l][None], wr_pad, alpha)
        pos, src_token, tile_plan = _dispatch_plan(ids, n_tok)
        xg = _gather_rows(u2.reshape((n_tok,) + _row_tile(d)), src_token)
        yg = _experts(xg, tile_plan, e_gate, e_up, e_down, l)
        z = _gather_rows(yg, pos).reshape((2, bsz, seq) + _row_tile(d))
        x = _combine(x1, z, gates.transpose(0, 2, 1), g2, ln2_g[l][None], ln2_b[l][None], alpha)
    return x
```

```python
import functools
import math

import jax
import jax.numpy as jnp
import numpy as np
from jax import lax
from jax.experimental import pallas as pl
from jax.experimental.pallas import tpu as pltpu

F32 = jnp.float32
BF16 = jnp.bfloat16

HEAD_DIM = 128
KV_HEADS = 2
IDX_HEADS = 8
IDX_DIM = 64
MAX_TOPK = 256
SSM_GROUP_CH = 16
SSM_STATE = 64
POOL_WINDOWS = (2, 4, 8, 16)
ROPE_THETA = 500000.0
ROPE_FRAC = 4
LN_EPS = 1e-5
N_EXPERTS = 16
N_EXPERT_GROUPS = 4
EXPERTS_PER_GROUP = N_EXPERTS // N_EXPERT_GROUPS

LANES = 128
VMEM_LIMIT = 56 * 1024 * 1024
ROW_TILE = 256
COMBINE_TILE = 128
ATTN_Q_TILE = 256
ATTN_KEY_STEP = 512
BISECT_STEPS = 14
REDUCE_SLAB = 64
ROW_SUBLANES = 16
GATHER_SRC_ROWS = 10240
SSM_CHUNK = 8
EXPERT_TILE = 256
CONVERT_ROWS = 128
COPY_ROWS = 256
ADA_COLS = 1024


def _params(*sem):
    return pltpu.CompilerParams(dimension_semantics=sem, vmem_limit_bytes=VMEM_LIMIT)


def _sigmoid(x):
    return 1.0 / (1.0 + jnp.exp(-x))


def _ada_kernel(c_ref, w_ref, b_ref, o_ref):
    c = c_ref[...]
    cond = (c * _sigmoid(c)).astype(BF16)
    o_ref[0] = jnp.dot(cond, w_ref[0].astype(BF16), preferred_element_type=F32) + b_ref[0]


def _ada_all(c_pad, w_ada, b_ada):
    depth, d, n6 = w_ada.shape
    rows = c_pad.shape[0]
    return pl.pallas_call(
        _ada_kernel,
        grid=(depth, n6 // ADA_COLS),
        in_specs=[pl.BlockSpec((rows, d), lambda l, j: (0, 0)),
                  pl.BlockSpec((1, d, ADA_COLS), lambda l, j: (l, 0, j)),
                  pl.BlockSpec((1, 1, ADA_COLS), lambda l, j: (l, 0, j))],
        out_specs=pl.BlockSpec((1, rows, ADA_COLS), lambda l, j: (l, 0, j)),
        out_shape=jax.ShapeDtypeStruct((depth, rows, n6), F32),
        compiler_params=_params("arbitrary", "arbitrary"),
        name="ada",
    )(c_pad, w_ada, b_ada.reshape(depth, 1, n6))


def _rope(xv, tab, half):
    w = xv.shape[1]
    rep = w // LANES
    c = jnp.tile(tab[:, 0:LANES], (1, rep))
    s1 = jnp.tile(tab[:, LANES:2 * LANES], (1, rep))
    s2 = jnp.tile(tab[:, 2 * LANES:3 * LANES], (1, rep))
    return xv * c + pltpu.roll(xv, w - half, 1) * s1 + pltpu.roll(xv, half, 1) * s2


def _inproj_kernel(x_ref, sc_ref, sh_ref, w_ref, wvt_ref, wiwt_ref, rq_ref, ri_ref,
                   q_ref, k_ref, vt_ref, iq_ref, ik_ref, iwt_ref, us_ref, up_ref, *, widths, idx_scale):
    aw, kw, iqw, sw, pw = widths
    u = (x_ref[0] * (1.0 + sc_ref[0]) + sh_ref[0]).astype(BF16)
    rq = rq_ref[0]
    ri = ri_ref[0]

    def mm(lo, n):
        return jnp.dot(u, w_ref[:, lo:lo + n], preferred_element_type=F32)

    def mm_t(wt_ref):
        return lax.dot_general(wt_ref[...], u, (((1,), (1,)), ((), ())), preferred_element_type=F32)

    qhalf = HEAD_DIM // ROPE_FRAC // 2
    ihalf = IDX_DIM // ROPE_FRAC // 2
    o = 0
    q_ref[0] = (_rope(mm(o, aw), rq, qhalf) * (HEAD_DIM ** -0.5)).astype(BF16)
    o += aw
    k_ref[0] = _rope(mm(o, kw), rq, qhalf).astype(BF16)
    o += kw
    iq_ref[0] = _rope(mm(o, iqw), ri, ihalf).astype(BF16)
    o += iqw
    us_ref[0] = mm(o, sw)
    o += sw
    up_ref[0] = mm(o, pw)
    o += pw
    ik_ref[0] = _rope(mm(o, LANES), ri, ihalf)[:, 0:IDX_DIM].astype(BF16)
    vt_ref[0] = mm_t(wvt_ref).astype(BF16)
    iwt_ref[0] = mm_t(wiwt_ref) * idx_scale


def _inproj(x, sc, sh, w_r, wvt, wiwt, rq, ri, widths):
    bsz, seq, d = x.shape
    aw, kw, iqw, sw, pw = widths
    tm = min(ROW_TILE, seq)
    tile = lambda n: pl.BlockSpec((1, tm, n), lambda b, i: (b, i, 0))
    tile_t = lambda n: pl.BlockSpec((1, n, tm), lambda b, i: (b, 0, i))
    vec = pl.BlockSpec((1, 1, d), lambda b, i: (b, 0, 0))
    full = lambda a: pl.BlockSpec(a.shape, lambda b, i: (0, 0))
    sds = jax.ShapeDtypeStruct
    return pl.pallas_call(
        functools.partial(_inproj_kernel, widths=widths, idx_scale=(IDX_DIM ** -0.5) * (IDX_HEADS ** -0.5)),
        grid=(bsz, seq // tm),
        in_specs=[tile(d), vec, vec, full(w_r), full(wvt), full(wiwt), tile(3 * LANES), tile(3 * LANES)],
        out_specs=[tile(aw), tile(kw), tile_t(kw), tile(iqw), tile(IDX_DIM), tile_t(IDX_HEADS), tile(sw), tile(pw)],
        out_shape=[sds((bsz, seq, aw), BF16), sds((bsz, seq, kw), BF16), sds((bsz, kw, seq), BF16),
                   sds((bsz, seq, iqw), BF16), sds((bsz, seq, IDX_DIM), BF16), sds((bsz, IDX_HEADS, seq), F32),
                   sds((bsz, seq, sw), F32), sds((bsz, seq, pw), F32)],
        compiler_params=_params("parallel", "arbitrary"),
        name="inproj",
    )(x, sc, sh, w_r, wvt, wiwt, rq, ri)


def _col_reduce(op, v):
    rows, n = v.shape
    slab = op(v.reshape(rows // REDUCE_SLAB, REDUCE_SLAB, n), axis=0)
    return op(slab, axis=0, keepdims=True)


def _attn_kernel(q_ref, iq_ref, iwt_ref, k_ref, vt_ref, ik_ref, o_ref, mask_ref, x_ref, *, q0, tq, keys, top_k):
    i = pl.program_id(1)
    t = q0 + i * tq + lax.broadcasted_iota(jnp.int32, (1, tq), 1)
    kpos = lax.broadcasted_iota(jnp.int32, (keys, 1), 0)
    causal = kpos <= t

    iq = iq_ref[0]
    ik = ik_ref[0]
    iwt = iwt_ref[0]
    isc = jnp.zeros((keys, tq), F32)
    for h in range(IDX_HEADS):
        r = lax.dot_general(ik, iq[:, h * IDX_DIM:(h + 1) * IDX_DIM], (((1,), (1,)), ((), ())),
                            preferred_element_type=F32)
        isc = isc + jnp.maximum(r, 0.0) * iwt[h:h + 1, :]
    x = jnp.where(causal, isc, -jnp.inf)
    x_ref[...] = x

    kf = float(top_k)
    n_valid = (t + 1).astype(F32)
    need = n_valid > kf

    def fold(step_fn, init):
        def body(s, acc):
            row = pl.multiple_of(s * REDUCE_SLAB, REDUCE_SLAB)
            return step_fn(acc, x_ref[pl.ds(row, REDUCE_SLAB), :])
        acc0 = jax.tree.map(lambda v: jnp.full((REDUCE_SLAB, tq), v, F32), init)
        return lax.fori_loop(0, keys // REDUCE_SLAB, body, acc0, unroll=4)

    def count_ge(theta):
        acc = fold(lambda a, xs: a + jnp.where(xs >= theta, 1.0, 0.0), 0.0)
        return jnp.sum(acc, axis=0, keepdims=True)

    rmax = _col_reduce(jnp.max, x)
    rmin = _col_reduce(jnp.min, jnp.where(causal, isc, jnp.inf))
    c_max = count_ge(rmax)
    top_tie = c_max >= kf
    lo0 = jnp.where(need, jnp.where(top_tie, rmax, rmin), -jnp.inf)
    clo0 = jnp.where(top_tie, c_max, n_valid)
    done0 = jnp.where(need & jnp.logical_not(top_tie) & (clo0 != kf), 0.0, 1.0)

    def bisect(lo, hi, clo, chi, done):
        mid = 0.5 * lo + 0.5 * hi
        c = count_ge(mid)
        live = done == 0.0
        ge = c >= kf
        up = live & ge
        dn = live & jnp.logical_not(ge)
        lo = jnp.where(up, mid, lo)
        clo = jnp.where(up, c, clo)
        hi = jnp.where(dn, mid, hi)
        chi = jnp.where(dn, c, chi)
        done = jnp.where(clo == kf, 1.0, done)
        return lo, hi, clo, chi, done

    def n_active(done):
        return jnp.sum(1.0 - done).astype(jnp.int32)

    def body2(s):
        _, lo, hi, clo, chi, done = s
        lo, hi, clo, chi, done = bisect(lo, hi, clo, chi, done)

        def min_max(acc, xs):
            inside = (xs >= lo) & (xs < hi)
            return (jnp.minimum(acc[0], jnp.where(inside, xs, jnp.inf)),
                    jnp.maximum(acc[1], jnp.where(inside, xs, -jnp.inf)))

        mins, maxs = fold(min_max, (jnp.inf, -jnp.inf))
        vmin = jnp.min(mins, axis=0, keepdims=True)
        vmax = jnp.max(maxs, axis=0, keepdims=True)
        c2 = count_ge(vmax)
        live = done == 0.0
        single = vmin == vmax
        top_ok = c2 >= kf
        take_top = live & jnp.logical_not(single) & top_ok
        drop_top = live & jnp.logical_not(single) & jnp.logical_not(top_ok)
        lo = jnp.where(live, jnp.where(take_top, vmax, vmin), lo)
        clo = jnp.where(take_top, c2, clo)
        hi = jnp.where(drop_top, vmax, hi)
        chi = jnp.where(drop_top, c2, chi)
        done = jnp.where(live & (single | top_ok), 1.0, done)
        return n_active(done), lo, hi, clo, chi, done

    state = lax.fori_loop(0, BISECT_STEPS, lambda _, s: bisect(*s), (lo0, rmax, clo0, c_max, done0))
    state = lax.while_loop(lambda s: s[0] > 0, body2, (n_active(state[4]),) + state)
    _, lo, _, clo, _, _ = state

    mask_ref[...] = jnp.where(causal & (x_ref[...] >= lo), 0.0, -1e30)

    tie_q = jnp.sum(jnp.where(need & (clo > kf), 1.0, 0.0)).astype(jnp.int32)

    @pl.when(tie_q > 0)
    def _():
        room = kf - _col_reduce(jnp.sum, jnp.where(x_ref[...] > lo, 1.0, 0.0))
        step = 256
        rr = lax.broadcasted_iota(jnp.int32, (step, step), 0)
        cc = lax.broadcasted_iota(jnp.int32, (step, step), 1)
        before = jnp.where(cc < rr, 1.0, 0.0).astype(BF16)
        carry = jnp.zeros((1, tq), F32)
        for c0 in range(0, keys, step):
            xs = x_ref[c0:c0 + step, :]
            e = jnp.where(xs == lo, 1.0, 0.0)
            rank = carry + jnp.dot(before, e.astype(BF16), preferred_element_type=F32)
            keep = (xs > lo) | ((xs == lo) & (rank < room))
            mask_ref[c0:c0 + step, :] = jnp.where(keep & (xs > -jnp.inf), 0.0, -1e30)
            carry = carry + jnp.sum(e, axis=0, keepdims=True)

    bias = mask_ref[...]
    q = q_ref[0]
    group = q.shape[1] // HEAD_DIM // KV_HEADS
    bias_g = jnp.concatenate([bias] * group, axis=1)
    for g in range(KV_HEADS):
        qg = jnp.concatenate([q[:, (g * group + j) * HEAD_DIM:(g * group + j + 1) * HEAD_DIM]
                              for j in range(group)], axis=0)
        kg = k_ref[0, :, g * HEAD_DIM:(g + 1) * HEAD_DIM]
        vtg = vt_ref[0, g * HEAD_DIM:(g + 1) * HEAD_DIM, :]
        s = lax.dot_general(kg, qg, (((1,), (1,)), ((), ())), preferred_element_type=F32) + bias_g
        m = _col_reduce(jnp.max, s)
        p = jnp.exp(s - m)
        den = _col_reduce(jnp.sum, p)
        og = jnp.dot(vtg, p.astype(BF16), preferred_element_type=F32) / den
        for j in range(group):
            h = g * group + j
            o_ref[0, :, h * HEAD_DIM:(h + 1) * HEAD_DIM] = og[:, j * tq:(j + 1) * tq].T.astype(BF16)


def _attention(q, k, vt, iq, ik, iwt, top_k):
    bsz, seq, aw = q.shape
    tq = min(ATTN_Q_TILE, seq)
    step = min(ATTN_KEY_STEP, seq)
    outs = []
    for q0 in range(0, seq, step):
        keys = q0 + step
        nq = step // tq
        qtile = lambda n, q0=q0: pl.BlockSpec((1, tq, n), lambda b, i: (b, q0 // tq + i, 0))
        ktile = lambda n, keys=keys: pl.BlockSpec((1, keys, n), lambda b, i: (b, 0, 0))
        outs.append(pl.pallas_call(
            functools.partial(_attn_kernel, q0=q0, tq=tq, keys=keys, top_k=top_k),
            grid=(bsz, nq),
            in_specs=[qtile(aw), qtile(iq.shape[2]),
                      pl.BlockSpec((1, iwt.shape[1], tq), lambda b, i, q0=q0: (b, 0, q0 // tq + i)),
                      ktile(k.shape[2]),
                      pl.BlockSpec((1, vt.shape[1], keys), lambda b, i: (b, 0, 0)),
                      ktile(ik.shape[2])],
            out_specs=pl.BlockSpec((1, tq, aw), lambda b, i: (b, i, 0)),
            out_shape=jax.ShapeDtypeStruct((bsz, step, aw), BF16),
            scratch_shapes=[pltpu.VMEM((keys, tq), F32), pltpu.VMEM((keys, tq), F32)],
            compiler_params=_params("parallel", "arbitrary"),
            name=f"attn_k{keys}",
        )(q, iq, iwt, k, vt, ik))
    return outs


def _ssm_kernel(u_ref, kin_ref, wre_ref, wim_ref, vre_ref, vim_ref, are_ref, aim_ref, y_ref,
                sre_ref, sim_ref, xre_ref, xim_ref, *, chunk):
    bsz, seq, lanes = u_ref.shape
    nc = seq // chunk
    rows = bsz * nc
    ns = wre_ref.shape[2]
    u = u_ref[...].reshape(rows, chunk, lanes).reshape(rows, chunk * lanes).astype(BF16)
    sre_ref[...] = jnp.dot(u, wre_ref[0], preferred_element_type=F32).reshape(bsz, nc, ns)
    sim_ref[...] = jnp.dot(u, wim_ref[0], preferred_element_type=F32).reshape(bsz, nc, ns)
    a_re = are_ref[0]
    a_im = aim_ref[0]
    x_re = jnp.zeros((bsz, ns), F32)
    x_im = jnp.zeros((bsz, ns), F32)
    for c in range(nc):
        xre_ref[:, c, :] = x_re
        xim_ref[:, c, :] = x_im
        n_re = a_re * x_re - a_im * x_im + sre_ref[:, c, :]
        n_im = a_re * x_im + a_im * x_re + sim_ref[:, c, :]
        x_re, x_im = n_re, n_im
    xr = xre_ref[...].reshape(rows, ns).astype(BF16)
    xi = xim_ref[...].reshape(rows, ns).astype(BF16)
    y = (jnp.dot(u, kin_ref[0], preferred_element_type=F32)
         + jnp.dot(xr, vre_ref[0], preferred_element_type=F32)
         + jnp.dot(xi, vim_ref[0], preferred_element_type=F32))
    y_ref[...] = y.reshape(rows, chunk, lanes).reshape(bsz, seq, lanes)


def _ssm_placement(gs, ch, p):
    a = np.arange(gs)[:, None, None]
    lane = np.arange(gs * ch)[None, None, :]
    on_lane = ((lane // ch) == a) & ((lane % ch) == np.arange(ch)[None, :, None])
    q = np.arange(gs * p)[None, None, :]
    on_state = ((q // p) == a) & ((q % p) == np.arange(p)[None, :, None])
    return jnp.asarray(on_lane, F32), jnp.asarray(on_state, F32)


def _ssm_prepare(lam_re, lam_im, log_step, b_re, b_im, c_re, c_im, d_skip, chunk):
    g, p = lam_re.shape
    ch = b_re.shape[2]
    hp = lax.Precision.HIGHEST
    step = jnp.exp(log_step)[:, None]
    lsr, lsi = lam_re * step, lam_im * step
    er = jnp.exp(lsr)
    nr, ni = er * jnp.cos(lsi) - 1.0, er * jnp.sin(lsi)
    den = lam_re * lam_re + lam_im * lam_im
    fr, fi = (nr * lam_re + ni * lam_im) / den, (ni * lam_re - nr * lam_im) / den
    bbr = fr[..., None] * b_re - fi[..., None] * b_im
    bbi = fr[..., None] * b_im + fi[..., None] * b_re
    n = jnp.arange(chunk + 1, dtype=F32)
    mag = jnp.exp(lsr[..., None] * n)
    pr, pi = mag * jnp.cos(lsi[..., None] * n), mag * jnp.sin(lsi[..., None] * n)
    prt, pit = pr[:, :, :chunk, None], pi[:, :, :chunk, None]
    wr = prt * bbr[:, :, None, :] - pit * bbi[:, :, None, :]
    wi = prt * bbi[:, :, None, :] + pit * bbr[:, :, None, :]
    ker = (jnp.einsum("gcp,gptd->gtcd", c_re, wr, precision=hp)
           - jnp.einsum("gcp,gptd->gtcd", c_im, wi, precision=hp))
    ker = ker.at[:, 0].add(jnp.eye(ch, dtype=F32) * d_skip.reshape(g, 1, ch))
    pr1, pi1 = pr[:, None, :, 1:], pi[:, None, :, 1:]
    vr = c_re[..., None] * pr1 - c_im[..., None] * pi1
    vi = c_re[..., None] * pi1 + c_im[..., None] * pr1
    gs = LANES // ch
    ns = g // gs
    lanes = chunk * LANES
    on_lane, on_state = _ssm_placement(gs, ch, p)
    bf16_exact = lambda m: m.astype(BF16).astype(F32)
    by_tile = lambda m: m.reshape(ns, gs, *m.shape[1:])
    lag_tile = jnp.einsum("satcd,acl->stadl", by_tile(bf16_exact(ker)), on_lane).reshape(ns, chunk, LANES, LANES)
    lag = jnp.arange(chunk)[None, :] - jnp.arange(chunk)[:, None]
    tiles = jnp.where((lag >= 0)[None, :, :, None, None], lag_tile[:, jnp.clip(lag, 0, chunk - 1)], 0.0)
    kin8 = tiles.transpose(0, 1, 3, 2, 4).reshape(ns, lanes, lanes).astype(BF16)

    def to_in(w):
        wide = jnp.einsum("sapid,apq->siadq", by_tile(bf16_exact(w[:, :, ::-1, :])), on_state)
        return wide.reshape(ns, lanes, gs * p).astype(BF16)

    def to_out(v):
        wide = jnp.einsum("sacpj,acl->sapjl", by_tile(bf16_exact(v)), on_lane)
        return wide.reshape(ns, gs * p, lanes).astype(BF16)

    a_re = pr[..., chunk].reshape(ns, 1, gs * p)
    a_im = pi[..., chunk].reshape(ns, 1, gs * p)
    return kin8, to_in(wr), to_in(wi), to_out(vr), to_out(-vi), a_re, a_im


def _ssm(us, prep, chunk):
    bsz, seq, w = us.shape
    kin, wre, wim, vre, vim, a_re, a_im = prep
    ns, lanes, states = wre.shape
    nc = seq // chunk
    act = pl.BlockSpec((bsz, seq, LANES), lambda i: (0, 0, i))
    mat = lambda a, b: pl.BlockSpec((1, a, b), lambda i: (i, 0, 0))
    return pl.pallas_call(
        functools.partial(_ssm_kernel, chunk=chunk),
        grid=(ns,),
        in_specs=[act, mat(lanes, lanes), mat(lanes, states), mat(lanes, states), mat(states, lanes),
                  mat(states, lanes), mat(1, states), mat(1, states)],
        out_specs=act,
        out_shape=jax.ShapeDtypeStruct((bsz, seq, w), F32),
        scratch_shapes=[pltpu.VMEM((bsz, nc, states), F32)] * 4,
        compiler_params=_params("parallel"),
        name="ssm",
    )(us, kin, wre, wim, vre, vim, a_re, a_im)


def _row_tile(d):
    return (ROW_SUBLANES, d // ROW_SUBLANES)


def _to_rows(v):
    return v.reshape((v.shape[0],) + _row_tile(v.shape[1]))


def _from_rows(v):
    return v.reshape(v.shape[0], v.shape[1] * v.shape[2])


def _layer_norm(h, g, b):
    mu = jnp.mean(h, axis=1, keepdims=True)
    hc = h - mu
    var = jnp.mean(hc * hc, axis=1, keepdims=True)
    return hc * lax.rsqrt(var + LN_EPS) * g + b


def _route(logits_t):
    ng, per = N_EXPERT_GROUPS, EXPERTS_PER_GROUP
    m = jnp.max(logits_t, axis=0, keepdims=True)
    e = jnp.exp(logits_t - m)
    prob = e / jnp.sum(e, axis=0, keepdims=True)
    v = [prob[j * ng:(j + 1) * ng, :] for j in range(per)]

    def top(vals):
        best = functools.reduce(jnp.maximum, vals)
        idx = jnp.full(best.shape, per - 1, jnp.int32)
        for j in range(per - 2, -1, -1):
            idx = jnp.where(vals[j] == best, j, idx)
        return best, idx

    m1, i1 = top(v)
    m2, i2 = top([jnp.where(i1 == j, -1.0, v[j]) for j in range(per)])
    score = m1 + m2
    gid = lax.broadcasted_iota(jnp.int32, score.shape, 0)
    best_g = jnp.min(jnp.where(score == jnp.max(score, axis=0, keepdims=True), gid, ng), axis=0, keepdims=True)
    sel = gid == best_g
    pick_f = lambda a: jnp.sum(jnp.where(sel, a, 0.0), axis=0, keepdims=True)
    pick_i = lambda a: jnp.sum(jnp.where(sel, a, 0), axis=0, keepdims=True)
    p1, p2 = pick_f(m1), pick_f(m2)
    e1, e2 = best_g * per + pick_i(i1), best_g * per + pick_i(i2)
    den = p1 + p2
    return jnp.concatenate([e1, e2], axis=0), jnp.concatenate([p1 / den, p2 / den], axis=0)


def _mix_kernel(x_ref, ys_ref, upc_ref, upp_ref, g1_ref, sc2_ref, sh2_ref, lng_ref, lnb_ref,
                wout_ref, wglu_ref, bglu_ref, wpool_ref, pscale_ref, wr_ref, *rest, alpha, tiles_per_variant):
    ya_refs, (x1_ref, u2_ref, ids_ref, gates_ref) = rest[:-4], rest[-4:]
    i = pl.program_id(1)
    tm = x_ref.shape[1]
    owner = i // tiles_per_variant
    ya = ya_refs[0][0]
    for k in range(1, len(ya_refs)):
        ya = jnp.where(owner == k, ya_refs[k][0], ya)
    y = ys_ref[0]
    y = 0.5 * y * (1.0 + jnp.tanh(math.sqrt(2.0 / math.pi) * (y + 0.044715 * (y * y * y))))
    z = jnp.dot(y.astype(BF16), wglu_ref[...], preferred_element_type=F32) + bglu_ref[...]
    y = y * _sigmoid(z)
    upc = upc_ref[0]
    upp = jnp.where(i > 0, upp_ref[0], 0.0)
    cat = jnp.concatenate([upp, upc], axis=0).astype(BF16)
    r = lax.broadcasted_iota(jnp.int32, (tm, 1), 0)
    lagm = (r + tm) - lax.broadcasted_iota(jnp.int32, (1, 2 * tm), 1)
    tpos = (i * tm + r + 1).astype(F32)
    gc = upc.shape[1] // len(POOL_WINDOWS)
    pooled = []
    for g, win in enumerate(POOL_WINDOWS):
        band = jnp.where((lagm >= 0) & (lagm < win), 1.0, 0.0).astype(BF16)
        ws = jnp.dot(band, cat[:, g * gc:(g + 1) * gc], preferred_element_type=F32)
        pg = ws / jnp.minimum(tpos, float(win)) - upc[:, g * gc:(g + 1) * gc]
        pooled.append(jnp.dot(pg.astype(BF16), wpool_ref[g], preferred_element_type=F32))
    yp = jnp.concatenate(pooled, axis=1) * pscale_ref[...]
    mixed = jnp.concatenate([ya, y.astype(BF16), yp.astype(BF16)], axis=1)
    mix = jnp.dot(mixed, wout_ref[...], preferred_element_type=F32)
    x1 = _layer_norm(alpha * x_ref[0] + (1.0 + g1_ref[0]) * mix, lng_ref[...], lnb_ref[...])
    x1_ref[0] = x1
    u2 = x1 * (1.0 + sc2_ref[0]) + sh2_ref[0]
    u2_ref[0] = _to_rows(u2.astype(BF16))
    u_hi = u2.astype(BF16)
    u_lo = (u2 - u_hi.astype(F32)).astype(BF16)
    logits = jnp.dot(jnp.concatenate([u_hi, u_hi, u_lo], axis=1), wr_ref[...], preferred_element_type=F32)
    ids, gates = _route(logits.T[0:N_EXPERTS, :])
    ids_ref[0] = ids
    gates_ref[0] = gates


def _mix(x, yas, ys, up, g1, sc2, sh2, lng, lnb, wout, wglu, bglu, wpool, pscale, wr_pad, alpha):
    bsz, seq, d = x.shape
    tm = min(ROW_TILE, seq)
    per = yas[0].shape[1] // tm
    att = [pl.BlockSpec((1, tm, ya.shape[2]), lambda b, i, k=k: (b, jnp.clip(i - k * per, 0, per - 1), 0))
           for k, ya in enumerate(yas)]
    tile = lambda n: pl.BlockSpec((1, tm, n), lambda b, i: (b, i, 0))
    prev = pl.BlockSpec((1, tm, up.shape[2]), lambda b, i: (b, jnp.maximum(i - 1, 0), 0))
    vec = pl.BlockSpec((1, 1, d), lambda b, i: (b, 0, 0))
    full = lambda a: pl.BlockSpec(a.shape, lambda b, i: (0,) * a.ndim)
    lane_rows = pl.BlockSpec((1, 2, tm), lambda b, i: (b, 0, i))
    return pl.pallas_call(
        functools.partial(_mix_kernel, alpha=alpha, tiles_per_variant=per),
        grid=(bsz, seq // tm),
        in_specs=[tile(d), tile(ys.shape[2]), tile(up.shape[2]), prev, vec, vec, vec,
                  full(lng), full(lnb), full(wout), full(wglu), full(bglu), full(wpool), full(pscale), full(wr_pad)]
        + att,
        out_specs=[tile(d), pl.BlockSpec((1, tm) + _row_tile(d), lambda b, i: (b, i, 0, 0)), lane_rows, lane_rows],
        out_shape=[jax.ShapeDtypeStruct((bsz, seq, d), F32), jax.ShapeDtypeStruct((bsz, seq) + _row_tile(d), BF16),
                   jax.ShapeDtypeStruct((bsz, 2, seq), jnp.int32), jax.ShapeDtypeStruct((bsz, 2, seq), F32)],
        compiler_params=_params("parallel", "arbitrary"),
        name="mix",
    )(x, ys, up, up, g1, sc2, sh2, lng, lnb, wout, wglu, bglu, wpool, pscale, wr_pad, *yas)


def _gather_kernel(idx_ref, src_ref, *rest, lo, n_src, partial_src, has_prev):
    if has_prev:
        prev_ref, out_ref, buf, sem = rest
    else:
        out_ref, buf, sem = rest

    @pl.when(pl.program_id(0) == 0)
    def _():
        cp = pltpu.make_async_copy(src_ref.at[pl.ds(lo, n_src)], buf, sem)
        cp.start()
        cp.wait()

    base = pl.program_id(0) * COPY_ROWS

    def move(r, carry):
        s = idx_ref[base + r]
        row = buf[jnp.maximum(s, 0)] if partial_src else buf[s]
        if has_prev:
            row = jnp.where(s >= 0, row, prev_ref[r])
        out_ref[r] = row
        return carry

    lax.fori_loop(0, COPY_ROWS, move, 0, unroll=8)


def _slab_plan(src, idx):
    n_chunks = pl.cdiv(src.shape[0], GATHER_SRC_ROWS)
    n_src = src.shape[0] // n_chunks
    assert n_src * n_chunks == src.shape[0]
    if n_chunks == 1:
        return n_src, [(0, idx)]
    local = [idx - c * n_src for c in range(n_chunks)]
    return n_src, [(c * n_src, jnp.where((loc >= 0) & (loc < n_src), loc, -1)) for c, loc in enumerate(local)]


def _gather_rows(src, idx, keep_last_slab=False):
    n = idx.shape[0]
    n_src, plan = _slab_plan(src, idx)
    n_chunks = len(plan)
    block = pl.BlockSpec((COPY_ROWS,) + src.shape[1:], lambda i, idx: (i, 0, 0))
    out = None
    for c, (lo, local) in enumerate(plan[:-1] if keep_last_slab else plan):
        has_prev = c > 0
        out = pl.pallas_call(
            functools.partial(_gather_kernel, lo=lo, n_src=n_src, partial_src=n_chunks > 1, has_prev=has_prev),
            grid_spec=pltpu.PrefetchScalarGridSpec(
                num_scalar_prefetch=1, grid=(n // COPY_ROWS,),
                in_specs=[pl.BlockSpec(memory_space=pl.ANY)] + ([block] if has_prev else []),
                out_specs=block,
                scratch_shapes=[pltpu.VMEM((n_src,) + src.shape[1:], src.dtype), pltpu.SemaphoreType.DMA(())]),
            out_shape=jax.ShapeDtypeStruct((n,) + src.shape[1:], src.dtype),
            compiler_params=_params("arbitrary"),
            name="gather",
        )(*((local, src, out) if has_prev else (local, src)))
    return (out, n_src, plan[-1]) if keep_last_slab else out


def _expert_kernel(te_ref, act_ref, first_ref, nxt_ref, x_ref, wg_hbm, wu_hbm, wd_hbm, y_ref,
                   stage_g, stage_u, stage_d, wg_ref, wu_ref, wd_ref, sems, *, layer):
    i = pl.program_id(0)

    def fetch(e):
        return (pltpu.make_async_copy(wg_hbm.at[layer, e], stage_g, sems.at[0]),
                pltpu.make_async_copy(wu_hbm.at[layer, e], stage_u, sems.at[1]),
                pltpu.make_async_copy(wd_hbm.at[layer, e], stage_d, sems.at[2]))

    @pl.when(i == 0)
    def _():
        for cp in fetch(te_ref[0]):
            cp.start()

    @pl.when(first_ref[i] != 0)
    def _():
        for cp in fetch(te_ref[i]):
            cp.wait()
        for stage, work in ((stage_g, wg_ref), (stage_u, wu_ref), (stage_d, wd_ref)):
            rows = stage.shape[0]

            def convert(r, carry, stage=stage, work=work):
                sl = pl.ds(pl.multiple_of(r * CONVERT_ROWS, CONVERT_ROWS), CONVERT_ROWS)
                work[sl, :] = stage[sl, :].astype(BF16)
                return carry

            lax.fori_loop(0, rows // CONVERT_ROWS, convert, 0)

        @pl.when(nxt_ref[i] >= 0)
        def _():
            for cp in fetch(nxt_ref[i]):
                cp.start()

    @pl.when(act_ref[i] != 0)
    def _():
        x = _from_rows(x_ref[...])
        g = jnp.dot(x, wg_ref[...], preferred_element_type=F32)
        u = jnp.dot(x, wu_ref[...], preferred_element_type=F32)
        h = (g * _sigmoid(g)) * u
        y_ref[...] = _to_rows(jnp.dot(h.astype(BF16), wd_ref[...], preferred_element_type=F32).astype(BF16))

    @pl.when(act_ref[i] == 0)
    def _():
        y_ref[...] = jnp.zeros(y_ref.shape, y_ref.dtype)


def _experts(xg, plan, wg, wu, wd, layer):
    rows = xg.shape[0]
    d, f = wg.shape[2], wg.shape[3]
    tm = EXPERT_TILE
    tile = pl.BlockSpec((tm,) + _row_tile(d), lambda i, *_: (i, 0, 0))
    hbm = pl.BlockSpec(memory_space=pl.ANY)
    return pl.pallas_call(
        functools.partial(_expert_kernel, layer=layer),
        grid_spec=pltpu.PrefetchScalarGridSpec(
            num_scalar_prefetch=4, grid=(rows // tm,),
            in_specs=[tile, hbm, hbm, hbm],
            out_specs=tile,
            scratch_shapes=[pltpu.VMEM((d, f), F32), pltpu.VMEM((d, f), F32), pltpu.VMEM((f, d), F32),
                            pltpu.VMEM((d, f), BF16), pltpu.VMEM((d, f), BF16), pltpu.VMEM((f, d), BF16),
                            pltpu.SemaphoreType.DMA((3,))]),
        out_shape=jax.ShapeDtypeStruct((rows,) + _row_tile(d), BF16),
        compiler_params=_params("arbitrary"),
        name="experts",
    )(*plan, xg, wg, wu, wd)


def _combine_kernel(idx_ref, src_ref, *rest, lo, n_src, n_tok, partial_src, has_prev, alpha):
    if has_prev:
        prev_ref, x_ref, gc_ref, g2_ref, lng_ref, lnb_ref, o_ref, buf, rows, sem = rest
    else:
        x_ref, gc_ref, g2_ref, lng_ref, lnb_ref, o_ref, buf, rows, sem = rest

    @pl.when(pl.program_id(0) == 0)
    def _():
        cp = pltpu.make_async_copy(src_ref.at[pl.ds(lo, n_src)], buf, sem)
        cp.start()
        cp.wait()

    tm = x_ref.shape[0]
    base = pl.program_id(0) * tm
    for k in range(2):
        def move(r, carry, k=k):
            s = idx_ref[k * n_tok + base + r]
            row = buf[jnp.maximum(s, 0)] if partial_src else buf[s]
            if has_prev:
                row = jnp.where(s >= 0, row, prev_ref[k, r])
            rows[k, r] = row
            return carry

        lax.fori_loop(0, tm, move, 0, unroll=8)
    gc = gc_ref[...]
    ffn = gc[:, 0:1] * _from_rows(rows[0]).astype(F32) + gc[:, 1:2] * _from_rows(rows[1]).astype(F32)
    o_ref[...] = _layer_norm(alpha * x_ref[...] + (1.0 + g2_ref[0]) * ffn, lng_ref[...], lnb_ref[...])


def _combine(yg, pos, x1, gates_col, g2, lng, lnb, alpha):
    bsz, seq, d = x1.shape
    n_tok = bsz * seq
    tm = COMBINE_TILE
    per_batch = seq // tm
    prev, n_src, (lo, local) = _gather_rows(yg, pos, keep_last_slab=True)
    has_prev = prev is not None
    row = _row_tile(d)
    tile = pl.BlockSpec((tm, d), lambda i, idx: (i, 0))
    in_specs = [pl.BlockSpec(memory_space=pl.ANY)]
    args = [local, yg]
    if has_prev:
        in_specs.append(pl.BlockSpec((2, tm) + row, lambda i, idx: (0, i, 0, 0)))
        args.append(prev.reshape((2, n_tok) + row))
    in_specs += [tile, pl.BlockSpec((tm, 2), lambda i, idx: (i, 0)),
                 pl.BlockSpec((1, 1, d), lambda i, idx: (i // per_batch, 0, 0)),
                 pl.BlockSpec(lng.shape, lambda i, idx: (0, 0)), pl.BlockSpec(lnb.shape, lambda i, idx: (0, 0))]
    args += [x1.reshape(n_tok, d), gates_col.reshape(n_tok, 2), g2, lng, lnb]
    out = pl.pallas_call(
        functools.partial(_combine_kernel, lo=lo, n_src=n_src, n_tok=n_tok, partial_src=has_prev,
                          has_prev=has_prev, alpha=alpha),
        grid_spec=pltpu.PrefetchScalarGridSpec(
            num_scalar_prefetch=1, grid=(n_tok // tm,),
            in_specs=in_specs,
            out_specs=tile,
            scratch_shapes=[pltpu.VMEM((n_src,) + row, yg.dtype), pltpu.VMEM((2, tm) + row, yg.dtype),
                            pltpu.SemaphoreType.DMA(())]),
        out_shape=jax.ShapeDtypeStruct((n_tok, d), F32),
        compiler_params=_params("arbitrary"),
        name="combine",
    )(*args)
    return out.reshape(bsz, seq, d)


def _dispatch_plan(ids, n_tok):
    tm = EXPERT_TILE
    e_pair = jnp.concatenate([ids[:, 0, :].reshape(n_tok), ids[:, 1, :].reshape(n_tok)])
    tok_pair = jnp.concatenate([jnp.arange(n_tok, dtype=jnp.int32)] * 2)
    onehot = (e_pair[:, None] == jnp.arange(N_EXPERTS, dtype=jnp.int32)[None, :]).astype(jnp.int32)
    csum = jnp.cumsum(onehot, axis=0)
    rank = jnp.sum(onehot * csum, axis=1) - 1
    counts = csum[-1]
    padded = ((counts + tm - 1) // tm) * tm
    ends = jnp.cumsum(padded)
    offs = ends - padded
    pos = jnp.sum(onehot * offs[None, :], axis=1) + rank
    rows = 2 * n_tok + N_EXPERTS * tm
    src_token = jnp.zeros((rows,), jnp.int32).at[pos].set(tok_pair)
    tile_start = jnp.arange(rows // tm, dtype=jnp.int32) * tm
    tile_expert = jnp.minimum(jnp.sum((tile_start[:, None] >= ends[None, :]).astype(jnp.int32), axis=1),
                              N_EXPERTS - 1)
    tile_active = (tile_start < ends[-1]).astype(jnp.int32)
    prev_expert = jnp.concatenate([jnp.full((1,), -1, jnp.int32), tile_expert[:-1]])
    tile_first = tile_active * (tile_expert != prev_expert).astype(jnp.int32)
    eid = jnp.arange(N_EXPERTS, dtype=jnp.int32)
    later = (padded > 0)[None, :] & (eid[None, :] > eid[:, None])
    next_expert = jnp.min(jnp.where(later, eid[None, :], N_EXPERTS), axis=1)
    next_expert = jnp.where(next_expert == N_EXPERTS, -1, next_expert).astype(jnp.int32)
    tile_next = next_expert[tile_expert]
    return pos.astype(jnp.int32), src_token, (tile_expert, tile_active, tile_first, tile_next)


def _rope_tables(positions, dim):
    rot = dim // ROPE_FRAC
    half = rot // 2
    inv_freq = ROPE_THETA ** (-jnp.arange(half, dtype=F32) * 2.0 / rot)
    ang = positions.astype(F32)[..., None] * inv_freq
    cos, sin = jnp.cos(ang), jnp.sin(ang)
    shape = ang.shape[:-1]
    one = jnp.ones(shape + (dim - rot,), F32)
    zero = jnp.zeros(shape + (dim - rot,), F32)
    zh = jnp.zeros(shape + (half,), F32)
    c = jnp.concatenate([cos, cos, one], axis=-1)
    s1 = jnp.concatenate([-sin, zh, zero], axis=-1)
    s2 = jnp.concatenate([zh, sin, zero], axis=-1)
    rep = LANES // dim
    return jnp.concatenate([jnp.tile(a, (1, 1, rep)) for a in (c, s1, s2)], axis=-1)


def kernel(x, c, positions, w_ada, b_ada, w_in, w_out, ssm_lam_re, ssm_lam_im, ssm_log_step, ssm_b_re, ssm_b_im, ssm_c_re, ssm_c_im, ssm_d, ssm_w_glu, ssm_b_glu, pool_w, pool_scale, ln1_g, ln1_b, ln2_g, ln2_b, w_router, e_gate, e_up, e_down):
    bsz, seq, d = x.shape
    depth = w_ada.shape[0]
    n_tok = bsz * seq
    alpha = (2.0 * depth) ** 0.25
    sw = ssm_d.shape[1]
    pw = pool_scale.shape[1]
    aw = w_out.shape[1] - sw - pw
    kw = KV_HEADS * HEAD_DIM
    iqw = IDX_HEADS * IDX_DIM
    widths = (aw, kw, iqw, sw, pw)
    top_k = min(MAX_TOPK, seq // 4)

    c_pad = jnp.concatenate([c, jnp.zeros((8 - bsz % 8, d), F32)], axis=0) if bsz % 8 else c
    ada = _ada_all(c_pad, w_ada, b_ada)
    rq = _rope_tables(positions, HEAD_DIM)
    ri = _rope_tables(positions, IDX_DIM)
    wr_pad = jnp.concatenate(
        [w_router.reshape(d, N_EXPERT_GROUPS, EXPERTS_PER_GROUP).transpose(0, 2, 1).reshape(d, N_EXPERTS),
         jnp.zeros((d, LANES - N_EXPERTS), F32)], axis=1)
    wr_hi = wr_pad.astype(BF16)
    wr_lo = (wr_pad - wr_hi.astype(F32)).astype(BF16)
    wr_pad = jnp.concatenate([wr_hi, wr_lo, wr_hi], axis=0)

    o_q, o_k, o_v, o_iq, o_ik, o_iw, o_us, o_up = (0, aw, aw + kw, aw + 2 * kw, aw + 2 * kw + iqw,
                                                   aw + 2 * kw + iqw + IDX_DIM,
                                                   aw + 2 * kw + iqw + IDX_DIM + IDX_HEADS,
                                                   aw + 2 * kw + iqw + IDX_DIM + IDX_HEADS + sw)
    for l in range(depth):
        sh1, sc1, g1, sh2, sc2, g2 = [ada[l, :bsz, j * d:(j + 1) * d].reshape(bsz, 1, d) for j in range(6)]
        wl = w_in[l]
        w_r = jnp.concatenate([wl[:, o_q:o_v], wl[:, o_iq:o_ik], wl[:, o_us:], wl[:, o_ik:o_iw],
                               jnp.zeros((d, LANES - IDX_DIM), F32)], axis=1).astype(BF16)
        wvt = wl[:, o_v:o_iq].T.astype(BF16)
        wiwt = wl[:, o_iw:o_us].T.astype(BF16)
        q, k, vt, iq, ik, iwt, us, up = _inproj(x, sc1, sh1, w_r, wvt, wiwt, rq, ri, widths)
        ya = _attention(q, k, vt, iq, ik, iwt, top_k)
        prep = _ssm_prepare(ssm_lam_re[l], ssm_lam_im[l], ssm_log_step[l], ssm_b_re[l], ssm_b_im[l],
                            ssm_c_re[l], ssm_c_im[l], ssm_d[l], SSM_CHUNK)
        ys = _ssm(us, prep, SSM_CHUNK)
        x1, u2, ids, gates = _mix(x, ya, ys, up, g1, sc2, sh2, ln1_g[l][None], ln1_b[l][None],
                                  w_out[l].astype(BF16), ssm_w_glu[l].astype(BF16), ssm_b_glu[l][None],
                                  pool_w[l].astype(BF16), pool_scale[l][None], wr_pad, alpha)
        pos, src_token, tile_plan = _dispatch_plan(ids, n_tok)
        xg = _gather_rows(u2.reshape((n_tok,) + _row_tile(d)), src_token)
        yg = _experts(xg, tile_plan, e_gate, e_up, e_down, l)
        x = _combine(yg, pos, x1, gates.transpose(0, 2, 1), g2, ln2_g[l][None], ln2_b[l][None], alpha)
    return x
```

```python
import functools
import math

import jax
import jax.numpy as jnp
import numpy as np
from jax import lax
from jax.experimental import pallas as pl
from jax.experimental.pallas import tpu as pltpu

F32 = jnp.float32
BF16 = jnp.bfloat16

HEAD_DIM = 128
KV_HEADS = 2
IDX_HEADS = 8
IDX_DIM = 64
MAX_TOPK = 256
SSM_GROUP_CH = 16
SSM_STATE = 64
POOL_WINDOWS = (2, 4, 8, 16)
ROPE_THETA = 500000.0
ROPE_FRAC = 4
LN_EPS = 1e-5
N_EXPERTS = 16
N_EXPERT_GROUPS = 4
EXPERTS_PER_GROUP = N_EXPERTS // N_EXPERT_GROUPS

LANES = 128
VMEM_LIMIT = 56 * 1024 * 1024
ROW_TILE = 256
COMBINE_TILE = 128
ATTN_Q_TILE = 256
ATTN_KEY_STEP = 512
BISECT_STEPS = 14
REDUCE_SLAB = 64
ROW_SUBLANES = 16
GATHER_SRC_ROWS = 10240
SSM_CHUNK = 8
EXPERT_TILE = 256
CONVERT_ROWS = 128
COPY_ROWS = 256
ADA_COLS = 1024


def _params(*sem):
    return pltpu.CompilerParams(dimension_semantics=sem, vmem_limit_bytes=VMEM_LIMIT)


def _sigmoid(x):
    return 1.0 / (1.0 + jnp.exp(-x))


def _ada_kernel(c_ref, w_ref, b_ref, o_ref):
    c = c_ref[...]
    cond = (c * _sigmoid(c)).astype(BF16)
    o_ref[0] = jnp.dot(cond, w_ref[0].astype(BF16), preferred_element_type=F32) + b_ref[0]


def _ada_all(c_pad, w_ada, b_ada):
    depth, d, n6 = w_ada.shape
    rows = c_pad.shape[0]
    return pl.pallas_call(
        _ada_kernel,
        grid=(depth, n6 // ADA_COLS),
        in_specs=[pl.BlockSpec((rows, d), lambda l, j: (0, 0)),
                  pl.BlockSpec((1, d, ADA_COLS), lambda l, j: (l, 0, j)),
                  pl.BlockSpec((1, 1, ADA_COLS), lambda l, j: (l, 0, j))],
        out_specs=pl.BlockSpec((1, rows, ADA_COLS), lambda l, j: (l, 0, j)),
        out_shape=jax.ShapeDtypeStruct((depth, rows, n6), F32),
        compiler_params=_params("arbitrary", "arbitrary"),
        name="ada",
    )(c_pad, w_ada, b_ada.reshape(depth, 1, n6))


def _rope(xv, tab, half):
    w = xv.shape[1]
    rep = w // LANES
    c = jnp.tile(tab[:, 0:LANES], (1, rep))
    s1 = jnp.tile(tab[:, LANES:2 * LANES], (1, rep))
    s2 = jnp.tile(tab[:, 2 * LANES:3 * LANES], (1, rep))
    return xv * c + pltpu.roll(xv, w - half, 1) * s1 + pltpu.roll(xv, half, 1) * s2


def _inproj_kernel(x_ref, sc_ref, sh_ref, w_ref, wvt_ref, wiwt_ref, rq_ref, ri_ref,
                   q_ref, k_ref, vt_ref, iq_ref, ik_ref, iwt_ref, us_ref, up_ref, *, widths, idx_scale):
    aw, kw, iqw, sw, pw = widths
    u = (x_ref[0] * (1.0 + sc_ref[0]) + sh_ref[0]).astype(BF16)
    rq = rq_ref[0]
    ri = ri_ref[0]

    def mm(lo, n):
        return jnp.dot(u, w_ref[:, lo:lo + n], preferred_element_type=F32)

    def mm_t(wt_ref):
        return lax.dot_general(wt_ref[...], u, (((1,), (1,)), ((), ())), preferred_element_type=F32)

    qhalf = HEAD_DIM // ROPE_FRAC // 2
    ihalf = IDX_DIM // ROPE_FRAC // 2
    o = 0
    q_ref[0] = (_rope(mm(o, aw), rq, qhalf) * (HEAD_DIM ** -0.5)).astype(BF16)
    o += aw
    k_ref[0] = _rope(mm(o, kw), rq, qhalf).astype(BF16)
    o += kw
    iq_ref[0] = _rope(mm(o, iqw), ri, ihalf).astype(BF16)
    o += iqw
    us_ref[0] = mm(o, sw)
    o += sw
    up_ref[0] = mm(o, pw)
    o += pw
    ik_ref[0] = _rope(mm(o, LANES), ri, ihalf)[:, 0:IDX_DIM].astype(BF16)
    vt_ref[0] = mm_t(wvt_ref).astype(BF16)
    iwt_ref[0] = mm_t(wiwt_ref) * idx_scale


def _inproj(x, sc, sh, w_r, wvt, wiwt, rq, ri, widths):
    bsz, seq, d = x.shape
    aw, kw, iqw, sw, pw = widths
    tm = min(ROW_TILE, seq)
    tile = lambda n: pl.BlockSpec((1, tm, n), lambda b, i: (b, i, 0))
    tile_t = lambda n: pl.BlockSpec((1, n, tm), lambda b, i: (b, 0, i))
    vec = pl.BlockSpec((1, 1, d), lambda b, i: (b, 0, 0))
    full = lambda a: pl.BlockSpec(a.shape, lambda b, i: (0, 0))
    sds = jax.ShapeDtypeStruct
    return pl.pallas_call(
        functools.partial(_inproj_kernel, widths=widths, idx_scale=(IDX_DIM ** -0.5) * (IDX_HEADS ** -0.5)),
        grid=(bsz, seq // tm),
        in_specs=[tile(d), vec, vec, full(w_r), full(wvt), full(wiwt), tile(3 * LANES), tile(3 * LANES)],
        out_specs=[tile(aw), tile(kw), tile_t(kw), tile(iqw), tile(IDX_DIM), tile_t(IDX_HEADS), tile(sw), tile(pw)],
        out_shape=[sds((bsz, seq, aw), BF16), sds((bsz, seq, kw), BF16), sds((bsz, kw, seq), BF16),
                   sds((bsz, seq, iqw), BF16), sds((bsz, seq, IDX_DIM), BF16), sds((bsz, IDX_HEADS, seq), F32),
                   sds((bsz, seq, sw), F32), sds((bsz, seq, pw), F32)],
        compiler_params=_params("parallel", "arbitrary"),
        name="inproj",
    )(x, sc, sh, w_r, wvt, wiwt, rq, ri)


def _col_reduce(op, v):
    rows, n = v.shape
    slab = op(v.reshape(rows // REDUCE_SLAB, REDUCE_SLAB, n), axis=0)
    return op(slab, axis=0, keepdims=True)


def _attn_kernel(*refs, q0, tq, keys, top_k, nq):
    i = pl.program_id(1)
    for j in range(nq):
        extent = keys - (nq - 1 - j) * tq
        pl.when(i == j)(functools.partial(_attn_tile, *refs, q0=q0, tq=tq, keys=extent, top_k=top_k))


def _attn_tile(q_ref, iq_ref, iwt_ref, k_ref, vt_ref, ik_ref, o_ref, mask_ref, x_ref, *, q0, tq, keys, top_k):
    i = pl.program_id(1)
    t = q0 + i * tq + lax.broadcasted_iota(jnp.int32, (1, tq), 1)
    kpos = lax.broadcasted_iota(jnp.int32, (keys, 1), 0)
    causal = kpos <= t

    iq = iq_ref[0]
    ik = ik_ref[0, 0:keys, :]
    iwt = iwt_ref[0]
    isc = jnp.zeros((keys, tq), F32)
    for h in range(IDX_HEADS):
        r = lax.dot_general(ik, iq[:, h * IDX_DIM:(h + 1) * IDX_DIM], (((1,), (1,)), ((), ())),
                            preferred_element_type=F32)
        isc = isc + jnp.maximum(r, 0.0) * iwt[h:h + 1, :]
    x = jnp.where(causal, isc, -jnp.inf)
    x_ref[0:keys, :] = x

    kf = float(top_k)
    n_valid = (t + 1).astype(F32)
    need = n_valid > kf

    def fold(step_fn, init):
        def body(s, acc):
            row = pl.multiple_of(s * REDUCE_SLAB, REDUCE_SLAB)
            return step_fn(acc, x_ref[pl.ds(row, REDUCE_SLAB), :])
        acc0 = jax.tree.map(lambda v: jnp.full((REDUCE_SLAB, tq), v, F32), init)
        return lax.fori_loop(0, keys // REDUCE_SLAB, body, acc0, unroll=4)

    def count_ge(theta):
        acc = fold(lambda a, xs: a + jnp.where(xs >= theta, 1.0, 0.0), 0.0)
        return jnp.sum(acc, axis=0, keepdims=True)

    rmax = _col_reduce(jnp.max, x)
    rmin = _col_reduce(jnp.min, jnp.where(causal, isc, jnp.inf))
    c_max = count_ge(rmax)
    top_tie = c_max >= kf
    lo0 = jnp.where(need, jnp.where(top_tie, rmax, rmin), -jnp.inf)
    clo0 = jnp.where(top_tie, c_max, n_valid)
    done0 = jnp.where(need & jnp.logical_not(top_tie) & (clo0 != kf), 0.0, 1.0)

    def bisect(lo, hi, clo, chi, done):
        mid = 0.5 * lo + 0.5 * hi
        c = count_ge(mid)
        live = done == 0.0
        ge = c >= kf
        up = live & ge
        dn = live & jnp.logical_not(ge)
        lo = jnp.where(up, mid, lo)
        clo = jnp.where(up, c, clo)
        hi = jnp.where(dn, mid, hi)
        chi = jnp.where(dn, c, chi)
        done = jnp.where(clo == kf, 1.0, done)
        return lo, hi, clo, chi, done

    def n_active(done):
        return jnp.sum(1.0 - done).astype(jnp.int32)

    def body2(s):
        _, lo, hi, clo, chi, done = s
        lo, hi, clo, chi, done = bisect(lo, hi, clo, chi, done)

        def min_max(acc, xs):
            inside = (xs >= lo) & (xs < hi)
            return (jnp.minimum(acc[0], jnp.where(inside, xs, jnp.inf)),
                    jnp.maximum(acc[1], jnp.where(inside, xs, -jnp.inf)))

        mins, maxs = fold(min_max, (jnp.inf, -jnp.inf))
        vmin = jnp.min(mins, axis=0, keepdims=True)
        vmax = jnp.max(maxs, axis=0, keepdims=True)
        c2 = count_ge(vmax)
        live = done == 0.0
        single = vmin == vmax
        top_ok = c2 >= kf
        take_top = live & jnp.logical_not(single) & top_ok
        drop_top = live & jnp.logical_not(single) & jnp.logical_not(top_ok)
        lo = jnp.where(live, jnp.where(take_top, vmax, vmin), lo)
        clo = jnp.where(take_top, c2, clo)
        hi = jnp.where(drop_top, vmax, hi)
        chi = jnp.where(drop_top, c2, chi)
        done = jnp.where(live & (single | top_ok), 1.0, done)
        return n_active(done), lo, hi, clo, chi, done

    state = lax.fori_loop(0, BISECT_STEPS, lambda _, s: bisect(*s), (lo0, rmax, clo0, c_max, done0))
    state = lax.while_loop(lambda s: s[0] > 0, body2, (n_active(state[4]),) + state)
    _, lo, _, clo, _, _ = state

    mask_ref[0:keys, :] = jnp.where(causal & (x_ref[0:keys, :] >= lo), 0.0, -1e30)

    tie_q = jnp.sum(jnp.where(need & (clo > kf), 1.0, 0.0)).astype(jnp.int32)

    @pl.when(tie_q > 0)
    def _():
        room = kf - _col_reduce(jnp.sum, jnp.where(x_ref[0:keys, :] > lo, 1.0, 0.0))
        step = 256
        rr = lax.broadcasted_iota(jnp.int32, (step, step), 0)
        cc = lax.broadcasted_iota(jnp.int32, (step, step), 1)
        before = jnp.where(cc < rr, 1.0, 0.0).astype(BF16)
        carry = jnp.zeros((1, tq), F32)
        for c0 in range(0, keys, step):
            xs = x_ref[c0:c0 + step, :]
            e = jnp.where(xs == lo, 1.0, 0.0)
            rank = carry + jnp.dot(before, e.astype(BF16), preferred_element_type=F32)
            keep = (xs > lo) | ((xs == lo) & (rank < room))
            mask_ref[c0:c0 + step, :] = jnp.where(keep & (xs > -jnp.inf), 0.0, -1e30)
            carry = carry + jnp.sum(e, axis=0, keepdims=True)

    bias = mask_ref[0:keys, :]
    q = q_ref[0]
    group = q.shape[1] // HEAD_DIM // KV_HEADS
    bias_g = jnp.concatenate([bias] * group, axis=1)
    for g in range(KV_HEADS):
        qg = jnp.concatenate([q[:, (g * group + j) * HEAD_DIM:(g * group + j + 1) * HEAD_DIM]
                              for j in range(group)], axis=0)
        kg = k_ref[0, 0:keys, g * HEAD_DIM:(g + 1) * HEAD_DIM]
        vtg = vt_ref[0, g * HEAD_DIM:(g + 1) * HEAD_DIM, 0:keys]
        s = lax.dot_general(kg, qg, (((1,), (1,)), ((), ())), preferred_element_type=F32) + bias_g
        m = _col_reduce(jnp.max, s)
        p = jnp.exp(s - m)
        den = _col_reduce(jnp.sum, p)
        og = jnp.dot(vtg, p.astype(BF16), preferred_element_type=F32) / den
        for j in range(group):
            h = g * group + j
            o_ref[0, :, h * HEAD_DIM:(h + 1) * HEAD_DIM] = og[:, j * tq:(j + 1) * tq].T.astype(BF16)


def _attention(q, k, vt, iq, ik, iwt, top_k):
    bsz, seq, aw = q.shape
    tq = min(ATTN_Q_TILE, seq)
    step = min(ATTN_KEY_STEP, seq)
    outs = []
    for q0 in range(0, seq, step):
        keys = q0 + step
        nq = step // tq
        qtile = lambda n, q0=q0: pl.BlockSpec((1, tq, n), lambda b, i: (b, q0 // tq + i, 0))
        ktile = lambda n, keys=keys: pl.BlockSpec((1, keys, n), lambda b, i: (b, 0, 0))
        outs.append(pl.pallas_call(
            functools.partial(_attn_kernel, q0=q0, tq=tq, keys=keys, top_k=top_k, nq=nq),
            grid=(bsz, nq),
            in_specs=[qtile(aw), qtile(iq.shape[2]),
                      pl.BlockSpec((1, iwt.shape[1], tq), lambda b, i, q0=q0: (b, 0, q0 // tq + i)),
                      ktile(k.shape[2]),
                      pl.BlockSpec((1, vt.shape[1], keys), lambda b, i: (b, 0, 0)),
                      ktile(ik.shape[2])],
            out_specs=pl.BlockSpec((1, tq, aw), lambda b, i: (b, i, 0)),
            out_shape=jax.ShapeDtypeStruct((bsz, step, aw), BF16),
            scratch_shapes=[pltpu.VMEM((keys, tq), F32), pltpu.VMEM((keys, tq), F32)],
            compiler_params=_params("parallel", "arbitrary"),
            name=f"attn_k{keys}",
        )(q, iq, iwt, k, vt, ik))
    return outs


def _ssm_kernel(u_ref, kin_ref, wre_ref, wim_ref, vre_ref, vim_ref, are_ref, aim_ref, y_ref,
                sre_ref, sim_ref, xre_ref, xim_ref, *, chunk):
    bsz, seq, lanes = u_ref.shape
    nc = seq // chunk
    rows = bsz * nc
    ns = wre_ref.shape[2]
    u = u_ref[...].reshape(rows, chunk, lanes).reshape(rows, chunk * lanes).astype(BF16)
    sre_ref[...] = jnp.dot(u, wre_ref[0], preferred_element_type=F32).reshape(bsz, nc, ns)
    sim_ref[...] = jnp.dot(u, wim_ref[0], preferred_element_type=F32).reshape(bsz, nc, ns)
    a_re = are_ref[0]
    a_im = aim_ref[0]
    x_re = jnp.zeros((bsz, ns), F32)
    x_im = jnp.zeros((bsz, ns), F32)
    for c in range(nc):
        xre_ref[:, c, :] = x_re
        xim_ref[:, c, :] = x_im
        n_re = a_re * x_re - a_im * x_im + sre_ref[:, c, :]
        n_im = a_re * x_im + a_im * x_re + sim_ref[:, c, :]
        x_re, x_im = n_re, n_im
    xr = xre_ref[...].reshape(rows, ns).astype(BF16)
    xi = xim_ref[...].reshape(rows, ns).astype(BF16)
    y = (jnp.dot(u, kin_ref[0], preferred_element_type=F32)
         + jnp.dot(xr, vre_ref[0], preferred_element_type=F32)
         + jnp.dot(xi, vim_ref[0], preferred_element_type=F32))
    y_ref[...] = y.reshape(rows, chunk, lanes).reshape(bsz, seq, lanes)


def _ssm_placement(gs, ch, p):
    a = np.arange(gs)[:, None, None]
    lane = np.arange(gs * ch)[None, None, :]
    on_lane = ((lane // ch) == a) & ((lane % ch) == np.arange(ch)[None, :, None])
    q = np.arange(gs * p)[None, None, :]
    on_state = ((q // p) == a) & ((q % p) == np.arange(p)[None, :, None])
    return jnp.asarray(on_lane, F32), jnp.asarray(on_state, F32)


def _ssm_prepare(lam_re, lam_im, log_step, b_re, b_im, c_re, c_im, d_skip, chunk):
    g, p = lam_re.shape
    ch = b_re.shape[2]
    hp = lax.Precision.HIGHEST
    step = jnp.exp(log_step)[:, None]
    lsr, lsi = lam_re * step, lam_im * step
    er = jnp.exp(lsr)
    nr, ni = er * jnp.cos(lsi) - 1.0, er * jnp.sin(lsi)
    den = lam_re * lam_re + lam_im * lam_im
    fr, fi = (nr * lam_re + ni * lam_im) / den, (ni * lam_re - nr * lam_im) / den
    bbr = fr[..., None] * b_re - fi[..., None] * b_im
    bbi = fr[..., None] * b_im + fi[..., None] * b_re
    n = jnp.arange(chunk + 1, dtype=F32)
    mag = jnp.exp(lsr[..., None] * n)
    pr, pi = mag * jnp.cos(lsi[..., None] * n), mag * jnp.sin(lsi[..., None] * n)
    prt, pit = pr[:, :, :chunk, None], pi[:, :, :chunk, None]
    wr = prt * bbr[:, :, None, :] - pit * bbi[:, :, None, :]
    wi = prt * bbi[:, :, None, :] + pit * bbr[:, :, None, :]
    ker = (jnp.einsum("gcp,gptd->gtcd", c_re, wr, precision=hp)
           - jnp.einsum("gcp,gptd->gtcd", c_im, wi, precision=hp))
    ker = ker.at[:, 0].add(jnp.eye(ch, dtype=F32) * d_skip.reshape(g, 1, ch))
    pr1, pi1 = pr[:, None, :, 1:], pi[:, None, :, 1:]
    vr = c_re[..., None] * pr1 - c_im[..., None] * pi1
    vi = c_re[..., None] * pi1 + c_im[..., None] * pr1
    gs = LANES // ch
    ns = g // gs
    lanes = chunk * LANES
    on_lane, on_state = _ssm_placement(gs, ch, p)
    bf16_exact = lambda m: m.astype(BF16).astype(F32)
    by_tile = lambda m: m.reshape(ns, gs, *m.shape[1:])
    lag_tile = jnp.einsum("satcd,acl->stadl", by_tile(bf16_exact(ker)), on_lane).reshape(ns, chunk, LANES, LANES)
    lag = jnp.arange(chunk)[None, :] - jnp.arange(chunk)[:, None]
    tiles = jnp.where((lag >= 0)[None, :, :, None, None], lag_tile[:, jnp.clip(lag, 0, chunk - 1)], 0.0)
    kin8 = tiles.transpose(0, 1, 3, 2, 4).reshape(ns, lanes, lanes).astype(BF16)

    def to_in(w):
        wide = jnp.einsum("sapid,apq->siadq", by_tile(bf16_exact(w[:, :, ::-1, :])), on_state)
        return wide.reshape(ns, lanes, gs * p).astype(BF16)

    def to_out(v):
        wide = jnp.einsum("sacpj,acl->sapjl", by_tile(bf16_exact(v)), on_lane)
        return wide.reshape(ns, gs * p, lanes).astype(BF16)

    a_re = pr[..., chunk].reshape(ns, 1, gs * p)
    a_im = pi[..., chunk].reshape(ns, 1, gs * p)
    return kin8, to_in(wr), to_in(wi), to_out(vr), to_out(-vi), a_re, a_im


def _ssm(us, prep, chunk):
    bsz, seq, w = us.shape
    kin, wre, wim, vre, vim, a_re, a_im = prep
    ns, lanes, states = wre.shape
    nc = seq // chunk
    act = pl.BlockSpec((bsz, seq, LANES), lambda i: (0, 0, i))
    mat = lambda a, b: pl.BlockSpec((1, a, b), lambda i: (i, 0, 0))
    return pl.pallas_call(
        functools.partial(_ssm_kernel, chunk=chunk),
        grid=(ns,),
        in_specs=[act, mat(lanes, lanes), mat(lanes, states), mat(lanes, states), mat(states, lanes),
                  mat(states, lanes), mat(1, states), mat(1, states)],
        out_specs=act,
        out_shape=jax.ShapeDtypeStruct((bsz, seq, w), F32),
        scratch_shapes=[pltpu.VMEM((bsz, nc, states), F32)] * 4,
        compiler_params=_params("parallel"),
        name="ssm",
    )(us, kin, wre, wim, vre, vim, a_re, a_im)


def _row_tile(d):
    return (ROW_SUBLANES, d // ROW_SUBLANES)


def _to_rows(v):
    return v.reshape((v.shape[0],) + _row_tile(v.shape[1]))


def _from_rows(v):
    return v.reshape(v.shape[0], v.shape[1] * v.shape[2])


def _layer_norm(h, g, b):
    mu = jnp.mean(h, axis=1, keepdims=True)
    hc = h - mu
    var = jnp.mean(hc * hc, axis=1, keepdims=True)
    return hc * lax.rsqrt(var + LN_EPS) * g + b


def _route(logits_t):
    ng, per = N_EXPERT_GROUPS, EXPERTS_PER_GROUP
    m = jnp.max(logits_t, axis=0, keepdims=True)
    e = jnp.exp(logits_t - m)
    prob = e / jnp.sum(e, axis=0, keepdims=True)
    v = [prob[j * ng:(j + 1) * ng, :] for j in range(per)]

    def top(vals):
        best = functools.reduce(jnp.maximum, vals)
        idx = jnp.full(best.shape, per - 1, jnp.int32)
        for j in range(per - 2, -1, -1):
            idx = jnp.where(vals[j] == best, j, idx)
        return best, idx

    m1, i1 = top(v)
    m2, i2 = top([jnp.where(i1 == j, -1.0, v[j]) for j in range(per)])
    score = m1 + m2
    gid = lax.broadcasted_iota(jnp.int32, score.shape, 0)
    best_g = jnp.min(jnp.where(score == jnp.max(score, axis=0, keepdims=True), gid, ng), axis=0, keepdims=True)
    sel = gid == best_g
    pick_f = lambda a: jnp.sum(jnp.where(sel, a, 0.0), axis=0, keepdims=True)
    pick_i = lambda a: jnp.sum(jnp.where(sel, a, 0), axis=0, keepdims=True)
    p1, p2 = pick_f(m1), pick_f(m2)
    e1, e2 = best_g * per + pick_i(i1), best_g * per + pick_i(i2)
    den = p1 + p2
    return jnp.concatenate([e1, e2], axis=0), jnp.concatenate([p1 / den, p2 / den], axis=0)


def _mix_kernel(x_ref, ys_ref, upc_ref, upp_ref, g1_ref, sc2_ref, sh2_ref, lng_ref, lnb_ref,
                wout_ref, wglu_ref, bglu_ref, wpool_ref, pscale_ref, wr_ref, *rest, alpha, tiles_per_variant):
    ya_refs, (x1_ref, u2_ref, ids_ref, gates_ref) = rest[:-4], rest[-4:]
    i = pl.program_id(1)
    tm = x_ref.shape[1]
    owner = i // tiles_per_variant
    ya = ya_refs[0][0]
    for k in range(1, len(ya_refs)):
        ya = jnp.where(owner == k, ya_refs[k][0], ya)
    y = ys_ref[0]
    y = 0.5 * y * (1.0 + jnp.tanh(math.sqrt(2.0 / math.pi) * (y + 0.044715 * (y * y * y))))
    z = jnp.dot(y.astype(BF16), wglu_ref[...], preferred_element_type=F32) + bglu_ref[...]
    y = y * _sigmoid(z)
    upc = upc_ref[0]
    upp = jnp.where(i > 0, upp_ref[0], 0.0)
    cat = jnp.concatenate([upp, upc], axis=0).astype(BF16)
    r = lax.broadcasted_iota(jnp.int32, (tm, 1), 0)
    lagm = (r + tm) - lax.broadcasted_iota(jnp.int32, (1, 2 * tm), 1)
    tpos = (i * tm + r + 1).astype(F32)
    gc = upc.shape[1] // len(POOL_WINDOWS)
    pooled = []
    for g, win in enumerate(POOL_WINDOWS):
        band = jnp.where((lagm >= 0) & (lagm < win), 1.0, 0.0).astype(BF16)
        ws = jnp.dot(band, cat[:, g * gc:(g + 1) * gc], preferred_element_type=F32)
        pg = ws / jnp.minimum(tpos, float(win)) - upc[:, g * gc:(g + 1) * gc]
        pooled.append(jnp.dot(pg.astype(BF16), wpool_ref[g], preferred_element_type=F32))
    yp = jnp.concatenate(pooled, axis=1) * pscale_ref[...]
    mixed = jnp.concatenate([ya, y.astype(BF16), yp.astype(BF16)], axis=1)
    mix = jnp.dot(mixed, wout_ref[...], preferred_element_type=F32)
    x1 = _layer_norm(alpha * x_ref[0] + (1.0 + g1_ref[0]) * mix, lng_ref[...], lnb_ref[...])
    x1_ref[0] = x1
    u2 = x1 * (1.0 + sc2_ref[0]) + sh2_ref[0]
    u2_ref[0] = _to_rows(u2.astype(BF16))
    u_hi = u2.astype(BF16)
    u_lo = (u2 - u_hi.astype(F32)).astype(BF16)
    logits = jnp.dot(jnp.concatenate([u_hi, u_hi, u_lo], axis=1), wr_ref[...], preferred_element_type=F32)
    ids, gates = _route(logits.T[0:N_EXPERTS, :])
    ids_ref[0] = ids
    gates_ref[0] = gates


def _mix(x, yas, ys, up, g1, sc2, sh2, lng, lnb, wout, wglu, bglu, wpool, pscale, wr_pad, alpha):
    bsz, seq, d = x.shape
    tm = min(ROW_TILE, seq)
    per = yas[0].shape[1] // tm
    att = [pl.BlockSpec((1, tm, ya.shape[2]), lambda b, i, k=k: (b, jnp.clip(i - k * per, 0, per - 1), 0))
           for k, ya in enumerate(yas)]
    tile = lambda n: pl.BlockSpec((1, tm, n), lambda b, i: (b, i, 0))
    prev = pl.BlockSpec((1, tm, up.shape[2]), lambda b, i: (b, jnp.maximum(i - 1, 0), 0))
    vec = pl.BlockSpec((1, 1, d), lambda b, i: (b, 0, 0))
    full = lambda a: pl.BlockSpec(a.shape, lambda b, i: (0,) * a.ndim)
    lane_rows = pl.BlockSpec((1, 2, tm), lambda b, i: (b, 0, i))
    return pl.pallas_call(
        functools.partial(_mix_kernel, alpha=alpha, tiles_per_variant=per),
        grid=(bsz, seq // tm),
        in_specs=[tile(d), tile(ys.shape[2]), tile(up.shape[2]), prev, vec, vec, vec,
                  full(lng), full(lnb), full(wout), full(wglu), full(bglu), full(wpool), full(pscale), full(wr_pad)]
        + att,
        out_specs=[tile(d), pl.BlockSpec((1, tm) + _row_tile(d), lambda b, i: (b, i, 0, 0)), lane_rows, lane_rows],
        out_shape=[jax.ShapeDtypeStruct((bsz, seq, d), F32), jax.ShapeDtypeStruct((bsz, seq) + _row_tile(d), BF16),
                   jax.ShapeDtypeStruct((bsz, 2, seq), jnp.int32), jax.ShapeDtypeStruct((bsz, 2, seq), F32)],
        compiler_params=_params("parallel", "arbitrary"),
        name="mix",
    )(x, ys, up, up, g1, sc2, sh2, lng, lnb, wout, wglu, bglu, wpool, pscale, wr_pad, *yas)


def _gather_kernel(idx_ref, src_ref, *rest, lo, n_src, partial_src, has_prev):
    if has_prev:
        prev_ref, out_ref, buf, sem = rest
    else:
        out_ref, buf, sem = rest

    @pl.when(pl.program_id(0) == 0)
    def _():
        cp = pltpu.make_async_copy(src_ref.at[pl.ds(lo, n_src)], buf, sem)
        cp.start()
        cp.wait()

    base = pl.program_id(0) * COPY_ROWS

    def move(r, carry):
        s = idx_ref[base + r]
        row = buf[jnp.maximum(s, 0)] if partial_src else buf[s]
        if has_prev:
            row = jnp.where(s >= 0, row, prev_ref[r])
        out_ref[r] = row
        return carry

    lax.fori_loop(0, COPY_ROWS, move, 0, unroll=8)


def _slab_plan(src, idx):
    n_chunks = pl.cdiv(src.shape[0], GATHER_SRC_ROWS)
    n_src = src.shape[0] // n_chunks
    assert n_src * n_chunks == src.shape[0]
    if n_chunks == 1:
        return n_src, [(0, idx)]
    local = [idx - c * n_src for c in range(n_chunks)]
    return n_src, [(c * n_src, jnp.where((loc >= 0) & (loc < n_src), loc, -1)) for c, loc in enumerate(local)]


def _gather_rows(src, idx, keep_last_slab=False):
    n = idx.shape[0]
    n_src, plan = _slab_plan(src, idx)
    n_chunks = len(plan)
    block = pl.BlockSpec((COPY_ROWS,) + src.shape[1:], lambda i, idx: (i, 0, 0))
    out = None
    for c, (lo, local) in enumerate(plan[:-1] if keep_last_slab else plan):
        has_prev = c > 0
        out = pl.pallas_call(
            functools.partial(_gather_kernel, lo=lo, n_src=n_src, partial_src=n_chunks > 1, has_prev=has_prev),
            grid_spec=pltpu.PrefetchScalarGridSpec(
                num_scalar_prefetch=1, grid=(n // COPY_ROWS,),
                in_specs=[pl.BlockSpec(memory_space=pl.ANY)] + ([block] if has_prev else []),
                out_specs=block,
                scratch_shapes=[pltpu.VMEM((n_src,) + src.shape[1:], src.dtype), pltpu.SemaphoreType.DMA(())]),
            out_shape=jax.ShapeDtypeStruct((n,) + src.shape[1:], src.dtype),
            compiler_params=_params("arbitrary"),
            name="gather",
        )(*((local, src, out) if has_prev else (local, src)))
    return (out, n_src, plan[-1]) if keep_last_slab else out


def _expert_kernel(te_ref, act_ref, first_ref, nxt_ref, x_ref, wg_hbm, wu_hbm, wd_hbm, y_ref,
                   stage_g, stage_u, stage_d, wg_ref, wu_ref, wd_ref, sems, *, layer):
    i = pl.program_id(0)

    def fetch(e):
        return (pltpu.make_async_copy(wg_hbm.at[layer, e], stage_g, sems.at[0]),
                pltpu.make_async_copy(wu_hbm.at[layer, e], stage_u, sems.at[1]),
                pltpu.make_async_copy(wd_hbm.at[layer, e], stage_d, sems.at[2]))

    @pl.when(i == 0)
    def _():
        for cp in fetch(te_ref[0]):
            cp.start()

    @pl.when(first_ref[i] != 0)
    def _():
        for cp in fetch(te_ref[i]):
            cp.wait()
        for stage, work in ((stage_g, wg_ref), (stage_u, wu_ref), (stage_d, wd_ref)):
            rows = stage.shape[0]

            def convert(r, carry, stage=stage, work=work):
                sl = pl.ds(pl.multiple_of(r * CONVERT_ROWS, CONVERT_ROWS), CONVERT_ROWS)
                work[sl, :] = stage[sl, :].astype(BF16)
                return carry

            lax.fori_loop(0, rows // CONVERT_ROWS, convert, 0)

        @pl.when(nxt_ref[i] >= 0)
        def _():
            for cp in fetch(nxt_ref[i]):
                cp.start()

    @pl.when(act_ref[i] != 0)
    def _():
        x = _from_rows(x_ref[...])
        g = jnp.dot(x, wg_ref[...], preferred_element_type=F32)
        u = jnp.dot(x, wu_ref[...], preferred_element_type=F32)
        h = (g * _sigmoid(g)) * u
        y_ref[...] = _to_rows(jnp.dot(h.astype(BF16), wd_ref[...], preferred_element_type=F32).astype(BF16))

    @pl.when(act_ref[i] == 0)
    def _():
        y_ref[...] = jnp.zeros(y_ref.shape, y_ref.dtype)


def _experts(xg, plan, wg, wu, wd, layer):
    rows = xg.shape[0]
    d, f = wg.shape[2], wg.shape[3]
    tm = EXPERT_TILE
    tile = pl.BlockSpec((tm,) + _row_tile(d), lambda i, *_: (i, 0, 0))
    hbm = pl.BlockSpec(memory_space=pl.ANY)
    return pl.pallas_call(
        functools.partial(_expert_kernel, layer=layer),
        grid_spec=pltpu.PrefetchScalarGridSpec(
            num_scalar_prefetch=4, grid=(rows // tm,),
            in_specs=[tile, hbm, hbm, hbm],
            out_specs=tile,
            scratch_shapes=[pltpu.VMEM((d, f), F32), pltpu.VMEM((d, f), F32), pltpu.VMEM((f, d), F32),
                            pltpu.VMEM((d, f), BF16), pltpu.VMEM((d, f), BF16), pltpu.VMEM((f, d), BF16),
                            pltpu.SemaphoreType.DMA((3,))]),
        out_shape=jax.ShapeDtypeStruct((rows,) + _row_tile(d), BF16),
        compiler_params=_params("arbitrary"),
        name="experts",
    )(*plan, xg, wg, wu, wd)


def _combine_kernel(idx_ref, src_ref, *rest, lo, n_src, n_tok, partial_src, has_prev, alpha):
    if has_prev:
        prev_ref, x_ref, gc_ref, g2_ref, lng_ref, lnb_ref, o_ref, buf, rows, sem = rest
    else:
        x_ref, gc_ref, g2_ref, lng_ref, lnb_ref, o_ref, buf, rows, sem = rest

    @pl.when(pl.program_id(0) == 0)
    def _():
        cp = pltpu.make_async_copy(src_ref.at[pl.ds(lo, n_src)], buf, sem)
        cp.start()
        cp.wait()

    tm = x_ref.shape[0]
    base = pl.program_id(0) * tm
    for k in range(2):
        def move(r, carry, k=k):
            s = idx_ref[k * n_tok + base + r]
            row = buf[jnp.maximum(s, 0)] if partial_src else buf[s]
            if has_prev:
                row = jnp.where(s >= 0, row, prev_ref[k, r])
            rows[k, r] = row
            return carry

        lax.fori_loop(0, tm, move, 0, unroll=8)
    gc = gc_ref[...]
    ffn = gc[:, 0:1] * _from_rows(rows[0]).astype(F32) + gc[:, 1:2] * _from_rows(rows[1]).astype(F32)
    o_ref[...] = _layer_norm(alpha * x_ref[...] + (1.0 + g2_ref[0]) * ffn, lng_ref[...], lnb_ref[...])


def _combine(yg, pos, x1, gates_col, g2, lng, lnb, alpha):
    bsz, seq, d = x1.shape
    n_tok = bsz * seq
    tm = COMBINE_TILE
    per_batch = seq // tm
    prev, n_src, (lo, local) = _gather_rows(yg, pos, keep_last_slab=True)
    has_prev = prev is not None
    row = _row_tile(d)
    tile = pl.BlockSpec((tm, d), lambda i, idx: (i, 0))
    in_specs = [pl.BlockSpec(memory_space=pl.ANY)]
    args = [local, yg]
    if has_prev:
        in_specs.append(pl.BlockSpec((2, tm) + row, lambda i, idx: (0, i, 0, 0)))
        args.append(prev.reshape((2, n_tok) + row))
    in_specs += [tile, pl.BlockSpec((tm, 2), lambda i, idx: (i, 0)),
                 pl.BlockSpec((1, 1, d), lambda i, idx: (i // per_batch, 0, 0)),
                 pl.BlockSpec(lng.shape, lambda i, idx: (0, 0)), pl.BlockSpec(lnb.shape, lambda i, idx: (0, 0))]
    args += [x1.reshape(n_tok, d), gates_col.reshape(n_tok, 2), g2, lng, lnb]
    out = pl.pallas_call(
        functools.partial(_combine_kernel, lo=lo, n_src=n_src, n_tok=n_tok, partial_src=has_prev,
                          has_prev=has_prev, alpha=alpha),
        grid_spec=pltpu.PrefetchScalarGridSpec(
            num_scalar_prefetch=1, grid=(n_tok // tm,),
            in_specs=in_specs,
            out_specs=tile,
            scratch_shapes=[pltpu.VMEM((n_src,) + row, yg.dtype), pltpu.VMEM((2, tm) + row, yg.dtype),
                            pltpu.SemaphoreType.DMA(())]),
        out_shape=jax.ShapeDtypeStruct((n_tok, d), F32),
        compiler_params=_params("arbitrary"),
        name="combine",
    )(*args)
    return out.reshape(bsz, seq, d)


def _dispatch_plan(ids, n_tok):
    tm = EXPERT_TILE
    e_pair = jnp.concatenate([ids[:, 0, :].reshape(n_tok), ids[:, 1, :].reshape(n_tok)])
    tok_pair = jnp.concatenate([jnp.arange(n_tok, dtype=jnp.int32)] * 2)
    onehot = (e_pair[:, None] == jnp.arange(N_EXPERTS, dtype=jnp.int32)[None, :]).astype(jnp.int32)
    csum = jnp.cumsum(onehot, axis=0)
    rank = jnp.sum(onehot * csum, axis=1) - 1
    counts = csum[-1]
    padded = ((counts + tm - 1) // tm) * tm
    ends = jnp.cumsum(padded)
    offs = ends - padded
    pos = jnp.sum(onehot * offs[None, :], axis=1) + rank
    rows = 2 * n_tok + N_EXPERTS * tm
    src_token = jnp.zeros((rows,), jnp.int32).at[pos].set(tok_pair)
    tile_start = jnp.arange(rows // tm, dtype=jnp.int32) * tm
    tile_expert = jnp.minimum(jnp.sum((tile_start[:, None] >= ends[None, :]).astype(jnp.int32), axis=1),
                              N_EXPERTS - 1)
    tile_active = (tile_start < ends[-1]).astype(jnp.int32)
    prev_expert = jnp.concatenate([jnp.full((1,), -1, jnp.int32), tile_expert[:-1]])
    tile_first = tile_active * (tile_expert != prev_expert).astype(jnp.int32)
    eid = jnp.arange(N_EXPERTS, dtype=jnp.int32)
    later = (padded > 0)[None, :] & (eid[None, :] > eid[:, None])
    next_expert = jnp.min(jnp.where(later, eid[None, :], N_EXPERTS), axis=1)
    next_expert = jnp.where(next_expert == N_EXPERTS, -1, next_expert).astype(jnp.int32)
    tile_next = next_expert[tile_expert]
    return pos.astype(jnp.int32), src_token, (tile_expert, tile_active, tile_first, tile_next)


def _rope_tables(positions, dim):
    rot = dim // ROPE_FRAC
    half = rot // 2
    inv_freq = ROPE_THETA ** (-jnp.arange(half, dtype=F32) * 2.0 / rot)
    ang = positions.astype(F32)[..., None] * inv_freq
    cos, sin = jnp.cos(ang), jnp.sin(ang)
    shape = ang.shape[:-1]
    one = jnp.ones(shape + (dim - rot,), F32)
    zero = jnp.zeros(shape + (dim - rot,), F32)
    zh = jnp.zeros(shape + (half,), F32)
    c = jnp.concatenate([cos, cos, one], axis=-1)
    s1 = jnp.concatenate([-sin, zh, zero], axis=-1)
    s2 = jnp.concatenate([zh, sin, zero], axis=-1)
    rep = LANES // dim
    return jnp.concatenate([jnp.tile(a, (1, 1, rep)) for a in (c, s1, s2)], axis=-1)


def kernel(x, c, positions, w_ada, b_ada, w_in, w_out, ssm_lam_re, ssm_lam_im, ssm_log_step, ssm_b_re, ssm_b_im, ssm_c_re, ssm_c_im, ssm_d, ssm_w_glu, ssm_b_glu, pool_w, pool_scale, ln1_g, ln1_b, ln2_g, ln2_b, w_router, e_gate, e_up, e_down):
    bsz, seq, d = x.shape
    depth = w_ada.shape[0]
    n_tok = bsz * seq
    alpha = (2.0 * depth) ** 0.25
    sw = ssm_d.shape[1]
    pw = pool_scale.shape[1]
    aw = w_out.shape[1] - sw - pw
    kw = KV_HEADS * HEAD_DIM
    iqw = IDX_HEADS * IDX_DIM
    widths = (aw, kw, iqw, sw, pw)
    top_k = min(MAX_TOPK, seq // 4)

    c_pad = jnp.concatenate([c, jnp.zeros((8 - bsz % 8, d), F32)], axis=0) if bsz % 8 else c
    ada = _ada_all(c_pad, w_ada, b_ada)
    rq = _rope_tables(positions, HEAD_DIM)
    ri = _rope_tables(positions, IDX_DIM)
    wr_pad = jnp.concatenate(
        [w_router.reshape(d, N_EXPERT_GROUPS, EXPERTS_PER_GROUP).transpose(0, 2, 1).reshape(d, N_EXPERTS),
         jnp.zeros((d, LANES - N_EXPERTS), F32)], axis=1)
    wr_hi = wr_pad.astype(BF16)
    wr_lo = (wr_pad - wr_hi.astype(F32)).astype(BF16)
    wr_pad = jnp.concatenate([wr_hi, wr_lo, wr_hi], axis=0)

    o_q, o_k, o_v, o_iq, o_ik, o_iw, o_us, o_up = (0, aw, aw + kw, aw + 2 * kw, aw + 2 * kw + iqw,
                                                   aw + 2 * kw + iqw + IDX_DIM,
                                                   aw + 2 * kw + iqw + IDX_DIM + IDX_HEADS,
                                                   aw + 2 * kw + iqw + IDX_DIM + IDX_HEADS + sw)
    for l in range(depth):
        sh1, sc1, g1, sh2, sc2, g2 = [ada[l, :bsz, j * d:(j + 1) * d].reshape(bsz, 1, d) for j in range(6)]
        wl = w_in[l]
        w_r = jnp.concatenate([wl[:, o_q:o_v], wl[:, o_iq:o_ik], wl[:, o_us:], wl[:, o_ik:o_iw],
                               jnp.zeros((d, LANES - IDX_DIM), F32)], axis=1).astype(BF16)
        wvt = wl[:, o_v:o_iq].T.astype(BF16)
        wiwt = wl[:, o_iw:o_us].T.astype(BF16)
        q, k, vt, iq, ik, iwt, us, up = _inproj(x, sc1, sh1, w_r, wvt, wiwt, rq, ri, widths)
        ya = _attention(q, k, vt, iq, ik, iwt, top_k)
        prep = _ssm_prepare(ssm_lam_re[l], ssm_lam_im[l], ssm_log_step[l], ssm_b_re[l], ssm_b_im[l],
                            ssm_c_re[l], ssm_c_im[l], ssm_d[l], SSM_CHUNK)
        ys = _ssm(us, prep, SSM_CHUNK)
        x1, u2, ids, gates = _mix(x, ya, ys, up, g1, sc2, sh2, ln1_g[l][None], ln1_b[l][None],
                                  w_out[l].astype(BF16), ssm_w_glu[l].astype(BF16), ssm_b_glu[l][None],
                                  pool_w[l].astype(BF16), pool_scale[l][None], wr_pad, alpha)
        pos, src_token, tile_plan = _dispatch_plan(ids, n_tok)
        xg = _gather_rows(u2.reshape((n_tok,) + _row_tile(d)), src_token)
        yg = _experts(xg, tile_plan, e_gate, e_up, e_down, l)
        x = _combine(yg, pos, x1, gates.transpose(0, 2, 1), g2, ln2_g[l][None], ln2_b[l][None], alpha)
    return x
```

```python
import functools
import math

import jax
import jax.numpy as jnp
import numpy as np
from jax import lax
from jax.experimental import pallas as pl
from jax.experimental.pallas import tpu as pltpu

F32 = jnp.float32
BF16 = jnp.bfloat16

HEAD_DIM = 128
KV_HEADS = 2
IDX_HEADS = 8
IDX_DIM = 64
MAX_TOPK = 256
SSM_GROUP_CH = 16
SSM_STATE = 64
POOL_WINDOWS = (2, 4, 8, 16)
ROPE_THETA = 500000.0
ROPE_FRAC = 4
LN_EPS = 1e-5
N_EXPERTS = 16
N_EXPERT_GROUPS = 4
EXPERTS_PER_GROUP = N_EXPERTS // N_EXPERT_GROUPS

LANES = 128
VMEM_LIMIT = 56 * 1024 * 1024
ROW_TILE = 256
COMBINE_TILE = 128
ATTN_Q_TILE = 256
ATTN_KEY_STEP = 512
BISECT_STEPS = 14
REDUCE_SLAB = 64
ROW_SUBLANES = 16
GATHER_SRC_ROWS = 10240
SSM_CHUNK = 8
EXPERT_TILE = 256
CONVERT_ROWS = 128
COPY_ROWS = 256
ADA_COLS = 1024


def _params(*sem):
    return pltpu.CompilerParams(dimension_semantics=sem, vmem_limit_bytes=VMEM_LIMIT)


def _sigmoid(x):
    return 1.0 / (1.0 + jnp.exp(-x))


def _ada_kernel(c_ref, w_ref, b_ref, o_ref):
    c = c_ref[...]
    cond = (c * _sigmoid(c)).astype(BF16)
    o_ref[0] = jnp.dot(cond, w_ref[0].astype(BF16), preferred_element_type=F32) + b_ref[0]


def _ada_all(c_pad, w_ada, b_ada):
    depth, d, n6 = w_ada.shape
    rows = c_pad.shape[0]
    return pl.pallas_call(
        _ada_kernel,
        grid=(depth, n6 // ADA_COLS),
        in_specs=[pl.BlockSpec((rows, d), lambda l, j: (0, 0)),
                  pl.BlockSpec((1, d, ADA_COLS), lambda l, j: (l, 0, j)),
                  pl.BlockSpec((1, 1, ADA_COLS), lambda l, j: (l, 0, j))],
        out_specs=pl.BlockSpec((1, rows, ADA_COLS), lambda l, j: (l, 0, j)),
        out_shape=jax.ShapeDtypeStruct((depth, rows, n6), F32),
        compiler_params=_params("arbitrary", "arbitrary"),
        name="ada",
    )(c_pad, w_ada, b_ada.reshape(depth, 1, n6))


def _rope(xv, tab, half):
    w = xv.shape[1]
    rep = w // LANES
    c = jnp.tile(tab[:, 0:LANES], (1, rep))
    s1 = jnp.tile(tab[:, LANES:2 * LANES], (1, rep))
    s2 = jnp.tile(tab[:, 2 * LANES:3 * LANES], (1, rep))
    return xv * c + pltpu.roll(xv, w - half, 1) * s1 + pltpu.roll(xv, half, 1) * s2


def _inproj_kernel(x_ref, sc_ref, sh_ref, w_ref, wvt_ref, wiwt_ref, rq_ref, ri_ref,
                   q_ref, k_ref, vt_ref, iq_ref, ik_ref, iwt_ref, us_ref, up_ref, *, widths, idx_scale):
    aw, kw, iqw, sw, pw = widths
    u = (x_ref[0] * (1.0 + sc_ref[0]) + sh_ref[0]).astype(BF16)
    rq = rq_ref[0]
    ri = ri_ref[0]

    def mm(lo, n):
        return jnp.dot(u, w_ref[:, lo:lo + n], preferred_element_type=F32)

    def mm_t(wt_ref):
        return lax.dot_general(wt_ref[...], u, (((1,), (1,)), ((), ())), preferred_element_type=F32)

    qhalf = HEAD_DIM // ROPE_FRAC // 2
    ihalf = IDX_DIM // ROPE_FRAC // 2
    o = 0
    q_ref[0] = (_rope(mm(o, aw), rq, qhalf) * (HEAD_DIM ** -0.5 * math.log2(math.e))).astype(BF16)
    o += aw
    k_ref[0] = _rope(mm(o, kw), rq, qhalf).astype(BF16)
    o += kw
    iq_ref[0] = _rope(mm(o, iqw), ri, ihalf).astype(BF16)
    o += iqw
    us_ref[0] = mm(o, sw)
    o += sw
    up_ref[0] = mm(o, pw)
    o += pw
    ik_ref[0] = _rope(mm(o, LANES), ri, ihalf)[:, 0:IDX_DIM].astype(BF16)
    vt_ref[0] = mm_t(wvt_ref).astype(BF16)
    iwt_ref[0] = mm_t(wiwt_ref) * idx_scale


def _inproj(x, sc, sh, w_r, wvt, wiwt, rq, ri, widths):
    bsz, seq, d = x.shape
    aw, kw, iqw, sw, pw = widths
    tm = min(ROW_TILE, seq)
    tile = lambda n: pl.BlockSpec((1, tm, n), lambda b, i: (b, i, 0))
    tile_t = lambda n: pl.BlockSpec((1, n, tm), lambda b, i: (b, 0, i))
    vec = pl.BlockSpec((1, 1, d), lambda b, i: (b, 0, 0))
    full = lambda a: pl.BlockSpec(a.shape, lambda b, i: (0, 0))
    sds = jax.ShapeDtypeStruct
    return pl.pallas_call(
        functools.partial(_inproj_kernel, widths=widths, idx_scale=(IDX_DIM ** -0.5) * (IDX_HEADS ** -0.5)),
        grid=(bsz, seq // tm),
        in_specs=[tile(d), vec, vec, full(w_r), full(wvt), full(wiwt), tile(3 * LANES), tile(3 * LANES)],
        out_specs=[tile(aw), tile(kw), tile_t(kw), tile(iqw), tile(IDX_DIM), tile_t(IDX_HEADS), tile(sw), tile(pw)],
        out_shape=[sds((bsz, seq, aw), BF16), sds((bsz, seq, kw), BF16), sds((bsz, kw, seq), BF16),
                   sds((bsz, seq, iqw), BF16), sds((bsz, seq, IDX_DIM), BF16), sds((bsz, IDX_HEADS, seq), F32),
                   sds((bsz, seq, sw), F32), sds((bsz, seq, pw), F32)],
        compiler_params=_params("parallel", "arbitrary"),
        name="inproj",
    )(x, sc, sh, w_r, wvt, wiwt, rq, ri)


def _col_reduce(op, v):
    rows, n = v.shape
    slab = op(v.reshape(rows // REDUCE_SLAB, REDUCE_SLAB, n), axis=0)
    return op(slab, axis=0, keepdims=True)


def _attn_kernel(*refs, q0, tq, keys, top_k, nq):
    i = pl.program_id(1)
    for j in range(nq):
        extent = keys - (nq - 1 - j) * tq
        pl.when(i == j)(functools.partial(_attn_tile, *refs, q0=q0, tq=tq, keys=extent, top_k=top_k))


def _attn_tile(q_ref, iq_ref, iwt_ref, k_ref, vt_ref, ik_ref, o_ref, mask_ref, x_ref, *, q0, tq, keys, top_k):
    i = pl.program_id(1)
    t = q0 + i * tq + lax.broadcasted_iota(jnp.int32, (1, tq), 1)
    kpos = lax.broadcasted_iota(jnp.int32, (keys, 1), 0)
    causal = kpos <= t

    iq = iq_ref[0]
    ik = ik_ref[0, 0:keys, :]
    iwt = iwt_ref[0]
    isc = jnp.zeros((keys, tq), F32)
    for h in range(IDX_HEADS):
        r = lax.dot_general(ik, iq[:, h * IDX_DIM:(h + 1) * IDX_DIM], (((1,), (1,)), ((), ())),
                            preferred_element_type=F32)
        isc = isc + jnp.maximum(r, 0.0) * iwt[h:h + 1, :]
    x = jnp.where(causal, isc, -jnp.inf)
    x_ref[0:keys, :] = x

    kf = float(top_k)
    n_valid = (t + 1).astype(F32)
    need = n_valid > kf

    def fold(step_fn, init):
        def body(s, acc):
            row = pl.multiple_of(s * REDUCE_SLAB, REDUCE_SLAB)
            return step_fn(acc, x_ref[pl.ds(row, REDUCE_SLAB), :])
        acc0 = jax.tree.map(lambda v: jnp.full((REDUCE_SLAB, tq), v, F32), init)
        return lax.fori_loop(0, keys // REDUCE_SLAB, body, acc0, unroll=4)

    def count_ge(theta):
        acc = fold(lambda a, xs: a + jnp.where(xs >= theta, 1.0, 0.0), 0.0)
        return jnp.sum(acc, axis=0, keepdims=True)

    rmax = _col_reduce(jnp.max, x)
    rmin = _col_reduce(jnp.min, jnp.where(causal, isc, jnp.inf))
    c_max = count_ge(rmax)
    top_tie = c_max >= kf
    lo0 = jnp.where(need, jnp.where(top_tie, rmax, rmin), -jnp.inf)
    clo0 = jnp.where(top_tie, c_max, n_valid)
    done0 = jnp.where(need & jnp.logical_not(top_tie) & (clo0 != kf), 0.0, 1.0)

    def bisect(lo, hi, clo, chi, done):
        mid = 0.5 * lo + 0.5 * hi
        c = count_ge(mid)
        live = done == 0.0
        ge = c >= kf
        up = live & ge
        dn = live & jnp.logical_not(ge)
        lo = jnp.where(up, mid, lo)
        clo = jnp.where(up, c, clo)
        hi = jnp.where(dn, mid, hi)
        chi = jnp.where(dn, c, chi)
        done = jnp.where(clo == kf, 1.0, done)
        return lo, hi, clo, chi, done

    def n_active(done):
        return jnp.sum(1.0 - done).astype(jnp.int32)

    def body2(s):
        _, lo, hi, clo, chi, done = s
        lo, hi, clo, chi, done = bisect(lo, hi, clo, chi, done)

        def min_max(acc, xs):
            inside = (xs >= lo) & (xs < hi)
            return (jnp.minimum(acc[0], jnp.where(inside, xs, jnp.inf)),
                    jnp.maximum(acc[1], jnp.where(inside, xs, -jnp.inf)))

        mins, maxs = fold(min_max, (jnp.inf, -jnp.inf))
        vmin = jnp.min(mins, axis=0, keepdims=True)
        vmax = jnp.max(maxs, axis=0, keepdims=True)
        c2 = count_ge(vmax)
        live = done == 0.0
        single = vmin == vmax
        top_ok = c2 >= kf
        take_top = live & jnp.logical_not(single) & top_ok
        drop_top = live & jnp.logical_not(single) & jnp.logical_not(top_ok)
        lo = jnp.where(live, jnp.where(take_top, vmax, vmin), lo)
        clo = jnp.where(take_top, c2, clo)
        hi = jnp.where(drop_top, vmax, hi)
        chi = jnp.where(drop_top, c2, chi)
        done = jnp.where(live & (single | top_ok), 1.0, done)
        return n_active(done), lo, hi, clo, chi, done

    state = lax.fori_loop(0, BISECT_STEPS, lambda _, s: bisect(*s), (lo0, rmax, clo0, c_max, done0))
    state = lax.while_loop(lambda s: s[0] > 0, body2, (n_active(state[4]),) + state)
    _, lo, _, clo, _, _ = state

    mask_ref[0:keys, :] = jnp.where(causal & (x_ref[0:keys, :] >= lo), 0.0, -1e30)

    tie_q = jnp.sum(jnp.where(need & (clo > kf), 1.0, 0.0)).astype(jnp.int32)

    @pl.when(tie_q > 0)
    def _():
        room = kf - _col_reduce(jnp.sum, jnp.where(x_ref[0:keys, :] > lo, 1.0, 0.0))
        step = 256
        rr = lax.broadcasted_iota(jnp.int32, (step, step), 0)
        cc = lax.broadcasted_iota(jnp.int32, (step, step), 1)
        before = jnp.where(cc < rr, 1.0, 0.0).astype(BF16)
        carry = jnp.zeros((1, tq), F32)
        for c0 in range(0, keys, step):
            xs = x_ref[c0:c0 + step, :]
            e = jnp.where(xs == lo, 1.0, 0.0)
            rank = carry + jnp.dot(before, e.astype(BF16), preferred_element_type=F32)
            keep = (xs > lo) | ((xs == lo) & (rank < room))
            mask_ref[c0:c0 + step, :] = jnp.where(keep & (xs > -jnp.inf), 0.0, -1e30)
            carry = carry + jnp.sum(e, axis=0, keepdims=True)

    bias = mask_ref[0:keys, :]
    q = q_ref[0]
    group = q.shape[1] // HEAD_DIM // KV_HEADS
    bias_g = jnp.concatenate([bias] * group, axis=1)
    for g in range(KV_HEADS):
        qg = jnp.concatenate([q[:, (g * group + j) * HEAD_DIM:(g * group + j + 1) * HEAD_DIM]
                              for j in range(group)], axis=0)
        kg = k_ref[0, 0:keys, g * HEAD_DIM:(g + 1) * HEAD_DIM]
        vtg = vt_ref[0, g * HEAD_DIM:(g + 1) * HEAD_DIM, 0:keys]
        s = lax.dot_general(kg, qg, (((1,), (1,)), ((), ())), preferred_element_type=F32) + bias_g
        m = _col_reduce(jnp.max, s)
        p = jnp.exp2(s - m)
        den = _col_reduce(jnp.sum, p)
        og = jnp.dot(vtg, p.astype(BF16), preferred_element_type=F32) / den
        for j in range(group):
            h = g * group + j
            o_ref[0, :, h * HEAD_DIM:(h + 1) * HEAD_DIM] = og[:, j * tq:(j + 1) * tq].T.astype(BF16)


def _attention(q, k, vt, iq, ik, iwt, top_k):
    bsz, seq, aw = q.shape
    tq = min(ATTN_Q_TILE, seq)
    step = min(ATTN_KEY_STEP, seq)
    outs = []
    for q0 in range(0, seq, step):
        keys = q0 + step
        nq = step // tq
        qtile = lambda n, q0=q0: pl.BlockSpec((1, tq, n), lambda b, i: (b, q0 // tq + i, 0))
        ktile = lambda n, keys=keys: pl.BlockSpec((1, keys, n), lambda b, i: (b, 0, 0))
        outs.append(pl.pallas_call(
            functools.partial(_attn_kernel, q0=q0, tq=tq, keys=keys, top_k=top_k, nq=nq),
            grid=(bsz, nq),
            in_specs=[qtile(aw), qtile(iq.shape[2]),
                      pl.BlockSpec((1, iwt.shape[1], tq), lambda b, i, q0=q0: (b, 0, q0 // tq + i)),
                      ktile(k.shape[2]),
                      pl.BlockSpec((1, vt.shape[1], keys), lambda b, i: (b, 0, 0)),
                      ktile(ik.shape[2])],
            out_specs=pl.BlockSpec((1, tq, aw), lambda b, i: (b, i, 0)),
            out_shape=jax.ShapeDtypeStruct((bsz, step, aw), BF16),
            scratch_shapes=[pltpu.VMEM((keys, tq), F32), pltpu.VMEM((keys, tq), F32)],
            compiler_params=_params("parallel", "arbitrary"),
            name=f"attn_k{keys}",
        )(q, iq, iwt, k, vt, ik))
    return outs


def _ssm_kernel(u_ref, kin_ref, wre_ref, wim_ref, vre_ref, vim_ref, are_ref, aim_ref, y_ref,
                sre_ref, sim_ref, xre_ref, xim_ref, *, chunk):
    bsz, seq, lanes = u_ref.shape
    nc = seq // chunk
    rows = bsz * nc
    ns = wre_ref.shape[2]
    u = u_ref[...].reshape(rows, chunk, lanes).reshape(rows, chunk * lanes).astype(BF16)
    sre_ref[...] = jnp.dot(u, wre_ref[0], preferred_element_type=F32).reshape(bsz, nc, ns)
    sim_ref[...] = jnp.dot(u, wim_ref[0], preferred_element_type=F32).reshape(bsz, nc, ns)
    a_re = are_ref[0]
    a_im = aim_ref[0]
    x_re = jnp.zeros((bsz, ns), F32)
    x_im = jnp.zeros((bsz, ns), F32)
    for c in range(nc):
        xre_ref[:, c, :] = x_re
        xim_ref[:, c, :] = x_im
        n_re = a_re * x_re - a_im * x_im + sre_ref[:, c, :]
        n_im = a_re * x_im + a_im * x_re + sim_ref[:, c, :]
        x_re, x_im = n_re, n_im
    xr = xre_ref[...].reshape(rows, ns).astype(BF16)
    xi = xim_ref[...].reshape(rows, ns).astype(BF16)
    y = (jnp.dot(u, kin_ref[0], preferred_element_type=F32)
         + jnp.dot(xr, vre_ref[0], preferred_element_type=F32)
         + jnp.dot(xi, vim_ref[0], preferred_element_type=F32))
    y_ref[...] = y.reshape(rows, chunk, lanes).reshape(bsz, seq, lanes)


def _ssm_placement(gs, ch, p):
    a = np.arange(gs)[:, None, None]
    lane = np.arange(gs * ch)[None, None, :]
    on_lane = ((lane // ch) == a) & ((lane % ch) == np.arange(ch)[None, :, None])
    q = np.arange(gs * p)[None, None, :]
    on_state = ((q // p) == a) & ((q % p) == np.arange(p)[None, :, None])
    return jnp.asarray(on_lane, F32), jnp.asarray(on_state, F32)


def _ssm_prepare(lam_re, lam_im, log_step, b_re, b_im, c_re, c_im, d_skip, chunk):
    g, p = lam_re.shape
    ch = b_re.shape[2]
    hp = lax.Precision.HIGHEST
    step = jnp.exp(log_step)[:, None]
    lsr, lsi = lam_re * step, lam_im * step
    er = jnp.exp(lsr)
    nr, ni = er * jnp.cos(lsi) - 1.0, er * jnp.sin(lsi)
    den = lam_re * lam_re + lam_im * lam_im
    fr, fi = (nr * lam_re + ni * lam_im) / den, (ni * lam_re - nr * lam_im) / den
    bbr = fr[..., None] * b_re - fi[..., None] * b_im
    bbi = fr[..., None] * b_im + fi[..., None] * b_re
    n = jnp.arange(chunk + 1, dtype=F32)
    mag = jnp.exp(lsr[..., None] * n)
    pr, pi = mag * jnp.cos(lsi[..., None] * n), mag * jnp.sin(lsi[..., None] * n)
    prt, pit = pr[:, :, :chunk, None], pi[:, :, :chunk, None]
    wr = prt * bbr[:, :, None, :] - pit * bbi[:, :, None, :]
    wi = prt * bbi[:, :, None, :] + pit * bbr[:, :, None, :]
    ker = (jnp.einsum("gcp,gptd->gtcd", c_re, wr, precision=hp)
           - jnp.einsum("gcp,gptd->gtcd", c_im, wi, precision=hp))
    ker = ker.at[:, 0].add(jnp.eye(ch, dtype=F32) * d_skip.reshape(g, 1, ch))
    pr1, pi1 = pr[:, None, :, 1:], pi[:, None, :, 1:]
    vr = c_re[..., None] * pr1 - c_im[..., None] * pi1
    vi = c_re[..., None] * pi1 + c_im[..., None] * pr1
    gs = LANES // ch
    ns = g // gs
    lanes = chunk * LANES
    on_lane, on_state = _ssm_placement(gs, ch, p)
    bf16_exact = lambda m: m.astype(BF16).astype(F32)
    by_tile = lambda m: m.reshape(ns, gs, *m.shape[1:])
    lag_tile = jnp.einsum("satcd,acl->stadl", by_tile(bf16_exact(ker)), on_lane).reshape(ns, chunk, LANES, LANES)
    lag = jnp.arange(chunk)[None, :] - jnp.arange(chunk)[:, None]
    tiles = jnp.where((lag >= 0)[None, :, :, None, None], lag_tile[:, jnp.clip(lag, 0, chunk - 1)], 0.0)
    kin8 = tiles.transpose(0, 1, 3, 2, 4).reshape(ns, lanes, lanes).astype(BF16)

    def to_in(w):
        wide = jnp.einsum("sapid,apq->siadq", by_tile(bf16_exact(w[:, :, ::-1, :])), on_state)
        return wide.reshape(ns, lanes, gs * p).astype(BF16)

    def to_out(v):
        wide = jnp.einsum("sacpj,acl->sapjl", by_tile(bf16_exact(v)), on_lane)
        return wide.reshape(ns, gs * p, lanes).astype(BF16)

    a_re = pr[..., chunk].reshape(ns, 1, gs * p)
    a_im = pi[..., chunk].reshape(ns, 1, gs * p)
    return kin8, to_in(wr), to_in(wi), to_out(vr), to_out(-vi), a_re, a_im


def _ssm(us, prep, chunk):
    bsz, seq, w = us.shape
    kin, wre, wim, vre, vim, a_re, a_im = prep
    ns, lanes, states = wre.shape
    nc = seq // chunk
    act = pl.BlockSpec((bsz, seq, LANES), lambda i: (0, 0, i))
    mat = lambda a, b: pl.BlockSpec((1, a, b), lambda i: (i, 0, 0))
    return pl.pallas_call(
        functools.partial(_ssm_kernel, chunk=chunk),
        grid=(ns,),
        in_specs=[act, mat(lanes, lanes), mat(lanes, states), mat(lanes, states), mat(states, lanes),
                  mat(states, lanes), mat(1, states), mat(1, states)],
        out_specs=act,
        out_shape=jax.ShapeDtypeStruct((bsz, seq, w), F32),
        scratch_shapes=[pltpu.VMEM((bsz, nc, states), F32)] * 4,
        compiler_params=_params("parallel"),
        name="ssm",
    )(us, kin, wre, wim, vre, vim, a_re, a_im)


def _row_tile(d):
    return (ROW_SUBLANES, d // ROW_SUBLANES)


def _to_rows(v):
    return v.reshape((v.shape[0],) + _row_tile(v.shape[1]))


def _from_rows(v):
    return v.reshape(v.shape[0], v.shape[1] * v.shape[2])


def _layer_norm(h, g, b):
    mu = jnp.mean(h, axis=1, keepdims=True)
    hc = h - mu
    var = jnp.mean(hc * hc, axis=1, keepdims=True)
    return hc * lax.rsqrt(var + LN_EPS) * g + b


def _route(logits_t):
    ng, per = N_EXPERT_GROUPS, EXPERTS_PER_GROUP
    m = jnp.max(logits_t, axis=0, keepdims=True)
    e = jnp.exp(logits_t - m)
    prob = e / jnp.sum(e, axis=0, keepdims=True)
    v = [prob[j * ng:(j + 1) * ng, :] for j in range(per)]

    def top(vals):
        best = functools.reduce(jnp.maximum, vals)
        idx = jnp.full(best.shape, per - 1, jnp.int32)
        for j in range(per - 2, -1, -1):
            idx = jnp.where(vals[j] == best, j, idx)
        return best, idx

    m1, i1 = top(v)
    m2, i2 = top([jnp.where(i1 == j, -1.0, v[j]) for j in range(per)])
    score = m1 + m2
    gid = lax.broadcasted_iota(jnp.int32, score.shape, 0)
    best_g = jnp.min(jnp.where(score == jnp.max(score, axis=0, keepdims=True), gid, ng), axis=0, keepdims=True)
    sel = gid == best_g
    pick_f = lambda a: jnp.sum(jnp.where(sel, a, 0.0), axis=0, keepdims=True)
    pick_i = lambda a: jnp.sum(jnp.where(sel, a, 0), axis=0, keepdims=True)
    p1, p2 = pick_f(m1), pick_f(m2)
    e1, e2 = best_g * per + pick_i(i1), best_g * per + pick_i(i2)
    den = p1 + p2
    return jnp.concatenate([e1, e2], axis=0), jnp.concatenate([p1 / den, p2 / den], axis=0)


def _mix_kernel(x_ref, ys_ref, upc_ref, upp_ref, g1_ref, sc2_ref, sh2_ref, lng_ref, lnb_ref,
                wout_ref, wglu_ref, bglu_ref, wpool_ref, pscale_ref, wr_ref, *rest, alpha, tiles_per_variant):
    ya_refs, (x1_ref, u2_ref, ids_ref, gates_ref) = rest[:-4], rest[-4:]
    i = pl.program_id(1)
    tm = x_ref.shape[1]
    owner = i // tiles_per_variant
    ya = ya_refs[0][0]
    for k in range(1, len(ya_refs)):
        ya = jnp.where(owner == k, ya_refs[k][0], ya)
    y = ys_ref[0]
    y = 0.5 * y * (1.0 + jnp.tanh(math.sqrt(2.0 / math.pi) * (y + 0.044715 * (y * y * y))))
    z = jnp.dot(y.astype(BF16), wglu_ref[...], preferred_element_type=F32) + bglu_ref[...]
    y = y * _sigmoid(z)
    upc = upc_ref[0]
    upp = jnp.where(i > 0, upp_ref[0], 0.0)
    cat = jnp.concatenate([upp, upc], axis=0).astype(BF16)
    r = lax.broadcasted_iota(jnp.int32, (tm, 1), 0)
    lagm = (r + tm) - lax.broadcasted_iota(jnp.int32, (1, 2 * tm), 1)
    tpos = (i * tm + r + 1).astype(F32)
    gc = upc.shape[1] // len(POOL_WINDOWS)
    pooled = []
    for g, win in enumerate(POOL_WINDOWS):
        band = jnp.where((lagm >= 0) & (lagm < win), 1.0, 0.0).astype(BF16)
        ws = jnp.dot(band, cat[:, g * gc:(g + 1) * gc], preferred_element_type=F32)
        pg = ws / jnp.minimum(tpos, float(win)) - upc[:, g * gc:(g + 1) * gc]
        pooled.append(jnp.dot(pg.astype(BF16), wpool_ref[g], preferred_element_type=F32))
    yp = jnp.concatenate(pooled, axis=1) * pscale_ref[...]
    mixed = jnp.concatenate([ya, y.astype(BF16), yp.astype(BF16)], axis=1)
    mix = jnp.dot(mixed, wout_ref[...], preferred_element_type=F32)
    x1 = _layer_norm(alpha * x_ref[0] + (1.0 + g1_ref[0]) * mix, lng_ref[...], lnb_ref[...])
    x1_ref[0] = x1
    u2 = x1 * (1.0 + sc2_ref[0]) + sh2_ref[0]
    u2_ref[0] = _to_rows(u2.astype(BF16))
    u_hi = u2.astype(BF16)
    u_lo = (u2 - u_hi.astype(F32)).astype(BF16)
    logits = jnp.dot(jnp.concatenate([u_hi, u_hi, u_lo], axis=1), wr_ref[...], preferred_element_type=F32)
    ids, gates = _route(logits.T[0:N_EXPERTS, :])
    ids_ref[0] = ids
    gates_ref[0] = gates


def _mix(x, yas, ys, up, g1, sc2, sh2, lng, lnb, wout, wglu, bglu, wpool, pscale, wr_pad, alpha):
    bsz, seq, d = x.shape
    tm = min(ROW_TILE, seq)
    per = yas[0].shape[1] // tm
    att = [pl.BlockSpec((1, tm, ya.shape[2]), lambda b, i, k=k: (b, jnp.clip(i - k * per, 0, per - 1), 0))
           for k, ya in enumerate(yas)]
    tile = lambda n: pl.BlockSpec((1, tm, n), lambda b, i: (b, i, 0))
    prev = pl.BlockSpec((1, tm, up.shape[2]), lambda b, i: (b, jnp.maximum(i - 1, 0), 0))
    vec = pl.BlockSpec((1, 1, d), lambda b, i: (b, 0, 0))
    full = lambda a: pl.BlockSpec(a.shape, lambda b, i: (0,) * a.ndim)
    lane_rows = pl.BlockSpec((1, 2, tm), lambda b, i: (b, 0, i))
    return pl.pallas_call(
        functools.partial(_mix_kernel, alpha=alpha, tiles_per_variant=per),
        grid=(bsz, seq // tm),
        in_specs=[tile(d), tile(ys.shape[2]), tile(up.shape[2]), prev, vec, vec, vec,
                  full(lng), full(lnb), full(wout), full(wglu), full(bglu), full(wpool), full(pscale), full(wr_pad)]
        + att,
        out_specs=[tile(d), pl.BlockSpec((1, tm) + _row_tile(d), lambda b, i: (b, i, 0, 0)), lane_rows, lane_rows],
        out_shape=[jax.ShapeDtypeStruct((bsz, seq, d), F32), jax.ShapeDtypeStruct((bsz, seq) + _row_tile(d), BF16),
                   jax.ShapeDtypeStruct((bsz, 2, seq), jnp.int32), jax.ShapeDtypeStruct((bsz, 2, seq), F32)],
        compiler_params=_params("parallel", "arbitrary"),
        name="mix",
    )(x, ys, up, up, g1, sc2, sh2, lng, lnb, wout, wglu, bglu, wpool, pscale, wr_pad, *yas)


def _gather_kernel(idx_ref, src_ref, *rest, lo, n_src, partial_src, has_prev):
    if has_prev:
        prev_ref, out_ref, buf, sem = rest
    else:
        out_ref, buf, sem = rest

    @pl.when(pl.program_id(0) == 0)
    def _():
        cp = pltpu.make_async_copy(src_ref.at[pl.ds(lo, n_src)], buf, sem)
        cp.start()
        cp.wait()

    base = pl.program_id(0) * COPY_ROWS

    def move(r, carry):
        s = idx_ref[base + r]
        row = buf[jnp.maximum(s, 0)] if partial_src else buf[s]
        if has_prev:
            row = jnp.where(s >= 0, row, prev_ref[r])
        out_ref[r] = row
        return carry

    lax.fori_loop(0, COPY_ROWS, move, 0, unroll=8)


def _slab_plan(src, idx):
    n_chunks = pl.cdiv(src.shape[0], GATHER_SRC_ROWS)
    n_src = src.shape[0] // n_chunks
    assert n_src * n_chunks == src.shape[0]
    if n_chunks == 1:
        return n_src, [(0, idx)]
    local = [idx - c * n_src for c in range(n_chunks)]
    return n_src, [(c * n_src, jnp.where((loc >= 0) & (loc < n_src), loc, -1)) for c, loc in enumerate(local)]


def _gather_rows(src, idx, keep_last_slab=False):
    n = idx.shape[0]
    n_src, plan = _slab_plan(src, idx)
    n_chunks = len(plan)
    block = pl.BlockSpec((COPY_ROWS,) + src.shape[1:], lambda i, idx: (i, 0, 0))
    out = None
    for c, (lo, local) in enumerate(plan[:-1] if keep_last_slab else plan):
        has_prev = c > 0
        out = pl.pallas_call(
            functools.partial(_gather_kernel, lo=lo, n_src=n_src, partial_src=n_chunks > 1, has_prev=has_prev),
            grid_spec=pltpu.PrefetchScalarGridSpec(
                num_scalar_prefetch=1, grid=(n // COPY_ROWS,),
                in_specs=[pl.BlockSpec(memory_space=pl.ANY)] + ([block] if has_prev else []),
                out_specs=block,
                scratch_shapes=[pltpu.VMEM((n_src,) + src.shape[1:], src.dtype), pltpu.SemaphoreType.DMA(())]),
            out_shape=jax.ShapeDtypeStruct((n,) + src.shape[1:], src.dtype),
            compiler_params=_params("arbitrary"),
            name="gather",
        )(*((local, src, out) if has_prev else (local, src)))
    return (out, n_src, plan[-1]) if keep_last_slab else out


def _expert_kernel(te_ref, act_ref, first_ref, nxt_ref, x_ref, wg_hbm, wu_hbm, wd_hbm, y_ref,
                   stage_g, stage_u, stage_d, wg_ref, wu_ref, wd_ref, sems, *, layer):
    i = pl.program_id(0)

    def fetch(e):
        return (pltpu.make_async_copy(wg_hbm.at[layer, e], stage_g, sems.at[0]),
                pltpu.make_async_copy(wu_hbm.at[layer, e], stage_u, sems.at[1]),
                pltpu.make_async_copy(wd_hbm.at[layer, e], stage_d, sems.at[2]))

    @pl.when(i == 0)
    def _():
        for cp in fetch(te_ref[0]):
            cp.start()

    @pl.when(first_ref[i] != 0)
    def _():
        for cp in fetch(te_ref[i]):
            cp.wait()
        for stage, work in ((stage_g, wg_ref), (stage_u, wu_ref), (stage_d, wd_ref)):
            rows = stage.shape[0]

            def convert(r, carry, stage=stage, work=work):
                sl = pl.ds(pl.multiple_of(r * CONVERT_ROWS, CONVERT_ROWS), CONVERT_ROWS)
                work[sl, :] = stage[sl, :].astype(BF16)
                return carry

            lax.fori_loop(0, rows // CONVERT_ROWS, convert, 0)

        @pl.when(nxt_ref[i] >= 0)
        def _():
            for cp in fetch(nxt_ref[i]):
                cp.start()

    @pl.when(act_ref[i] != 0)
    def _():
        x = _from_rows(x_ref[...])
        g = jnp.dot(x, wg_ref[...], preferred_element_type=F32)
        u = jnp.dot(x, wu_ref[...], preferred_element_type=F32)
        h = (g * _sigmoid(g)) * u
        y_ref[...] = _to_rows(jnp.dot(h.astype(BF16), wd_ref[...], preferred_element_type=F32).astype(BF16))

    @pl.when(act_ref[i] == 0)
    def _():
        y_ref[...] = jnp.zeros(y_ref.shape, y_ref.dtype)


def _experts(xg, plan, wg, wu, wd, layer):
    rows = xg.shape[0]
    d, f = wg.shape[2], wg.shape[3]
    tm = EXPERT_TILE
    tile = pl.BlockSpec((tm,) + _row_tile(d), lambda i, *_: (i, 0, 0))
    hbm = pl.BlockSpec(memory_space=pl.ANY)
    return pl.pallas_call(
        functools.partial(_expert_kernel, layer=layer),
        grid_spec=pltpu.PrefetchScalarGridSpec(
            num_scalar_prefetch=4, grid=(rows // tm,),
            in_specs=[tile, hbm, hbm, hbm],
            out_specs=tile,
            scratch_shapes=[pltpu.VMEM((d, f), F32), pltpu.VMEM((d, f), F32), pltpu.VMEM((f, d), F32),
                            pltpu.VMEM((d, f), BF16), pltpu.VMEM((d, f), BF16), pltpu.VMEM((f, d), BF16),
                            pltpu.SemaphoreType.DMA((3,))]),
        out_shape=jax.ShapeDtypeStruct((rows,) + _row_tile(d), BF16),
        compiler_params=_params("arbitrary"),
        name="experts",
    )(*plan, xg, wg, wu, wd)


def _combine_kernel(idx_ref, src_ref, *rest, lo, n_src, n_tok, partial_src, has_prev, alpha):
    if has_prev:
        prev_ref, x_ref, gc_ref, g2_ref, lng_ref, lnb_ref, o_ref, buf, rows, sem = rest
    else:
        x_ref, gc_ref, g2_ref, lng_ref, lnb_ref, o_ref, buf, rows, sem = rest

    @pl.when(pl.program_id(0) == 0)
    def _():
        cp = pltpu.make_async_copy(src_ref.at[pl.ds(lo, n_src)], buf, sem)
        cp.start()
        cp.wait()

    tm = x_ref.shape[0]
    base = pl.program_id(0) * tm
    for k in range(2):
        def move(r, carry, k=k):
            s = idx_ref[k * n_tok + base + r]
            row = buf[jnp.maximum(s, 0)] if partial_src else buf[s]
            if has_prev:
                row = jnp.where(s >= 0, row, prev_ref[k, r])
            rows[k, r] = row
            return carry

        lax.fori_loop(0, tm, move, 0, unroll=8)
    gc = gc_ref[...]
    ffn = gc[:, 0:1] * _from_rows(rows[0]).astype(F32) + gc[:, 1:2] * _from_rows(rows[1]).astype(F32)
    o_ref[...] = _layer_norm(alpha * x_ref[...] + (1.0 + g2_ref[0]) * ffn, lng_ref[...], lnb_ref[...])


def _combine(yg, pos, x1, gates_col, g2, lng, lnb, alpha):
    bsz, seq, d = x1.shape
    n_tok = bsz * seq
    tm = COMBINE_TILE
    per_batch = seq // tm
    prev, n_src, (lo, local) = _gather_rows(yg, pos, keep_last_slab=True)
    has_prev = prev is not None
    row = _row_tile(d)
    tile = pl.BlockSpec((tm, d), lambda i, idx: (i, 0))
    in_specs = [pl.BlockSpec(memory_space=pl.ANY)]
    args = [local, yg]
    if has_prev:
        in_specs.append(pl.BlockSpec((2, tm) + row, lambda i, idx: (0, i, 0, 0)))
        args.append(prev.reshape((2, n_tok) + row))
    in_specs += [tile, pl.BlockSpec((tm, 2), lambda i, idx: (i, 0)),
                 pl.BlockSpec((1, 1, d), lambda i, idx: (i // per_batch, 0, 0)),
                 pl.BlockSpec(lng.shape, lambda i, idx: (0, 0)), pl.BlockSpec(lnb.shape, lambda i, idx: (0, 0))]
    args += [x1.reshape(n_tok, d), gates_col.reshape(n_tok, 2), g2, lng, lnb]
    out = pl.pallas_call(
        functools.partial(_combine_kernel, lo=lo, n_src=n_src, n_tok=n_tok, partial_src=has_prev,
                          has_prev=has_prev, alpha=alpha),
        grid_spec=pltpu.PrefetchScalarGridSpec(
            num_scalar_prefetch=1, grid=(n_tok // tm,),
            in_specs=in_specs,
            out_specs=tile,
            scratch_shapes=[pltpu.VMEM((n_src,) + row, yg.dtype), pltpu.VMEM((2, tm) + row, yg.dtype),
                            pltpu.SemaphoreType.DMA(())]),
        out_shape=jax.ShapeDtypeStruct((n_tok, d), F32),
        compiler_params=_params("arbitrary"),
        name="combine",
    )(*args)
    return out.reshape(bsz, seq, d)


def _dispatch_plan(ids, n_tok):
    tm = EXPERT_TILE
    e_pair = jnp.concatenate([ids[:, 0, :].reshape(n_tok), ids[:, 1, :].reshape(n_tok)])
    tok_pair = jnp.concatenate([jnp.arange(n_tok, dtype=jnp.int32)] * 2)
    onehot = (e_pair[:, None] == jnp.arange(N_EXPERTS, dtype=jnp.int32)[None, :]).astype(jnp.int32)
    csum = jnp.cumsum(onehot, axis=0)
    rank = jnp.sum(onehot * csum, axis=1) - 1
    counts = csum[-1]
    padded = ((counts + tm - 1) // tm) * tm
    ends = jnp.cumsum(padded)
    offs = ends - padded
    pos = jnp.sum(onehot * offs[None, :], axis=1) + rank
    rows = 2 * n_tok + N_EXPERTS * tm
    src_token = jnp.zeros((rows,), jnp.int32).at[pos].set(tok_pair)
    tile_start = jnp.arange(rows // tm, dtype=jnp.int32) * tm
    tile_expert = jnp.minimum(jnp.sum((tile_start[:, None] >= ends[None, :]).astype(jnp.int32), axis=1),
                              N_EXPERTS - 1)
    tile_active = (tile_start < ends[-1]).astype(jnp.int32)
    prev_expert = jnp.concatenate([jnp.full((1,), -1, jnp.int32), tile_expert[:-1]])
    tile_first = tile_active * (tile_expert != prev_expert).astype(jnp.int32)
    eid = jnp.arange(N_EXPERTS, dtype=jnp.int32)
    later = (padded > 0)[None, :] & (eid[None, :] > eid[:, None])
    next_expert = jnp.min(jnp.where(later, eid[None, :], N_EXPERTS), axis=1)
    next_expert = jnp.where(next_expert == N_EXPERTS, -1, next_expert).astype(jnp.int32)
    tile_next = next_expert[tile_expert]
    return pos.astype(jnp.int32), src_token, (tile_expert, tile_active, tile_first, tile_next)


def _rope_tables(positions, dim):
    rot = dim // ROPE_FRAC
    half = rot // 2
    inv_freq = ROPE_THETA ** (-jnp.arange(half, dtype=F32) * 2.0 / rot)
    ang = positions.astype(F32)[..., None] * inv_freq
    cos, sin = jnp.cos(ang), jnp.sin(ang)
    shape = ang.shape[:-1]
    one = jnp.ones(shape + (dim - rot,), F32)
    zero = jnp.zeros(shape + (dim - rot,), F32)
    zh = jnp.zeros(shape + (half,), F32)
    c = jnp.concatenate([cos, cos, one], axis=-1)
    s1 = jnp.concatenate([-sin, zh, zero], axis=-1)
    s2 = jnp.concatenate([zh, sin, zero], axis=-1)
    rep = LANES // dim
    return jnp.concatenate([jnp.tile(a, (1, 1, rep)) for a in (c, s1, s2)], axis=-1)


def kernel(x, c, positions, w_ada, b_ada, w_in, w_out, ssm_lam_re, ssm_lam_im, ssm_log_step, ssm_b_re, ssm_b_im, ssm_c_re, ssm_c_im, ssm_d, ssm_w_glu, ssm_b_glu, pool_w, pool_scale, ln1_g, ln1_b, ln2_g, ln2_b, w_router, e_gate, e_up, e_down):
    bsz, seq, d = x.shape
    depth = w_ada.shape[0]
    n_tok = bsz * seq
    alpha = (2.0 * depth) ** 0.25
    sw = ssm_d.shape[1]
    pw = pool_scale.shape[1]
    aw = w_out.shape[1] - sw - pw
    kw = KV_HEADS * HEAD_DIM
    iqw = IDX_HEADS * IDX_DIM
    widths = (aw, kw, iqw, sw, pw)
    top_k = min(MAX_TOPK, seq // 4)

    c_pad = jnp.concatenate([c, jnp.zeros((8 - bsz % 8, d), F32)], axis=0) if bsz % 8 else c
    ada = _ada_all(c_pad, w_ada, b_ada)
    rq = _rope_tables(positions, HEAD_DIM)
    ri = _rope_tables(positions, IDX_DIM)
    wr_pad = jnp.concatenate(
        [w_router.reshape(d, N_EXPERT_GROUPS, EXPERTS_PER_GROUP).transpose(0, 2, 1).reshape(d, N_EXPERTS),
         jnp.zeros((d, LANES - N_EXPERTS), F32)], axis=1)
    wr_hi = wr_pad.astype(BF16)
    wr_lo = (wr_pad - wr_hi.astype(F32)).astype(BF16)
    wr_pad = jnp.concatenate([wr_hi, wr_lo, wr_hi], axis=0)

    o_q, o_k, o_v, o_iq, o_ik, o_iw, o_us, o_up = (0, aw, aw + kw, aw + 2 * kw, aw + 2 * kw + iqw,
                                                   aw + 2 * kw + iqw + IDX_DIM,
                                                   aw + 2 * kw + iqw + IDX_DIM + IDX_HEADS,
                                                   aw + 2 * kw + iqw + IDX_DIM + IDX_HEADS + sw)
    for l in range(depth):
        sh1, sc1, g1, sh2, sc2, g2 = [ada[l, :bsz, j * d:(j + 1) * d].reshape(bsz, 1, d) for j in range(6)]
        wl = w_in[l]
        w_r = jnp.concatenate([wl[:, o_q:o_v], wl[:, o_iq:o_ik], wl[:, o_us:], wl[:, o_ik:o_iw],
                               jnp.zeros((d, LANES - IDX_DIM), F32)], axis=1).astype(BF16)
        wvt = wl[:, o_v:o_iq].T.astype(BF16)
        wiwt = wl[:, o_iw:o_us].T.astype(BF16)
        q, k, vt, iq, ik, iwt, us, up = _inproj(x, sc1, sh1, w_r, wvt, wiwt, rq, ri, widths)
        ya = _attention(q, k, vt, iq, ik, iwt, top_k)
        prep = _ssm_prepare(ssm_lam_re[l], ssm_lam_im[l], ssm_log_step[l], ssm_b_re[l], ssm_b_im[l],
                            ssm_c_re[l], ssm_c_im[l], ssm_d[l], SSM_CHUNK)
        ys = _ssm(us, prep, SSM_CHUNK)
        x1, u2, ids, gates = _mix(x, ya, ys, up, g1, sc2, sh2, ln1_g[l][None], ln1_b[l][None],
                                  w_out[l].astype(BF16), ssm_w_glu[l].astype(BF16), ssm_b_glu[l][None],
                                  pool_w[l].astype(BF16), pool_scale[l][None], wr_pad, alpha)
        pos, src_token, tile_plan = _dispatch_plan(ids, n_tok)
        xg = _gather_rows(u2.reshape((n_tok,) + _row_tile(d)), src_token)
        yg = _experts(xg, tile_plan, e_gate, e_up, e_down, l)
        x = _combine(yg, pos, x1, gates.transpose(0, 2, 1), g2, ln2_g[l][None], ln2_b[l][None], alpha)
    return x
```

```python
import functools
import math

import jax
import jax.numpy as jnp
import numpy as np
from jax import lax
from jax.experimental import pallas as pl
from jax.experimental.pallas import tpu as pltpu

F32 = jnp.float32
BF16 = jnp.bfloat16

HEAD_DIM = 128
KV_HEADS = 2
IDX_HEADS = 8
IDX_DIM = 64
MAX_TOPK = 256
SSM_GROUP_CH = 16
SSM_STATE = 64
POOL_WINDOWS = (2, 4, 8, 16)
ROPE_THETA = 500000.0
ROPE_FRAC = 4
LN_EPS = 1e-5
N_EXPERTS = 16
N_EXPERT_GROUPS = 4
EXPERTS_PER_GROUP = N_EXPERTS // N_EXPERT_GROUPS

LANES = 128
VMEM_LIMIT = 56 * 1024 * 1024
ROW_TILE = 256
COMBINE_TILE = 128
ATTN_Q_TILE = 256
ATTN_KEY_STEP = 512
BISECT_STEPS = 18
REDUCE_SLAB = 64
ROW_SUBLANES = 16
GATHER_SRC_ROWS = 10240
SSM_CHUNK = 8
EXPERT_TILE = 256
CONVERT_ROWS = 128
COPY_ROWS = 256
ADA_COLS = 1024


def _params(*sem):
    return pltpu.CompilerParams(dimension_semantics=sem, vmem_limit_bytes=VMEM_LIMIT)


def _sigmoid(x):
    return 1.0 / (1.0 + jnp.exp(-x))


def _ada_kernel(c_ref, w_ref, b_ref, o_ref):
    c = c_ref[...]
    cond = (c * _sigmoid(c)).astype(BF16)
    o_ref[0] = jnp.dot(cond, w_ref[0].astype(BF16), preferred_element_type=F32) + b_ref[0]


def _ada_all(c_pad, w_ada, b_ada):
    depth, d, n6 = w_ada.shape
    rows = c_pad.shape[0]
    return pl.pallas_call(
        _ada_kernel,
        grid=(depth, n6 // ADA_COLS),
        in_specs=[pl.BlockSpec((rows, d), lambda l, j: (0, 0)),
                  pl.BlockSpec((1, d, ADA_COLS), lambda l, j: (l, 0, j)),
                  pl.BlockSpec((1, 1, ADA_COLS), lambda l, j: (l, 0, j))],
        out_specs=pl.BlockSpec((1, rows, ADA_COLS), lambda l, j: (l, 0, j)),
        out_shape=jax.ShapeDtypeStruct((depth, rows, n6), F32),
        compiler_params=_params("arbitrary", "arbitrary"),
        name="ada",
    )(c_pad, w_ada, b_ada.reshape(depth, 1, n6))


def _rope(xv, tab, half):
    w = xv.shape[1]
    rep = w // LANES
    c = jnp.tile(tab[:, 0:LANES], (1, rep))
    s1 = jnp.tile(tab[:, LANES:2 * LANES], (1, rep))
    s2 = jnp.tile(tab[:, 2 * LANES:3 * LANES], (1, rep))
    return xv * c + pltpu.roll(xv, w - half, 1) * s1 + pltpu.roll(xv, half, 1) * s2


def _inproj_kernel(x_ref, sc_ref, sh_ref, w_ref, wvt_ref, wiwt_ref, rq_ref, ri_ref,
                   q_ref, k_ref, vt_ref, iq_ref, ik_ref, iwt_ref, us_ref, up_ref, *, widths, idx_scale):
    aw, kw, iqw, sw, pw = widths
    u = (x_ref[0] * (1.0 + sc_ref[0]) + sh_ref[0]).astype(BF16)
    rq = rq_ref[0]
    ri = ri_ref[0]

    def mm(lo, n):
        return jnp.dot(u, w_ref[:, lo:lo + n], preferred_element_type=F32)

    def mm_t(wt_ref):
        return lax.dot_general(wt_ref[...], u, (((1,), (1,)), ((), ())), preferred_element_type=F32)

    qhalf = HEAD_DIM // ROPE_FRAC // 2
    ihalf = IDX_DIM // ROPE_FRAC // 2
    o = 0
    q_ref[0] = (_rope(mm(o, aw), rq, qhalf) * (HEAD_DIM ** -0.5 * math.log2(math.e))).astype(BF16)
    o += aw
    k_ref[0] = _rope(mm(o, kw), rq, qhalf).astype(BF16)
    o += kw
    iq_ref[0] = _rope(mm(o, iqw), ri, ihalf).astype(BF16)
    o += iqw
    us_ref[0] = mm(o, sw)
    o += sw
    up_ref[0] = mm(o, pw)
    o += pw
    ik_ref[0] = _rope(mm(o, LANES), ri, ihalf)[:, 0:IDX_DIM].astype(BF16)
    vt_ref[0] = mm_t(wvt_ref).astype(BF16)
    iwt_ref[0] = mm_t(wiwt_ref) * idx_scale


def _inproj(x, sc, sh, w_r, wvt, wiwt, rq, ri, widths):
    bsz, seq, d = x.shape
    aw, kw, iqw, sw, pw = widths
    tm = min(ROW_TILE, seq)
    tile = lambda n: pl.BlockSpec((1, tm, n), lambda b, i: (b, i, 0))
    tile_t = lambda n: pl.BlockSpec((1, n, tm), lambda b, i: (b, 0, i))
    vec = pl.BlockSpec((1, 1, d), lambda b, i: (b, 0, 0))
    full = lambda a: pl.BlockSpec(a.shape, lambda b, i: (0, 0))
    sds = jax.ShapeDtypeStruct
    return pl.pallas_call(
        functools.partial(_inproj_kernel, widths=widths, idx_scale=(IDX_DIM ** -0.5) * (IDX_HEADS ** -0.5)),
        grid=(bsz, seq // tm),
        in_specs=[tile(d), vec, vec, full(w_r), full(wvt), full(wiwt), tile(3 * LANES), tile(3 * LANES)],
        out_specs=[tile(aw), tile(kw), tile_t(kw), tile(iqw), tile(IDX_DIM), tile_t(IDX_HEADS), tile(sw), tile(pw)],
        out_shape=[sds((bsz, seq, aw), BF16), sds((bsz, seq, kw), BF16), sds((bsz, kw, seq), BF16),
                   sds((bsz, seq, iqw), BF16), sds((bsz, seq, IDX_DIM), BF16), sds((bsz, IDX_HEADS, seq), F32),
                   sds((bsz, seq, sw), F32), sds((bsz, seq, pw), F32)],
        compiler_params=_params("parallel", "arbitrary"),
        name="inproj",
    )(x, sc, sh, w_r, wvt, wiwt, rq, ri)


def _col_reduce(op, v):
    rows, n = v.shape
    slab = op(v.reshape(rows // REDUCE_SLAB, REDUCE_SLAB, n), axis=0)
    return op(slab, axis=0, keepdims=True)


def _attn_kernel(*refs, q0, tq, keys, top_k, nq):
    i = pl.program_id(1)
    for j in range(nq):
        extent = keys - (nq - 1 - j) * tq
        pl.when(i == j)(functools.partial(_attn_tile, *refs, q0=q0, tq=tq, keys=extent, top_k=top_k))


def _attn_tile(q_ref, iq_ref, iwt_ref, k_ref, vt_ref, ik_ref, o_ref, mask_ref, x_ref, *, q0, tq, keys, top_k):
    i = pl.program_id(1)
    t = q0 + i * tq + lax.broadcasted_iota(jnp.int32, (1, tq), 1)
    kpos = lax.broadcasted_iota(jnp.int32, (keys, 1), 0)
    causal = kpos <= t

    iq = iq_ref[0]
    ik = ik_ref[0, 0:keys, :]
    iwt = iwt_ref[0]
    isc = jnp.zeros((keys, tq), F32)
    for h in range(IDX_HEADS):
        r = lax.dot_general(ik, iq[:, h * IDX_DIM:(h + 1) * IDX_DIM], (((1,), (1,)), ((), ())),
                            preferred_element_type=F32)
        isc = isc + jnp.maximum(r, 0.0) * iwt[h:h + 1, :]
    x = jnp.where(causal, isc, -jnp.inf)
    x_ref[0:keys, :] = x

    kf = float(top_k)
    n_valid = (t + 1).astype(F32)
    need = n_valid > kf

    def fold(step_fn, init):
        def body(s, acc):
            row = pl.multiple_of(s * REDUCE_SLAB, REDUCE_SLAB)
            return step_fn(acc, x_ref[pl.ds(row, REDUCE_SLAB), :])
        acc0 = jax.tree.map(lambda v: jnp.full((REDUCE_SLAB, tq), v, F32), init)
        return lax.fori_loop(0, keys // REDUCE_SLAB, body, acc0, unroll=4)

    def count_ge(theta):
        acc = fold(lambda a, xs: a + jnp.where(xs >= theta, 1.0, 0.0), 0.0)
        return jnp.sum(acc, axis=0, keepdims=True)

    rmax = _col_reduce(jnp.max, x)
    rmin = _col_reduce(jnp.min, jnp.where(causal, isc, jnp.inf))
    c_max = count_ge(rmax)
    top_tie = c_max >= kf
    lo0 = jnp.where(need, jnp.where(top_tie, rmax, rmin), -jnp.inf)
    clo0 = jnp.where(top_tie, c_max, n_valid)
    done0 = jnp.where(need & jnp.logical_not(top_tie) & (clo0 != kf), 0.0, 1.0)

    def bisect(lo, hi, clo, chi, done):
        mid = 0.5 * lo + 0.5 * hi
        c = count_ge(mid)
        live = done == 0.0
        ge = c >= kf
        up = live & ge
        dn = live & jnp.logical_not(ge)
        lo = jnp.where(up, mid, lo)
        clo = jnp.where(up, c, clo)
        hi = jnp.where(dn, mid, hi)
        chi = jnp.where(dn, c, chi)
        done = jnp.where(clo == kf, 1.0, done)
        return lo, hi, clo, chi, done

    def n_active(done):
        return jnp.sum(1.0 - done).astype(jnp.int32)

    def body2(s):
        _, lo, hi, clo, chi, done = s
        lo, hi, clo, chi, done = bisect(lo, hi, clo, chi, done)

        def min_max(acc, xs):
            inside = (xs >= lo) & (xs < hi)
            return (jnp.minimum(acc[0], jnp.where(inside, xs, jnp.inf)),
                    jnp.maximum(acc[1], jnp.where(inside, xs, -jnp.inf)))

        mins, maxs = fold(min_max, (jnp.inf, -jnp.inf))
        vmin = jnp.min(mins, axis=0, keepdims=True)
        vmax = jnp.max(maxs, axis=0, keepdims=True)
        c2 = count_ge(vmax)
        live = done == 0.0
        single = vmin == vmax
        top_ok = c2 >= kf
        take_top = live & jnp.logical_not(single) & top_ok
        drop_top = live & jnp.logical_not(single) & jnp.logical_not(top_ok)
        lo = jnp.where(live, jnp.where(take_top, vmax, vmin), lo)
        clo = jnp.where(take_top, c2, clo)
        hi = jnp.where(drop_top, vmax, hi)
        chi = jnp.where(drop_top, c2, chi)
        done = jnp.where(live & (single | top_ok), 1.0, done)
        return n_active(done), lo, hi, clo, chi, done

    state = lax.fori_loop(0, BISECT_STEPS, lambda _, s: bisect(*s), (lo0, rmax, clo0, c_max, done0))
    state = lax.while_loop(lambda s: s[0] > 0, body2, (n_active(state[4]),) + state)
    _, lo, _, clo, _, _ = state

    mask_ref[0:keys, :] = jnp.where(causal & (x_ref[0:keys, :] >= lo), 0.0, -1e30)

    tie_q = jnp.sum(jnp.where(need & (clo > kf), 1.0, 0.0)).astype(jnp.int32)

    @pl.when(tie_q > 0)
    def _():
        room = kf - _col_reduce(jnp.sum, jnp.where(x_ref[0:keys, :] > lo, 1.0, 0.0))
        step = 256
        rr = lax.broadcasted_iota(jnp.int32, (step, step), 0)
        cc = lax.broadcasted_iota(jnp.int32, (step, step), 1)
        before = jnp.where(cc < rr, 1.0, 0.0).astype(BF16)
        carry = jnp.zeros((1, tq), F32)
        for c0 in range(0, keys, step):
            xs = x_ref[c0:c0 + step, :]
            e = jnp.where(xs == lo, 1.0, 0.0)
            rank = carry + jnp.dot(before, e.astype(BF16), preferred_element_type=F32)
            keep = (xs > lo) | ((xs == lo) & (rank < room))
            mask_ref[c0:c0 + step, :] = jnp.where(keep & (xs > -jnp.inf), 0.0, -1e30)
            carry = carry + jnp.sum(e, axis=0, keepdims=True)

    bias = mask_ref[0:keys, :]
    q = q_ref[0]
    group = q.shape[1] // HEAD_DIM // KV_HEADS
    bias_g = jnp.concatenate([bias] * group, axis=1)
    for g in range(KV_HEADS):
        qg = jnp.concatenate([q[:, (g * group + j) * HEAD_DIM:(g * group + j + 1) * HEAD_DIM]
                              for j in range(group)], axis=0)
        kg = k_ref[0, 0:keys, g * HEAD_DIM:(g + 1) * HEAD_DIM]
        vtg = vt_ref[0, g * HEAD_DIM:(g + 1) * HEAD_DIM, 0:keys]
        s = lax.dot_general(kg, qg, (((1,), (1,)), ((), ())), preferred_element_type=F32) + bias_g
        m = _col_reduce(jnp.max, s)
        p = jnp.exp2(s - m)
        den = _col_reduce(jnp.sum, p)
        og = jnp.dot(vtg, p.astype(BF16), preferred_element_type=F32) / den
        for j in range(group):
            h = g * group + j
            o_ref[0, :, h * HEAD_DIM:(h + 1) * HEAD_DIM] = og[:, j * tq:(j + 1) * tq].T.astype(BF16)


def _attention(q, k, vt, iq, ik, iwt, top_k):
    bsz, seq, aw = q.shape
    tq = min(ATTN_Q_TILE, seq)
    step = min(ATTN_KEY_STEP, seq)
    outs = []
    for q0 in range(0, seq, step):
        keys = q0 + step
        nq = step // tq
        qtile = lambda n, q0=q0: pl.BlockSpec((1, tq, n), lambda b, i: (b, q0 // tq + i, 0))
        ktile = lambda n, keys=keys: pl.BlockSpec((1, keys, n), lambda b, i: (b, 0, 0))
        outs.append(pl.pallas_call(
            functools.partial(_attn_kernel, q0=q0, tq=tq, keys=keys, top_k=top_k, nq=nq),
            grid=(bsz, nq),
            in_specs=[qtile(aw), qtile(iq.shape[2]),
                      pl.BlockSpec((1, iwt.shape[1], tq), lambda b, i, q0=q0: (b, 0, q0 // tq + i)),
                      ktile(k.shape[2]),
                      pl.BlockSpec((1, vt.shape[1], keys), lambda b, i: (b, 0, 0)),
                      ktile(ik.shape[2])],
            out_specs=pl.BlockSpec((1, tq, aw), lambda b, i: (b, i, 0)),
            out_shape=jax.ShapeDtypeStruct((bsz, step, aw), BF16),
            scratch_shapes=[pltpu.VMEM((keys, tq), F32), pltpu.VMEM((keys, tq), F32)],
            compiler_params=_params("parallel", "arbitrary"),
            name=f"attn_k{keys}",
        )(q, iq, iwt, k, vt, ik))
    return outs


def _ssm_kernel(u_ref, kin_ref, wre_ref, wim_ref, vre_ref, vim_ref, are_ref, aim_ref, y_ref,
                sre_ref, sim_ref, xre_ref, xim_ref, *, chunk):
    bsz, seq, lanes = u_ref.shape
    nc = seq // chunk
    rows = bsz * nc
    ns = wre_ref.shape[2]
    u = u_ref[...].reshape(rows, chunk, lanes).reshape(rows, chunk * lanes).astype(BF16)
    sre_ref[...] = jnp.dot(u, wre_ref[0], preferred_element_type=F32).reshape(bsz, nc, ns)
    sim_ref[...] = jnp.dot(u, wim_ref[0], preferred_element_type=F32).reshape(bsz, nc, ns)
    a_re = are_ref[0]
    a_im = aim_ref[0]
    x_re = jnp.zeros((bsz, ns), F32)
    x_im = jnp.zeros((bsz, ns), F32)
    for c in range(nc):
        xre_ref[:, c, :] = x_re
        xim_ref[:, c, :] = x_im
        n_re = a_re * x_re - a_im * x_im + sre_ref[:, c, :]
        n_im = a_re * x_im + a_im * x_re + sim_ref[:, c, :]
        x_re, x_im = n_re, n_im
    xr = xre_ref[...].reshape(rows, ns).astype(BF16)
    xi = xim_ref[...].reshape(rows, ns).astype(BF16)
    y = (jnp.dot(u, kin_ref[0], preferred_element_type=F32)
         + jnp.dot(xr, vre_ref[0], preferred_element_type=F32)
         + jnp.dot(xi, vim_ref[0], preferred_element_type=F32))
    y_ref[...] = y.reshape(rows, chunk, lanes).reshape(bsz, seq, lanes)


def _ssm_placement(gs, ch, p):
    a = np.arange(gs)[:, None, None]
    lane = np.arange(gs * ch)[None, None, :]
    on_lane = ((lane // ch) == a) & ((lane % ch) == np.arange(ch)[None, :, None])
    q = np.arange(gs * p)[None, None, :]
    on_state = ((q // p) == a) & ((q % p) == np.arange(p)[None, :, None])
    return jnp.asarray(on_lane, F32), jnp.asarray(on_state, F32)


def _ssm_prepare(lam_re, lam_im, log_step, b_re, b_im, c_re, c_im, d_skip, chunk):
    g, p = lam_re.shape
    ch = b_re.shape[2]
    hp = lax.Precision.HIGHEST
    step = jnp.exp(log_step)[:, None]
    lsr, lsi = lam_re * step, lam_im * step
    er = jnp.exp(lsr)
    nr, ni = er * jnp.cos(lsi) - 1.0, er * jnp.sin(lsi)
    den = lam_re * lam_re + lam_im * lam_im
    fr, fi = (nr * lam_re + ni * lam_im) / den, (ni * lam_re - nr * lam_im) / den
    bbr = fr[..., None] * b_re - fi[..., None] * b_im
    bbi = fr[..., None] * b_im + fi[..., None] * b_re
    n = jnp.arange(chunk + 1, dtype=F32)
    mag = jnp.exp(lsr[..., None] * n)
    pr, pi = mag * jnp.cos(lsi[..., None] * n), mag * jnp.sin(lsi[..., None] * n)
    prt, pit = pr[:, :, :chunk, None], pi[:, :, :chunk, None]
    wr = prt * bbr[:, :, None, :] - pit * bbi[:, :, None, :]
    wi = prt * bbi[:, :, None, :] + pit * bbr[:, :, None, :]
    ker = (jnp.einsum("gcp,gptd->gtcd", c_re, wr, precision=hp)
           - jnp.einsum("gcp,gptd->gtcd", c_im, wi, precision=hp))
    ker = ker.at[:, 0].add(jnp.eye(ch, dtype=F32) * d_skip.reshape(g, 1, ch))
    pr1, pi1 = pr[:, None, :, 1:], pi[:, None, :, 1:]
    vr = c_re[..., None] * pr1 - c_im[..., None] * pi1
    vi = c_re[..., None] * pi1 + c_im[..., None] * pr1
    gs = LANES // ch
    ns = g // gs
    lanes = chunk * LANES
    on_lane, on_state = _ssm_placement(gs, ch, p)
    bf16_exact = lambda m: m.astype(BF16).astype(F32)
    by_tile = lambda m: m.reshape(ns, gs, *m.shape[1:])
    lag_tile = jnp.einsum("satcd,acl->stadl", by_tile(bf16_exact(ker)), on_lane).reshape(ns, chunk, LANES, LANES)
    lag = jnp.arange(chunk)[None, :] - jnp.arange(chunk)[:, None]
    tiles = jnp.where((lag >= 0)[None, :, :, None, None], lag_tile[:, jnp.clip(lag, 0, chunk - 1)], 0.0)
    kin8 = tiles.transpose(0, 1, 3, 2, 4).reshape(ns, lanes, lanes).astype(BF16)

    def to_in(w):
        wide = jnp.einsum("sapid,apq->siadq", by_tile(bf16_exact(w[:, :, ::-1, :])), on_state)
        return wide.reshape(ns, lanes, gs * p).astype(BF16)

    def to_out(v):
        wide = jnp.einsum("sacpj,acl->sapjl", by_tile(bf16_exact(v)), on_lane)
        return wide.reshape(ns, gs * p, lanes).astype(BF16)

    a_re = pr[..., chunk].reshape(ns, 1, gs * p)
    a_im = pi[..., chunk].reshape(ns, 1, gs * p)
    return kin8, to_in(wr), to_in(wi), to_out(vr), to_out(-vi), a_re, a_im


def _ssm(us, prep, chunk):
    bsz, seq, w = us.shape
    kin, wre, wim, vre, vim, a_re, a_im = prep
    ns, lanes, states = wre.shape
    nc = seq // chunk
    act = pl.BlockSpec((bsz, seq, LANES), lambda i: (0, 0, i))
    mat = lambda a, b: pl.BlockSpec((1, a, b), lambda i: (i, 0, 0))
    return pl.pallas_call(
        functools.partial(_ssm_kernel, chunk=chunk),
        grid=(ns,),
        in_specs=[act, mat(lanes, lanes), mat(lanes, states), mat(lanes, states), mat(states, lanes),
                  mat(states, lanes), mat(1, states), mat(1, states)],
        out_specs=act,
        out_shape=jax.ShapeDtypeStruct((bsz, seq, w), F32),
        scratch_shapes=[pltpu.VMEM((bsz, nc, states), F32)] * 4,
        compiler_params=_params("parallel"),
        name="ssm",
    )(us, kin, wre, wim, vre, vim, a_re, a_im)


def _row_tile(d):
    return (ROW_SUBLANES, d // ROW_SUBLANES)


def _to_rows(v):
    return v.reshape((v.shape[0],) + _row_tile(v.shape[1]))


def _from_rows(v):
    return v.reshape(v.shape[0], v.shape[1] * v.shape[2])


def _layer_norm(h, g, b):
    mu = jnp.mean(h, axis=1, keepdims=True)
    hc = h - mu
    var = jnp.mean(hc * hc, axis=1, keepdims=True)
    return hc * lax.rsqrt(var + LN_EPS) * g + b


def _route(logits_t):
    ng, per = N_EXPERT_GROUPS, EXPERTS_PER_GROUP
    m = jnp.max(logits_t, axis=0, keepdims=True)
    e = jnp.exp(logits_t - m)
    prob = e / jnp.sum(e, axis=0, keepdims=True)
    v = [prob[j * ng:(j + 1) * ng, :] for j in range(per)]

    def top(vals):
        best = functools.reduce(jnp.maximum, vals)
        idx = jnp.full(best.shape, per - 1, jnp.int32)
        for j in range(per - 2, -1, -1):
            idx = jnp.where(vals[j] == best, j, idx)
        return best, idx

    m1, i1 = top(v)
    m2, i2 = top([jnp.where(i1 == j, -1.0, v[j]) for j in range(per)])
    score = m1 + m2
    gid = lax.broadcasted_iota(jnp.int32, score.shape, 0)
    best_g = jnp.min(jnp.where(score == jnp.max(score, axis=0, keepdims=True), gid, ng), axis=0, keepdims=True)
    sel = gid == best_g
    pick_f = lambda a: jnp.sum(jnp.where(sel, a, 0.0), axis=0, keepdims=True)
    pick_i = lambda a: jnp.sum(jnp.where(sel, a, 0), axis=0, keepdims=True)
    p1, p2 = pick_f(m1), pick_f(m2)
    e1, e2 = best_g * per + pick_i(i1), best_g * per + pick_i(i2)
    den = p1 + p2
    return jnp.concatenate([e1, e2], axis=0), jnp.concatenate([p1 / den, p2 / den], axis=0)


def _mix_kernel(x_ref, ys_ref, upc_ref, upp_ref, g1_ref, sc2_ref, sh2_ref, lng_ref, lnb_ref,
                wout_ref, wglu_ref, bglu_ref, wpool_ref, pscale_ref, wr_ref, *rest, alpha, tiles_per_variant):
    ya_refs, (x1_ref, u2_ref, ids_ref, gates_ref) = rest[:-4], rest[-4:]
    i = pl.program_id(1)
    tm = x_ref.shape[1]
    owner = i // tiles_per_variant
    ya = ya_refs[0][0]
    for k in range(1, len(ya_refs)):
        ya = jnp.where(owner == k, ya_refs[k][0], ya)
    y = ys_ref[0]
    y = 0.5 * y * (1.0 + jnp.tanh(math.sqrt(2.0 / math.pi) * (y + 0.044715 * (y * y * y))))
    z = jnp.dot(y.astype(BF16), wglu_ref[...], preferred_element_type=F32) + bglu_ref[...]
    y = y * _sigmoid(z)
    upc = upc_ref[0]
    upp = jnp.where(i > 0, upp_ref[0], 0.0)
    cat = jnp.concatenate([upp, upc], axis=0).astype(BF16)
    r = lax.broadcasted_iota(jnp.int32, (tm, 1), 0)
    lagm = (r + tm) - lax.broadcasted_iota(jnp.int32, (1, 2 * tm), 1)
    tpos = (i * tm + r + 1).astype(F32)
    gc = upc.shape[1] // len(POOL_WINDOWS)
    pooled = []
    for g, win in enumerate(POOL_WINDOWS):
        band = jnp.where((lagm >= 0) & (lagm < win), 1.0, 0.0).astype(BF16)
        ws = jnp.dot(band, cat[:, g * gc:(g + 1) * gc], preferred_element_type=F32)
        pg = ws / jnp.minimum(tpos, float(win)) - upc[:, g * gc:(g + 1) * gc]
        pooled.append(jnp.dot(pg.astype(BF16), wpool_ref[g], preferred_element_type=F32))
    yp = jnp.concatenate(pooled, axis=1) * pscale_ref[...]
    mixed = jnp.concatenate([ya, y.astype(BF16), yp.astype(BF16)], axis=1)
    mix = jnp.dot(mixed, wout_ref[...], preferred_element_type=F32)
    x1 = _layer_norm(alpha * x_ref[0] + (1.0 + g1_ref[0]) * mix, lng_ref[...], lnb_ref[...])
    x1_ref[0] = x1
    u2 = x1 * (1.0 + sc2_ref[0]) + sh2_ref[0]
    u2_ref[0] = _to_rows(u2.astype(BF16))
    u_hi = u2.astype(BF16)
    u_lo = (u2 - u_hi.astype(F32)).astype(BF16)
    logits = jnp.dot(jnp.concatenate([u_hi, u_hi, u_lo], axis=1), wr_ref[...], preferred_element_type=F32)
    ids, gates = _route(logits.T[0:N_EXPERTS, :])
    ids_ref[0] = ids
    gates_ref[0] = gates


def _mix(x, yas, ys, up, g1, sc2, sh2, lng, lnb, wout, wglu, bglu, wpool, pscale, wr_pad, alpha):
    bsz, seq, d = x.shape
    tm = min(ROW_TILE, seq)
    per = yas[0].shape[1] // tm
    att = [pl.BlockSpec((1, tm, ya.shape[2]), lambda b, i, k=k: (b, jnp.clip(i - k * per, 0, per - 1), 0))
           for k, ya in enumerate(yas)]
    tile = lambda n: pl.BlockSpec((1, tm, n), lambda b, i: (b, i, 0))
    prev = pl.BlockSpec((1, tm, up.shape[2]), lambda b, i: (b, jnp.maximum(i - 1, 0), 0))
    vec = pl.BlockSpec((1, 1, d), lambda b, i: (b, 0, 0))
    full = lambda a: pl.BlockSpec(a.shape, lambda b, i: (0,) * a.ndim)
    lane_rows = pl.BlockSpec((1, 2, tm), lambda b, i: (b, 0, i))
    return pl.pallas_call(
        functools.partial(_mix_kernel, alpha=alpha, tiles_per_variant=per),
        grid=(bsz, seq // tm),
        in_specs=[tile(d), tile(ys.shape[2]), tile(up.shape[2]), prev, vec, vec, vec,
                  full(lng), full(lnb), full(wout), full(wglu), full(bglu), full(wpool), full(pscale), full(wr_pad)]
        + att,
        out_specs=[tile(d), pl.BlockSpec((1, tm) + _row_tile(d), lambda b, i: (b, i, 0, 0)), lane_rows, lane_rows],
        out_shape=[jax.ShapeDtypeStruct((bsz, seq, d), F32), jax.ShapeDtypeStruct((bsz, seq) + _row_tile(d), BF16),
                   jax.ShapeDtypeStruct((bsz, 2, seq), jnp.int32), jax.ShapeDtypeStruct((bsz, 2, seq), F32)],
        compiler_params=_params("parallel", "arbitrary"),
        name="mix",
    )(x, ys, up, up, g1, sc2, sh2, lng, lnb, wout, wglu, bglu, wpool, pscale, wr_pad, *yas)


def _gather_kernel(idx_ref, src_ref, *rest, lo, n_src, partial_src, has_prev):
    if has_prev:
        prev_ref, out_ref, buf, sem = rest
    else:
        out_ref, buf, sem = rest

    @pl.when(pl.program_id(0) == 0)
    def _():
        cp = pltpu.make_async_copy(src_ref.at[pl.ds(lo, n_src)], buf, sem)
        cp.start()
        cp.wait()

    base = pl.program_id(0) * COPY_ROWS

    def move(r, carry):
        s = idx_ref[base + r]
        row = buf[jnp.maximum(s, 0)] if partial_src else buf[s]
        if has_prev:
            row = jnp.where(s >= 0, row, prev_ref[r])
        out_ref[r] = row
        return carry

    lax.fori_loop(0, COPY_ROWS, move, 0, unroll=8)


def _slab_plan(src, idx):
    n_chunks = pl.cdiv(src.shape[0], GATHER_SRC_ROWS)
    n_src = src.shape[0] // n_chunks
    assert n_src * n_chunks == src.shape[0]
    if n_chunks == 1:
        return n_src, [(0, idx)]
    local = [idx - c * n_src for c in range(n_chunks)]
    return n_src, [(c * n_src, jnp.where((loc >= 0) & (loc < n_src), loc, -1)) for c, loc in enumerate(local)]


def _gather_rows(src, idx, keep_last_slab=False):
    n = idx.shape[0]
    n_src, plan = _slab_plan(src, idx)
    n_chunks = len(plan)
    block = pl.BlockSpec((COPY_ROWS,) + src.shape[1:], lambda i, idx: (i, 0, 0))
    out = None
    for c, (lo, local) in enumerate(plan[:-1] if keep_last_slab else plan):
        has_prev = c > 0
        out = pl.pallas_call(
            functools.partial(_gather_kernel, lo=lo, n_src=n_src, partial_src=n_chunks > 1, has_prev=has_prev),
            grid_spec=pltpu.PrefetchScalarGridSpec(
                num_scalar_prefetch=1, grid=(n // COPY_ROWS,),
                in_specs=[pl.BlockSpec(memory_space=pl.ANY)] + ([block] if has_prev else []),
                out_specs=block,
                scratch_shapes=[pltpu.VMEM((n_src,) + src.shape[1:], src.dtype), pltpu.SemaphoreType.DMA(())]),
            out_shape=jax.ShapeDtypeStruct((n,) + src.shape[1:], src.dtype),
            compiler_params=_params("arbitrary"),
            name="gather",
        )(*((local, src, out) if has_prev else (local, src)))
    return (out, n_src, plan[-1]) if keep_last_slab else out


def _expert_kernel(te_ref, act_ref, first_ref, nxt_ref, x_ref, wg_hbm, wu_hbm, wd_hbm, y_ref,
                   stage_g, stage_u, stage_d, wg_ref, wu_ref, wd_ref, sems, *, layer):
    i = pl.program_id(0)

    def fetch(e):
        return (pltpu.make_async_copy(wg_hbm.at[layer, e], stage_g, sems.at[0]),
                pltpu.make_async_copy(wu_hbm.at[layer, e], stage_u, sems.at[1]),
                pltpu.make_async_copy(wd_hbm.at[layer, e], stage_d, sems.at[2]))

    @pl.when(i == 0)
    def _():
        for cp in fetch(te_ref[0]):
            cp.start()

    @pl.when(first_ref[i] != 0)
    def _():
        for cp in fetch(te_ref[i]):
            cp.wait()
        for stage, work in ((stage_g, wg_ref), (stage_u, wu_ref), (stage_d, wd_ref)):
            rows = stage.shape[0]

            def convert(r, carry, stage=stage, work=work):
                sl = pl.ds(pl.multiple_of(r * CONVERT_ROWS, CONVERT_ROWS), CONVERT_ROWS)
                work[sl, :] = stage[sl, :].astype(BF16)
                return carry

            lax.fori_loop(0, rows // CONVERT_ROWS, convert, 0)

        @pl.when(nxt_ref[i] >= 0)
        def _():
            for cp in fetch(nxt_ref[i]):
                cp.start()

    @pl.when(act_ref[i] != 0)
    def _():
        x = _from_rows(x_ref[...])
        g = jnp.dot(x, wg_ref[...], preferred_element_type=F32)
        u = jnp.dot(x, wu_ref[...], preferred_element_type=F32)
        h = (g * _sigmoid(g)) * u
        y_ref[...] = _to_rows(jnp.dot(h.astype(BF16), wd_ref[...], preferred_element_type=F32).astype(BF16))

    @pl.when(act_ref[i] == 0)
    def _():
        y_ref[...] = jnp.zeros(y_ref.shape, y_ref.dtype)


def _experts(xg, plan, wg, wu, wd, layer):
    rows = xg.shape[0]
    d, f = wg.shape[2], wg.shape[3]
    tm = EXPERT_TILE
    tile = pl.BlockSpec((tm,) + _row_tile(d), lambda i, *_: (i, 0, 0))
    hbm = pl.BlockSpec(memory_space=pl.ANY)
    return pl.pallas_call(
        functools.partial(_expert_kernel, layer=layer),
        grid_spec=pltpu.PrefetchScalarGridSpec(
            num_scalar_prefetch=4, grid=(rows // tm,),
            in_specs=[tile, hbm, hbm, hbm],
            out_specs=tile,
            scratch_shapes=[pltpu.VMEM((d, f), F32), pltpu.VMEM((d, f), F32), pltpu.VMEM((f, d), F32),
                            pltpu.VMEM((d, f), BF16), pltpu.VMEM((d, f), BF16), pltpu.VMEM((f, d), BF16),
                            pltpu.SemaphoreType.DMA((3,))]),
        out_shape=jax.ShapeDtypeStruct((rows,) + _row_tile(d), BF16),
        compiler_params=_params("arbitrary"),
        name="experts",
    )(*plan, xg, wg, wu, wd)


def _combine_kernel(idx_ref, src_ref, *rest, lo, n_src, n_tok, partial_src, has_prev, alpha):
    if has_prev:
        prev_ref, x_ref, gc_ref, g2_ref, lng_ref, lnb_ref, o_ref, buf, rows, sem = rest
    else:
        x_ref, gc_ref, g2_ref, lng_ref, lnb_ref, o_ref, buf, rows, sem = rest

    @pl.when(pl.program_id(0) == 0)
    def _():
        cp = pltpu.make_async_copy(src_ref.at[pl.ds(lo, n_src)], buf, sem)
        cp.start()
        cp.wait()

    tm = x_ref.shape[0]
    base = pl.program_id(0) * tm
    for k in range(2):
        def move(r, carry, k=k):
            s = idx_ref[k * n_tok + base + r]
            row = buf[jnp.maximum(s, 0)] if partial_src else buf[s]
            if has_prev:
                row = jnp.where(s >= 0, row, prev_ref[k, r])
            rows[k, r] = row
            return carry

        lax.fori_loop(0, tm, move, 0, unroll=8)
    gc = gc_ref[...]
    ffn = gc[:, 0:1] * _from_rows(rows[0]).astype(F32) + gc[:, 1:2] * _from_rows(rows[1]).astype(F32)
    o_ref[...] = _layer_norm(alpha * x_ref[...] + (1.0 + g2_ref[0]) * ffn, lng_ref[...], lnb_ref[...])


def _combine(yg, pos, x1, gates_col, g2, lng, lnb, alpha):
    bsz, seq, d = x1.shape
    n_tok = bsz * seq
    tm = COMBINE_TILE
    per_batch = seq // tm
    prev, n_src, (lo, local) = _gather_rows(yg, pos, keep_last_slab=True)
    has_prev = prev is not None
    row = _row_tile(d)
    tile = pl.BlockSpec((tm, d), lambda i, idx: (i, 0))
    in_specs = [pl.BlockSpec(memory_space=pl.ANY)]
    args = [local, yg]
    if has_prev:
        in_specs.append(pl.BlockSpec((2, tm) + row, lambda i, idx: (0, i, 0, 0)))
        args.append(prev.reshape((2, n_tok) + row))
    in_specs += [tile, pl.BlockSpec((tm, 2), lambda i, idx: (i, 0)),
                 pl.BlockSpec((1, 1, d), lambda i, idx: (i // per_batch, 0, 0)),
                 pl.BlockSpec(lng.shape, lambda i, idx: (0, 0)), pl.BlockSpec(lnb.shape, lambda i, idx: (0, 0))]
    args += [x1.reshape(n_tok, d), gates_col.reshape(n_tok, 2), g2, lng, lnb]
    out = pl.pallas_call(
        functools.partial(_combine_kernel, lo=lo, n_src=n_src, n_tok=n_tok, partial_src=has_prev,
                          has_prev=has_prev, alpha=alpha),
        grid_spec=pltpu.PrefetchScalarGridSpec(
            num_scalar_prefetch=1, grid=(n_tok // tm,),
            in_specs=in_specs,
            out_specs=tile,
            scratch_shapes=[pltpu.VMEM((n_src,) + row, yg.dtype), pltpu.VMEM((2, tm) + row, yg.dtype),
                            pltpu.SemaphoreType.DMA(())]),
        out_shape=jax.ShapeDtypeStruct((n_tok, d), F32),
        compiler_params=_params("arbitrary"),
        name="combine",
    )(*args)
    return out.reshape(bsz, seq, d)


def _dispatch_plan(ids, n_tok):
    tm = EXPERT_TILE
    e_pair = jnp.concatenate([ids[:, 0, :].reshape(n_tok), ids[:, 1, :].reshape(n_tok)])
    tok_pair = jnp.concatenate([jnp.arange(n_tok, dtype=jnp.int32)] * 2)
    onehot = (e_pair[:, None] == jnp.arange(N_EXPERTS, dtype=jnp.int32)[None, :]).astype(jnp.int32)
    csum = jnp.cumsum(onehot, axis=0)
    rank = jnp.sum(onehot * csum, axis=1) - 1
    counts = csum[-1]
    padded = ((counts + tm - 1) // tm) * tm
    ends = jnp.cumsum(padded)
    offs = ends - padded
    pos = jnp.sum(onehot * offs[None, :], axis=1) + rank
    rows = 2 * n_tok + N_EXPERTS * tm
    src_token = jnp.zeros((rows,), jnp.int32).at[pos].set(tok_pair)
    tile_start = jnp.arange(rows // tm, dtype=jnp.int32) * tm
    tile_expert = jnp.minimum(jnp.sum((tile_start[:, None] >= ends[None, :]).astype(jnp.int32), axis=1),
                              N_EXPERTS - 1)
    tile_active = (tile_start < ends[-1]).astype(jnp.int32)
    prev_expert = jnp.concatenate([jnp.full((1,), -1, jnp.int32), tile_expert[:-1]])
    tile_first = tile_active * (tile_expert != prev_expert).astype(jnp.int32)
    eid = jnp.arange(N_EXPERTS, dtype=jnp.int32)
    later = (padded > 0)[None, :] & (eid[None, :] > eid[:, None])
    next_expert = jnp.min(jnp.where(later, eid[None, :], N_EXPERTS), axis=1)
    next_expert = jnp.where(next_expert == N_EXPERTS, -1, next_expert).astype(jnp.int32)
    tile_next = next_expert[tile_expert]
    return pos.astype(jnp.int32), src_token, (tile_expert, tile_active, tile_first, tile_next)


def _rope_tables(positions, dim):
    rot = dim // ROPE_FRAC
    half = rot // 2
    inv_freq = ROPE_THETA ** (-jnp.arange(half, dtype=F32) * 2.0 / rot)
    ang = positions.astype(F32)[..., None] * inv_freq
    cos, sin = jnp.cos(ang), jnp.sin(ang)
    shape = ang.shape[:-1]
    one = jnp.ones(shape + (dim - rot,), F32)
    zero = jnp.zeros(shape + (dim - rot,), F32)
    zh = jnp.zeros(shape + (half,), F32)
    c = jnp.concatenate([cos, cos, one], axis=-1)
    s1 = jnp.concatenate([-sin, zh, zero], axis=-1)
    s2 = jnp.concatenate([zh, sin, zero], axis=-1)
    rep = LANES // dim
    return jnp.concatenate([jnp.tile(a, (1, 1, rep)) for a in (c, s1, s2)], axis=-1)


def kernel(x, c, positions, w_ada, b_ada, w_in, w_out, ssm_lam_re, ssm_lam_im, ssm_log_step, ssm_b_re, ssm_b_im, ssm_c_re, ssm_c_im, ssm_d, ssm_w_glu, ssm_b_glu, pool_w, pool_scale, ln1_g, ln1_b, ln2_g, ln2_b, w_router, e_gate, e_up, e_down):
    bsz, seq, d = x.shape
    depth = w_ada.shape[0]
    n_tok = bsz * seq
    alpha = (2.0 * depth) ** 0.25
    sw = ssm_d.shape[1]
    pw = pool_scale.shape[1]
    aw = w_out.shape[1] - sw - pw
    kw = KV_HEADS * HEAD_DIM
    iqw = IDX_HEADS * IDX_DIM
    widths = (aw, kw, iqw, sw, pw)
    top_k = min(MAX_TOPK, seq // 4)

    c_pad = jnp.concatenate([c, jnp.zeros((8 - bsz % 8, d), F32)], axis=0) if bsz % 8 else c
    ada = _ada_all(c_pad, w_ada, b_ada)
    rq = _rope_tables(positions, HEAD_DIM)
    ri = _rope_tables(positions, IDX_DIM)
    wr_pad = jnp.concatenate(
        [w_router.reshape(d, N_EXPERT_GROUPS, EXPERTS_PER_GROUP).transpose(0, 2, 1).reshape(d, N_EXPERTS),
         jnp.zeros((d, LANES - N_EXPERTS), F32)], axis=1)
    wr_hi = wr_pad.astype(BF16)
    wr_lo = (wr_pad - wr_hi.astype(F32)).astype(BF16)
    wr_pad = jnp.concatenate([wr_hi, wr_lo, wr_hi], axis=0)

    o_q, o_k, o_v, o_iq, o_ik, o_iw, o_us, o_up = (0, aw, aw + kw, aw + 2 * kw, aw + 2 * kw + iqw,
                                                   aw + 2 * kw + iqw + IDX_DIM,
                                                   aw + 2 * kw + iqw + IDX_DIM + IDX_HEADS,
                                                   aw + 2 * kw + iqw + IDX_DIM + IDX_HEADS + sw)
    for l in range(depth):
        sh1, sc1, g1, sh2, sc2, g2 = [ada[l, :bsz, j * d:(j + 1) * d].reshape(bsz, 1, d) for j in range(6)]
        wl = w_in[l]
        w_r = jnp.concatenate([wl[:, o_q:o_v], wl[:, o_iq:o_ik], wl[:, o_us:], wl[:, o_ik:o_iw],
                               jnp.zeros((d, LANES - IDX_DIM), F32)], axis=1).astype(BF16)
        wvt = wl[:, o_v:o_iq].T.astype(BF16)
        wiwt = wl[:, o_iw:o_us].T.astype(BF16)
        q, k, vt, iq, ik, iwt, us, up = _inproj(x, sc1, sh1, w_r, wvt, wiwt, rq, ri, widths)
        ya = _attention(q, k, vt, iq, ik, iwt, top_k)
        prep = _ssm_prepare(ssm_lam_re[l], ssm_lam_im[l], ssm_log_step[l], ssm_b_re[l], ssm_b_im[l],
                            ssm_c_re[l], ssm_c_im[l], ssm_d[l], SSM_CHUNK)
        ys = _ssm(us, prep, SSM_CHUNK)
        x1, u2, ids, gates = _mix(x, ya, ys, up, g1, sc2, sh2, ln1_g[l][None], ln1_b[l][None],
                                  w_out[l].astype(BF16), ssm_w_glu[l].astype(BF16), ssm_b_glu[l][None],
                                  pool_w[l].astype(BF16), pool_scale[l][None], wr_pad, alpha)
        pos, src_token, tile_plan = _dispatch_plan(ids, n_tok)
        xg = _gather_rows(u2.reshape((n_tok,) + _row_tile(d)), src_token)
        yg = _experts(xg, tile_plan, e_gate, e_up, e_down, l)
        x = _combine(yg, pos, x1, gates.transpose(0, 2, 1), g2, ln2_g[l][None], ln2_b[l][None], alpha)
    return x
```

```python
import functools
import math

import jax
import jax.numpy as jnp
import numpy as np
from jax import lax
from jax.experimental import pallas as pl
from jax.experimental.pallas import tpu as pltpu

F32 = jnp.float32
BF16 = jnp.bfloat16

HEAD_DIM = 128
KV_HEADS = 2
IDX_HEADS = 8
IDX_DIM = 64
MAX_TOPK = 256
POOL_WINDOWS = (2, 4, 8, 16)
ROPE_THETA = 500000.0
ROPE_FRAC = 4
LN_EPS = 1e-5
N_EXPERTS = 16
N_EXPERT_GROUPS = 4
EXPERTS_PER_GROUP = N_EXPERTS // N_EXPERT_GROUPS

LANES = 128
VMEM_LIMIT = 56 * 1024 * 1024
ROW_TILE = 256
COMBINE_TILE = 128
ATTN_Q_TILE = 256
ATTN_KEY_STEP = 512
BISECT_STEPS = 14
REDUCE_SLAB = 64
ROW_SUBLANES = 16
GATHER_SRC_ROWS = 10240
SSM_CHUNK = 8
EXPERT_TILE = 256
CONVERT_ROWS = 128
COPY_ROWS = 256
ADA_COLS = 1024


def _params(*sem):
    return pltpu.CompilerParams(dimension_semantics=sem, vmem_limit_bytes=VMEM_LIMIT)


def _sigmoid(x):
    return 1.0 / (1.0 + jnp.exp(-x))


def _ada_kernel(c_ref, w_ref, b_ref, o_ref):
    c = c_ref[...]
    cond = (c * _sigmoid(c)).astype(BF16)
    o_ref[0] = jnp.dot(cond, w_ref[0].astype(BF16), preferred_element_type=F32) + b_ref[0]


def _ada_all(c_pad, w_ada, b_ada):
    depth, d, n6 = w_ada.shape
    rows = c_pad.shape[0]
    return pl.pallas_call(
        _ada_kernel,
        grid=(depth, n6 // ADA_COLS),
        in_specs=[pl.BlockSpec((rows, d), lambda l, j: (0, 0)),
                  pl.BlockSpec((1, d, ADA_COLS), lambda l, j: (l, 0, j)),
                  pl.BlockSpec((1, 1, ADA_COLS), lambda l, j: (l, 0, j))],
        out_specs=pl.BlockSpec((1, rows, ADA_COLS), lambda l, j: (l, 0, j)),
        out_shape=jax.ShapeDtypeStruct((depth, rows, n6), F32),
        compiler_params=_params("arbitrary", "arbitrary"),
        name="ada",
    )(c_pad, w_ada, b_ada.reshape(depth, 1, n6))


def _rope(xv, tab, half):
    w = xv.shape[1]
    rep = w // LANES
    c = jnp.tile(tab[:, 0:LANES], (1, rep))
    s1 = jnp.tile(tab[:, LANES:2 * LANES], (1, rep))
    s2 = jnp.tile(tab[:, 2 * LANES:3 * LANES], (1, rep))
    return xv * c + pltpu.roll(xv, w - half, 1) * s1 + pltpu.roll(xv, half, 1) * s2


def _inproj_kernel(x_ref, sc_ref, sh_ref, w_ref, wvt_ref, wiwt_ref, rq_ref, ri_ref,
                   q_ref, k_ref, vt_ref, iq_ref, ik_ref, iwt_ref, us_ref, up_ref, *, widths, idx_scale):
    aw, kw, iqw, sw, pw = widths
    u = (x_ref[0] * (1.0 + sc_ref[0]) + sh_ref[0]).astype(BF16)
    rq = rq_ref[0]
    ri = ri_ref[0]

    def mm(lo, n):
        return jnp.dot(u, w_ref[:, lo:lo + n], preferred_element_type=F32)

    def mm_t(wt_ref):
        return lax.dot_general(wt_ref[...], u, (((1,), (1,)), ((), ())), preferred_element_type=F32)

    qhalf = HEAD_DIM // ROPE_FRAC // 2
    ihalf = IDX_DIM // ROPE_FRAC // 2
    o = 0
    q_ref[0] = (_rope(mm(o, aw), rq, qhalf) * (HEAD_DIM ** -0.5 * math.log2(math.e))).astype(BF16)
    o += aw
    k_ref[0] = _rope(mm(o, kw), rq, qhalf).astype(BF16)
    o += kw
    iq_ref[0] = _rope(mm(o, iqw), ri, ihalf).astype(BF16)
    o += iqw
    us_ref[0] = mm(o, sw)
    o += sw
    up_ref[0] = mm(o, pw)
    o += pw
    ik_ref[0] = _rope(mm(o, LANES), ri, ihalf)[:, 0:IDX_DIM].astype(BF16)
    vt_ref[0] = mm_t(wvt_ref).astype(BF16)
    iwt_ref[0] = mm_t(wiwt_ref) * idx_scale


def _inproj(x, sc, sh, w_r, wvt, wiwt, rq, ri, widths):
    bsz, seq, d = x.shape
    aw, kw, iqw, sw, pw = widths
    tm = min(ROW_TILE, seq)
    tile = lambda n: pl.BlockSpec((1, tm, n), lambda b, i: (b, i, 0))
    tile_t = lambda n: pl.BlockSpec((1, n, tm), lambda b, i: (b, 0, i))
    vec = pl.BlockSpec((1, 1, d), lambda b, i: (b, 0, 0))
    full = lambda a: pl.BlockSpec(a.shape, lambda b, i: (0, 0))
    sds = jax.ShapeDtypeStruct
    return pl.pallas_call(
        functools.partial(_inproj_kernel, widths=widths, idx_scale=(IDX_DIM ** -0.5) * (IDX_HEADS ** -0.5)),
        grid=(bsz, seq // tm),
        in_specs=[tile(d), vec, vec, full(w_r), full(wvt), full(wiwt), tile(3 * LANES), tile(3 * LANES)],
        out_specs=[tile(aw), tile(kw), tile_t(kw), tile(iqw), tile(IDX_DIM), tile_t(IDX_HEADS), tile(sw), tile(pw)],
        out_shape=[sds((bsz, seq, aw), BF16), sds((bsz, seq, kw), BF16), sds((bsz, kw, seq), BF16),
                   sds((bsz, seq, iqw), BF16), sds((bsz, seq, IDX_DIM), BF16), sds((bsz, IDX_HEADS, seq), F32),
                   sds((bsz, seq, sw), F32), sds((bsz, seq, pw), F32)],
        compiler_params=_params("parallel", "arbitrary"),
        name="inproj",
    )(x, sc, sh, w_r, wvt, wiwt, rq, ri)


def _col_reduce(op, v):
    rows, n = v.shape
    slab = op(v.reshape(rows // REDUCE_SLAB, REDUCE_SLAB, n), axis=0)
    return op(slab, axis=0, keepdims=True)


def _attn_kernel(*refs, q0, tq, keys, top_k, nq):
    i = pl.program_id(1)
    for j in range(nq):
        extent = keys - (nq - 1 - j) * tq
        pl.when(i == j)(functools.partial(_attn_tile, *refs, q0=q0, tq=tq, keys=extent, top_k=top_k))


def _attn_tile(q_ref, iq_ref, iwt_ref, k_ref, vt_ref, ik_ref, o_ref, mask_ref, x_ref, *, q0, tq, keys, top_k):
    i = pl.program_id(1)
    t = q0 + i * tq + lax.broadcasted_iota(jnp.int32, (1, tq), 1)
    kpos = lax.broadcasted_iota(jnp.int32, (keys, 1), 0)
    causal = kpos <= t

    iq = iq_ref[0]
    ik = ik_ref[0, 0:keys, :]
    iwt = iwt_ref[0]
    isc = jnp.zeros((keys, tq), F32)
    for h in range(IDX_HEADS):
        r = lax.dot_general(ik, iq[:, h * IDX_DIM:(h + 1) * IDX_DIM], (((1,), (1,)), ((), ())),
                            preferred_element_type=F32)
        isc = isc + jnp.maximum(r, 0.0) * iwt[h:h + 1, :]
    x = jnp.where(causal, isc, -jnp.inf)
    x_ref[0:keys, :] = x

    kf = float(top_k)
    n_valid = (t + 1).astype(F32)
    need = n_valid > kf

    def fold(step_fn, init):
        def body(s, acc):
            row = pl.multiple_of(s * REDUCE_SLAB, REDUCE_SLAB)
            return step_fn(acc, x_ref[pl.ds(row, REDUCE_SLAB), :])
        acc0 = jax.tree.map(lambda v: jnp.full((REDUCE_SLAB, tq), v, F32), init)
        return lax.fori_loop(0, keys // REDUCE_SLAB, body, acc0, unroll=4)

    def count_ge(theta):
        acc = fold(lambda a, xs: a + jnp.where(xs >= theta, 1.0, 0.0), 0.0)
        return jnp.sum(acc, axis=0, keepdims=True)

    rmax = _col_reduce(jnp.max, x)
    rmin = _col_reduce(jnp.min, jnp.where(causal, isc, jnp.inf))
    c_max = count_ge(rmax)
    top_tie = c_max >= kf
    lo0 = jnp.where(need, jnp.where(top_tie, rmax, rmin), -jnp.inf)
    clo0 = jnp.where(top_tie, c_max, n_valid)
    done0 = jnp.where(need & jnp.logical_not(top_tie) & (clo0 != kf), 0.0, 1.0)

    def bisect(lo, hi, clo, chi, done):
        mid = 0.5 * lo + 0.5 * hi
        c = count_ge(mid)
        live = done == 0.0
        ge = c >= kf
        up = live & ge
        dn = live & jnp.logical_not(ge)
        lo = jnp.where(up, mid, lo)
        clo = jnp.where(up, c, clo)
        hi = jnp.where(dn, mid, hi)
        chi = jnp.where(dn, c, chi)
        done = jnp.where(clo == kf, 1.0, done)
        return lo, hi, clo, chi, done

    def n_active(done):
        return jnp.sum(1.0 - done).astype(jnp.int32)

    def body2(s):
        _, lo, hi, clo, chi, done = s
        lo, hi, clo, chi, done = bisect(lo, hi, clo, chi, done)

        def min_max(acc, xs):
            inside = (xs >= lo) & (xs < hi)
            return (jnp.minimum(acc[0], jnp.where(inside, xs, jnp.inf)),
                    jnp.maximum(acc[1], jnp.where(inside, xs, -jnp.inf)))

        mins, maxs = fold(min_max, (jnp.inf, -jnp.inf))
        vmin = jnp.min(mins, axis=0, keepdims=True)
        vmax = jnp.max(maxs, axis=0, keepdims=True)
        c2 = count_ge(vmax)
        live = done == 0.0
        single = vmin == vmax
        top_ok = c2 >= kf
        take_top = live & jnp.logical_not(single) & top_ok
        drop_top = live & jnp.logical_not(single) & jnp.logical_not(top_ok)
        lo = jnp.where(live, jnp.where(take_top, vmax, vmin), lo)
        clo = jnp.where(take_top, c2, clo)
        hi = jnp.where(drop_top, vmax, hi)
        chi = jnp.where(drop_top, c2, chi)
        done = jnp.where(live & (single | top_ok), 1.0, done)
        return n_active(done), lo, hi, clo, chi, done

    state = lax.fori_loop(0, BISECT_STEPS, lambda _, s: bisect(*s), (lo0, rmax, clo0, c_max, done0))
    state = lax.while_loop(lambda s: s[0] > 0, body2, (n_active(state[4]),) + state)
    _, lo, _, clo, _, _ = state

    mask_ref[0:keys, :] = jnp.where(causal & (x_ref[0:keys, :] >= lo), 0.0, -1e30)

    tie_q = jnp.sum(jnp.where(need & (clo > kf), 1.0, 0.0)).astype(jnp.int32)

    @pl.when(tie_q > 0)
    def _():
        room = kf - _col_reduce(jnp.sum, jnp.where(x_ref[0:keys, :] > lo, 1.0, 0.0))
        step = 256
        rr = lax.broadcasted_iota(jnp.int32, (step, step), 0)
        cc = lax.broadcasted_iota(jnp.int32, (step, step), 1)
        before = jnp.where(cc < rr, 1.0, 0.0).astype(BF16)
        carry = jnp.zeros((1, tq), F32)
        for c0 in range(0, keys, step):
            xs = x_ref[c0:c0 + step, :]
            e = jnp.where(xs == lo, 1.0, 0.0)
            rank = carry + jnp.dot(before, e.astype(BF16), preferred_element_type=F32)
            keep = (xs > lo) | ((xs == lo) & (rank < room))
            mask_ref[c0:c0 + step, :] = jnp.where(keep & (xs > -jnp.inf), 0.0, -1e30)
            carry = carry + jnp.sum(e, axis=0, keepdims=True)

    bias = mask_ref[0:keys, :]
    q = q_ref[0]
    group = q.shape[1] // HEAD_DIM // KV_HEADS
    bias_g = jnp.concatenate([bias] * group, axis=1)
    for g in range(KV_HEADS):
        qg = jnp.concatenate([q[:, (g * group + j) * HEAD_DIM:(g * group + j + 1) * HEAD_DIM]
                              for j in range(group)], axis=0)
        kg = k_ref[0, 0:keys, g * HEAD_DIM:(g + 1) * HEAD_DIM]
        vtg = vt_ref[0, g * HEAD_DIM:(g + 1) * HEAD_DIM, 0:keys]
        s = lax.dot_general(kg, qg, (((1,), (1,)), ((), ())), preferred_element_type=F32) + bias_g
        m = _col_reduce(jnp.max, s)
        p = jnp.exp2(s - m)
        den = _col_reduce(jnp.sum, p)
        og = jnp.dot(vtg, p.astype(BF16), preferred_element_type=F32) / den
        for j in range(group):
            h = g * group + j
            o_ref[0, :, h * HEAD_DIM:(h + 1) * HEAD_DIM] = og[:, j * tq:(j + 1) * tq].T.astype(BF16)


def _attention(q, k, vt, iq, ik, iwt, top_k):
    bsz, seq, aw = q.shape
    tq = min(ATTN_Q_TILE, seq)
    step = min(ATTN_KEY_STEP, seq)
    outs = []
    for q0 in range(0, seq, step):
        keys = q0 + step
        nq = step // tq
        qtile = lambda n, q0=q0: pl.BlockSpec((1, tq, n), lambda b, i: (b, q0 // tq + i, 0))
        ktile = lambda n, keys=keys: pl.BlockSpec((1, keys, n), lambda b, i: (b, 0, 0))
        outs.append(pl.pallas_call(
            functools.partial(_attn_kernel, q0=q0, tq=tq, keys=keys, top_k=top_k, nq=nq),
            grid=(bsz, nq),
            in_specs=[qtile(aw), qtile(iq.shape[2]),
                      pl.BlockSpec((1, iwt.shape[1], tq), lambda b, i, q0=q0: (b, 0, q0 // tq + i)),
                      ktile(k.shape[2]),
                      pl.BlockSpec((1, vt.shape[1], keys), lambda b, i: (b, 0, 0)),
                      ktile(ik.shape[2])],
            out_specs=pl.BlockSpec((1, tq, aw), lambda b, i: (b, i, 0)),
            out_shape=jax.ShapeDtypeStruct((bsz, step, aw), BF16),
            scratch_shapes=[pltpu.VMEM((keys, tq), F32), pltpu.VMEM((keys, tq), F32)],
            compiler_params=_params("parallel", "arbitrary"),
            name=f"attn_k{keys}",
        )(q, iq, iwt, k, vt, ik))
    return outs


def _ssm_kernel(u_ref, kin_ref, wre_ref, wim_ref, vre_ref, vim_ref, are_ref, aim_ref, y_ref,
                sre_ref, sim_ref, xre_ref, xim_ref, *, chunk):
    bsz, seq, lanes = u_ref.shape
    nc = seq // chunk
    rows = bsz * nc
    ns = wre_ref.shape[2]
    u = u_ref[...].reshape(rows, chunk, lanes).reshape(rows, chunk * lanes).astype(BF16)
    sre_ref[...] = jnp.dot(u, wre_ref[0], preferred_element_type=F32).reshape(bsz, nc, ns)
    sim_ref[...] = jnp.dot(u, wim_ref[0], preferred_element_type=F32).reshape(bsz, nc, ns)
    a_re = are_ref[0]
    a_im = aim_ref[0]
    x_re = jnp.zeros((bsz, ns), F32)
    x_im = jnp.zeros((bsz, ns), F32)
    for c in range(nc):
        xre_ref[:, c, :] = x_re
        xim_ref[:, c, :] = x_im
        n_re = a_re * x_re - a_im * x_im + sre_ref[:, c, :]
        n_im = a_re * x_im + a_im * x_re + sim_ref[:, c, :]
        x_re, x_im = n_re, n_im
    xr = xre_ref[...].reshape(rows, ns).astype(BF16)
    xi = xim_ref[...].reshape(rows, ns).astype(BF16)
    y = (jnp.dot(u, kin_ref[0], preferred_element_type=F32)
         + jnp.dot(xr, vre_ref[0], preferred_element_type=F32)
         + jnp.dot(xi, vim_ref[0], preferred_element_type=F32))
    y_ref[...] = y.reshape(rows, chunk, lanes).reshape(bsz, seq, lanes)


def _ssm_placement(gs, ch, p):
    a = np.arange(gs)[:, None, None]
    lane = np.arange(gs * ch)[None, None, :]
    on_lane = ((lane // ch) == a) & ((lane % ch) == np.arange(ch)[None, :, None])
    q = np.arange(gs * p)[None, None, :]
    on_state = ((q // p) == a) & ((q % p) == np.arange(p)[None, :, None])
    return jnp.asarray(on_lane, F32), jnp.asarray(on_state, F32)


def _ssm_prepare(lam_re, lam_im, log_step, b_re, b_im, c_re, c_im, d_skip, chunk):
    g, p = lam_re.shape
    ch = b_re.shape[2]
    hp = lax.Precision.HIGHEST
    step = jnp.exp(log_step)[:, None]
    lsr, lsi = lam_re * step, lam_im * step
    er = jnp.exp(lsr)
    nr, ni = er * jnp.cos(lsi) - 1.0, er * jnp.sin(lsi)
    den = lam_re * lam_re + lam_im * lam_im
    fr, fi = (nr * lam_re + ni * lam_im) / den, (ni * lam_re - nr * lam_im) / den
    bbr = fr[..., None] * b_re - fi[..., None] * b_im
    bbi = fr[..., None] * b_im + fi[..., None] * b_re
    n = jnp.arange(chunk + 1, dtype=F32)
    mag = jnp.exp(lsr[..., None] * n)
    pr, pi = mag * jnp.cos(lsi[..., None] * n), mag * jnp.sin(lsi[..., None] * n)
    prt, pit = pr[:, :, :chunk, None], pi[:, :, :chunk, None]
    wr = prt * bbr[:, :, None, :] - pit * bbi[:, :, None, :]
    wi = prt * bbi[:, :, None, :] + pit * bbr[:, :, None, :]
    ker = (jnp.einsum("gcp,gptd->gtcd", c_re, wr, precision=hp)
           - jnp.einsum("gcp,gptd->gtcd", c_im, wi, precision=hp))
    ker = ker.at[:, 0].add(jnp.eye(ch, dtype=F32) * d_skip.reshape(g, 1, ch))
    pr1, pi1 = pr[:, None, :, 1:], pi[:, None, :, 1:]
    vr = c_re[..., None] * pr1 - c_im[..., None] * pi1
    vi = c_re[..., None] * pi1 + c_im[..., None] * pr1
    gs = LANES // ch
    ns = g // gs
    lanes = chunk * LANES
    on_lane, on_state = _ssm_placement(gs, ch, p)
    bf16_exact = lambda m: m.astype(BF16).astype(F32)
    by_tile = lambda m: m.reshape(ns, gs, *m.shape[1:])
    lag_tile = jnp.einsum("satcd,acl->stadl", by_tile(bf16_exact(ker)), on_lane).reshape(ns, chunk, LANES, LANES)
    lag = jnp.arange(chunk)[None, :] - jnp.arange(chunk)[:, None]
    tiles = jnp.where((lag >= 0)[None, :, :, None, None], lag_tile[:, jnp.clip(lag, 0, chunk - 1)], 0.0)
    kin8 = tiles.transpose(0, 1, 3, 2, 4).reshape(ns, lanes, lanes).astype(BF16)

    def to_in(w):
        wide = jnp.einsum("sapid,apq->siadq", by_tile(bf16_exact(w[:, :, ::-1, :])), on_state)
        return wide.reshape(ns, lanes, gs * p).astype(BF16)

    def to_out(v):
        wide = jnp.einsum("sacpj,acl->sapjl", by_tile(bf16_exact(v)), on_lane)
        return wide.reshape(ns, gs * p, lanes).astype(BF16)

    a_re = pr[..., chunk].reshape(ns, 1, gs * p)
    a_im = pi[..., chunk].reshape(ns, 1, gs * p)
    return kin8, to_in(wr), to_in(wi), to_out(vr), to_out(-vi), a_re, a_im


def _ssm(us, prep, chunk):
    bsz, seq, w = us.shape
    kin, wre, wim, vre, vim, a_re, a_im = prep
    ns, lanes, states = wre.shape
    nc = seq // chunk
    act = pl.BlockSpec((bsz, seq, LANES), lambda i: (0, 0, i))
    mat = lambda a, b: pl.BlockSpec((1, a, b), lambda i: (i, 0, 0))
    return pl.pallas_call(
        functools.partial(_ssm_kernel, chunk=chunk),
        grid=(ns,),
        in_specs=[act, mat(lanes, lanes), mat(lanes, states), mat(lanes, states), mat(states, lanes),
                  mat(states, lanes), mat(1, states), mat(1, states)],
        out_specs=act,
        out_shape=jax.ShapeDtypeStruct((bsz, seq, w), F32),
        scratch_shapes=[pltpu.VMEM((bsz, nc, states), F32)] * 4,
        compiler_params=_params("parallel"),
        name="ssm",
    )(us, kin, wre, wim, vre, vim, a_re, a_im)


def _row_tile(d):
    return (ROW_SUBLANES, d // ROW_SUBLANES)


def _to_rows(v):
    return v.reshape((v.shape[0],) + _row_tile(v.shape[1]))


def _from_rows(v):
    return v.reshape(v.shape[0], v.shape[1] * v.shape[2])


def _layer_norm(h, g, b):
    mu = jnp.mean(h, axis=1, keepdims=True)
    hc = h - mu
    var = jnp.mean(hc * hc, axis=1, keepdims=True)
    return hc * lax.rsqrt(var + LN_EPS) * g + b


def _route(logits_t):
    ng, per = N_EXPERT_GROUPS, EXPERTS_PER_GROUP
    m = jnp.max(logits_t, axis=0, keepdims=True)
    e = jnp.exp(logits_t - m)
    prob = e / jnp.sum(e, axis=0, keepdims=True)
    v = [prob[j * ng:(j + 1) * ng, :] for j in range(per)]

    def top(vals):
        best = functools.reduce(jnp.maximum, vals)
        idx = jnp.full(best.shape, per - 1, jnp.int32)
        for j in range(per - 2, -1, -1):
            idx = jnp.where(vals[j] == best, j, idx)
        return best, idx

    m1, i1 = top(v)
    m2, i2 = top([jnp.where(i1 == j, -1.0, v[j]) for j in range(per)])
    score = m1 + m2
    gid = lax.broadcasted_iota(jnp.int32, score.shape, 0)
    best_g = jnp.min(jnp.where(score == jnp.max(score, axis=0, keepdims=True), gid, ng), axis=0, keepdims=True)
    sel = gid == best_g
    pick_f = lambda a: jnp.sum(jnp.where(sel, a, 0.0), axis=0, keepdims=True)
    pick_i = lambda a: jnp.sum(jnp.where(sel, a, 0), axis=0, keepdims=True)
    p1, p2 = pick_f(m1), pick_f(m2)
    e1, e2 = best_g * per + pick_i(i1), best_g * per + pick_i(i2)
    den = p1 + p2
    return jnp.concatenate([e1, e2], axis=0), jnp.concatenate([p1 / den, p2 / den], axis=0)


def _mix_kernel(x_ref, ys_ref, upc_ref, upp_ref, g1_ref, sc2_ref, sh2_ref, lng_ref, lnb_ref,
                wout_ref, wglu_ref, bglu_ref, wpool_ref, pscale_ref, wr_ref, *rest, alpha, tiles_per_variant):
    ya_refs, (x1_ref, u2_ref, ids_ref, gates_ref) = rest[:-4], rest[-4:]
    i = pl.program_id(1)
    tm = x_ref.shape[1]
    owner = i // tiles_per_variant
    ya = ya_refs[0][0]
    for k in range(1, len(ya_refs)):
        ya = jnp.where(owner == k, ya_refs[k][0], ya)
    y = ys_ref[0]
    y = 0.5 * y * (1.0 + jnp.tanh(math.sqrt(2.0 / math.pi) * (y + 0.044715 * (y * y * y))))
    z = jnp.dot(y.astype(BF16), wglu_ref[...], preferred_element_type=F32) + bglu_ref[...]
    y = y * _sigmoid(z)
    upc = upc_ref[0]
    upp = jnp.where(i > 0, upp_ref[0], 0.0)
    cat = jnp.concatenate([upp, upc], axis=0).astype(BF16)
    r = lax.broadcasted_iota(jnp.int32, (tm, 1), 0)
    lagm = (r + tm) - lax.broadcasted_iota(jnp.int32, (1, 2 * tm), 1)
    tpos = (i * tm + r + 1).astype(F32)
    gc = upc.shape[1] // len(POOL_WINDOWS)
    pooled = []
    for g, win in enumerate(POOL_WINDOWS):
        band = jnp.where((lagm >= 0) & (lagm < win), 1.0, 0.0).astype(BF16)
        ws = jnp.dot(band, cat[:, g * gc:(g + 1) * gc], preferred_element_type=F32)
        pg = ws / jnp.minimum(tpos, float(win)) - upc[:, g * gc:(g + 1) * gc]
        pooled.append(jnp.dot(pg.astype(BF16), wpool_ref[g], preferred_element_type=F32))
    yp = jnp.concatenate(pooled, axis=1) * pscale_ref[...]
    mixed = jnp.concatenate([ya, y.astype(BF16), yp.astype(BF16)], axis=1)
    mix = jnp.dot(mixed, wout_ref[...], preferred_element_type=F32)
    x1 = _layer_norm(alpha * x_ref[0] + (1.0 + g1_ref[0]) * mix, lng_ref[...], lnb_ref[...])
    x1_ref[0] = x1
    u2 = x1 * (1.0 + sc2_ref[0]) + sh2_ref[0]
    u2_ref[0] = _to_rows(u2.astype(BF16))
    u_hi = u2.astype(BF16)
    u_lo = (u2 - u_hi.astype(F32)).astype(BF16)
    logits = jnp.dot(jnp.concatenate([u_hi, u_hi, u_lo], axis=1), wr_ref[...], preferred_element_type=F32)
    ids, gates = _route(logits.T[0:N_EXPERTS, :])
    ids_ref[0] = ids
    gates_ref[0] = gates


def _mix(x, yas, ys, up, g1, sc2, sh2, lng, lnb, wout, wglu, bglu, wpool, pscale, wr_pad, alpha):
    bsz, seq, d = x.shape
    tm = min(ROW_TILE, seq)
    per = yas[0].shape[1] // tm
    att = [pl.BlockSpec((1, tm, ya.shape[2]), lambda b, i, k=k: (b, jnp.clip(i - k * per, 0, per - 1), 0))
           for k, ya in enumerate(yas)]
    tile = lambda n: pl.BlockSpec((1, tm, n), lambda b, i: (b, i, 0))
    prev = pl.BlockSpec((1, tm, up.shape[2]), lambda b, i: (b, jnp.maximum(i - 1, 0), 0))
    vec = pl.BlockSpec((1, 1, d), lambda b, i: (b, 0, 0))
    full = lambda a: pl.BlockSpec(a.shape, lambda b, i: (0,) * a.ndim)
    lane_rows = pl.BlockSpec((1, 2, tm), lambda b, i: (b, 0, i))
    return pl.pallas_call(
        functools.partial(_mix_kernel, alpha=alpha, tiles_per_variant=per),
        grid=(bsz, seq // tm),
        in_specs=[tile(d), tile(ys.shape[2]), tile(up.shape[2]), prev, vec, vec, vec,
                  full(lng), full(lnb), full(wout), full(wglu), full(bglu), full(wpool), full(pscale), full(wr_pad)]
        + att,
        out_specs=[tile(d), pl.BlockSpec((1, tm) + _row_tile(d), lambda b, i: (b, i, 0, 0)), lane_rows, lane_rows],
        out_shape=[jax.ShapeDtypeStruct((bsz, seq, d), F32), jax.ShapeDtypeStruct((bsz, seq) + _row_tile(d), BF16),
                   jax.ShapeDtypeStruct((bsz, 2, seq), jnp.int32), jax.ShapeDtypeStruct((bsz, 2, seq), F32)],
        compiler_params=_params("parallel", "arbitrary"),
        name="mix",
    )(x, ys, up, up, g1, sc2, sh2, lng, lnb, wout, wglu, bglu, wpool, pscale, wr_pad, *yas)


def _gather_kernel(idx_ref, src_ref, *rest, lo, n_src, partial_src, has_prev):
    if has_prev:
        prev_ref, out_ref, buf, sem = rest
    else:
        out_ref, buf, sem = rest

    @pl.when(pl.program_id(0) == 0)
    def _():
        cp = pltpu.make_async_copy(src_ref.at[pl.ds(lo, n_src)], buf, sem)
        cp.start()
        cp.wait()

    base = pl.program_id(0) * COPY_ROWS

    def move(r, carry):
        s = idx_ref[base + r]
        row = buf[jnp.maximum(s, 0)] if partial_src else buf[s]
        if has_prev:
            row = jnp.where(s >= 0, row, prev_ref[r])
        out_ref[r] = row
        return carry

    lax.fori_loop(0, COPY_ROWS, move, 0, unroll=8)


def _slab_plan(src, idx):
    n_chunks = pl.cdiv(src.shape[0], GATHER_SRC_ROWS)
    n_src = src.shape[0] // n_chunks
    assert n_src * n_chunks == src.shape[0]
    if n_chunks == 1:
        return n_src, [(0, idx)]
    local = [idx - c * n_src for c in range(n_chunks)]
    return n_src, [(c * n_src, jnp.where((loc >= 0) & (loc < n_src), loc, -1)) for c, loc in enumerate(local)]


def _gather_rows(src, idx, keep_last_slab=False):
    n = idx.shape[0]
    n_src, plan = _slab_plan(src, idx)
    n_chunks = len(plan)
    block = pl.BlockSpec((COPY_ROWS,) + src.shape[1:], lambda i, idx: (i, 0, 0))
    out = None
    for c, (lo, local) in enumerate(plan[:-1] if keep_last_slab else plan):
        has_prev = c > 0
        out = pl.pallas_call(
            functools.partial(_gather_kernel, lo=lo, n_src=n_src, partial_src=n_chunks > 1, has_prev=has_prev),
            grid_spec=pltpu.PrefetchScalarGridSpec(
                num_scalar_prefetch=1, grid=(n // COPY_ROWS,),
                in_specs=[pl.BlockSpec(memory_space=pl.ANY)] + ([block] if has_prev else []),
                out_specs=block,
                scratch_shapes=[pltpu.VMEM((n_src,) + src.shape[1:], src.dtype), pltpu.SemaphoreType.DMA(())]),
            out_shape=jax.ShapeDtypeStruct((n,) + src.shape[1:], src.dtype),
            compiler_params=_params("arbitrary"),
            name="gather",
        )(*((local, src, out) if has_prev else (local, src)))
    return (out, n_src, plan[-1]) if keep_last_slab else out


def _expert_kernel(te_ref, act_ref, first_ref, nxt_ref, x_ref, wg_hbm, wu_hbm, wd_hbm, y_ref,
                   stage_g, stage_u, stage_d, wg_ref, wu_ref, wd_ref, sems, *, layer):
    i = pl.program_id(0)

    def fetch(e):
        return (pltpu.make_async_copy(wg_hbm.at[layer, e], stage_g, sems.at[0]),
                pltpu.make_async_copy(wu_hbm.at[layer, e], stage_u, sems.at[1]),
                pltpu.make_async_copy(wd_hbm.at[layer, e], stage_d, sems.at[2]))

    @pl.when(i == 0)
    def _():
        for cp in fetch(te_ref[0]):
            cp.start()

    @pl.when(first_ref[i] != 0)
    def _():
        for cp in fetch(te_ref[i]):
            cp.wait()
        for stage, work in ((stage_g, wg_ref), (stage_u, wu_ref), (stage_d, wd_ref)):
            rows = stage.shape[0]

            def convert(r, carry, stage=stage, work=work):
                sl = pl.ds(pl.multiple_of(r * CONVERT_ROWS, CONVERT_ROWS), CONVERT_ROWS)
                work[sl, :] = stage[sl, :].astype(BF16)
                return carry

            lax.fori_loop(0, rows // CONVERT_ROWS, convert, 0)

        @pl.when(nxt_ref[i] >= 0)
        def _():
            for cp in fetch(nxt_ref[i]):
                cp.start()

    @pl.when(act_ref[i] != 0)
    def _():
        x = _from_rows(x_ref[...])
        g = jnp.dot(x, wg_ref[...], preferred_element_type=F32)
        u = jnp.dot(x, wu_ref[...], preferred_element_type=F32)
        h = (g * _sigmoid(g)) * u
        y_ref[...] = _to_rows(jnp.dot(h.astype(BF16), wd_ref[...], preferred_element_type=F32).astype(BF16))

    @pl.when(act_ref[i] == 0)
    def _():
        y_ref[...] = jnp.zeros(y_ref.shape, y_ref.dtype)


def _experts(xg, plan, wg, wu, wd, layer):
    rows = xg.shape[0]
    d, f = wg.shape[2], wg.shape[3]
    tm = EXPERT_TILE
    tile = pl.BlockSpec((tm,) + _row_tile(d), lambda i, *_: (i, 0, 0))
    hbm = pl.BlockSpec(memory_space=pl.ANY)
    return pl.pallas_call(
        functools.partial(_expert_kernel, layer=layer),
        grid_spec=pltpu.PrefetchScalarGridSpec(
            num_scalar_prefetch=4, grid=(rows // tm,),
            in_specs=[tile, hbm, hbm, hbm],
            out_specs=tile,
            scratch_shapes=[pltpu.VMEM((d, f), F32), pltpu.VMEM((d, f), F32), pltpu.VMEM((f, d), F32),
                            pltpu.VMEM((d, f), BF16), pltpu.VMEM((d, f), BF16), pltpu.VMEM((f, d), BF16),
                            pltpu.SemaphoreType.DMA((3,))]),
        out_shape=jax.ShapeDtypeStruct((rows,) + _row_tile(d), BF16),
        compiler_params=_params("arbitrary"),
        name="experts",
    )(*plan, xg, wg, wu, wd)


def _combine_kernel(idx_ref, src_ref, *rest, lo, n_src, n_tok, partial_src, has_prev, alpha):
    if has_prev:
        prev_ref, x_ref, gc_ref, g2_ref, lng_ref, lnb_ref, o_ref, buf, rows, sem = rest
    else:
        x_ref, gc_ref, g2_ref, lng_ref, lnb_ref, o_ref, buf, rows, sem = rest

    @pl.when(pl.program_id(0) == 0)
    def _():
        cp = pltpu.make_async_copy(src_ref.at[pl.ds(lo, n_src)], buf, sem)
        cp.start()
        cp.wait()

    tm = x_ref.shape[0]
    base = pl.program_id(0) * tm
    for k in range(2):
        def move(r, carry, k=k):
            s = idx_ref[k * n_tok + base + r]
            row = buf[jnp.maximum(s, 0)] if partial_src else buf[s]
            if has_prev:
                row = jnp.where(s >= 0, row, prev_ref[k, r])
            rows[k, r] = row
            return carry

        lax.fori_loop(0, tm, move, 0, unroll=8)
    gc = gc_ref[...]
    ffn = gc[:, 0:1] * _from_rows(rows[0]).astype(F32) + gc[:, 1:2] * _from_rows(rows[1]).astype(F32)
    o_ref[...] = _layer_norm(alpha * x_ref[...] + (1.0 + g2_ref[0]) * ffn, lng_ref[...], lnb_ref[...])


def _combine(yg, pos, x1, gates_col, g2, lng, lnb, alpha):
    bsz, seq, d = x1.shape
    n_tok = bsz * seq
    tm = COMBINE_TILE
    per_batch = seq // tm
    prev, n_src, (lo, local) = _gather_rows(yg, pos, keep_last_slab=True)
    has_prev = prev is not None
    row = _row_tile(d)
    tile = pl.BlockSpec((tm, d), lambda i, idx: (i, 0))
    in_specs = [pl.BlockSpec(memory_space=pl.ANY)]
    args = [local, yg]
    if has_prev:
        in_specs.append(pl.BlockSpec((2, tm) + row, lambda i, idx: (0, i, 0, 0)))
        args.append(prev.reshape((2, n_tok) + row))
    in_specs += [tile, pl.BlockSpec((tm, 2), lambda i, idx: (i, 0)),
                 pl.BlockSpec((1, 1, d), lambda i, idx: (i // per_batch, 0, 0)),
                 pl.BlockSpec(lng.shape, lambda i, idx: (0, 0)), pl.BlockSpec(lnb.shape, lambda i, idx: (0, 0))]
    args += [x1.reshape(n_tok, d), gates_col.reshape(n_tok, 2), g2, lng, lnb]
    out = pl.pallas_call(
        functools.partial(_combine_kernel, lo=lo, n_src=n_src, n_tok=n_tok, partial_src=has_prev,
                          has_prev=has_prev, alpha=alpha),
        grid_spec=pltpu.PrefetchScalarGridSpec(
            num_scalar_prefetch=1, grid=(n_tok // tm,),
            in_specs=in_specs,
            out_specs=tile,
            scratch_shapes=[pltpu.VMEM((n_src,) + row, yg.dtype), pltpu.VMEM((2, tm) + row, yg.dtype),
                            pltpu.SemaphoreType.DMA(())]),
        out_shape=jax.ShapeDtypeStruct((n_tok, d), F32),
        compiler_params=_params("arbitrary"),
        name="combine",
    )(*args)
    return out.reshape(bsz, seq, d)


def _dispatch_plan(ids, n_tok):
    tm = EXPERT_TILE
    e_pair = jnp.concatenate([ids[:, 0, :].reshape(n_tok), ids[:, 1, :].reshape(n_tok)])
    tok_pair = jnp.concatenate([jnp.arange(n_tok, dtype=jnp.int32)] * 2)
    onehot = (e_pair[:, None] == jnp.arange(N_EXPERTS, dtype=jnp.int32)[None, :]).astype(jnp.int32)
    csum = jnp.cumsum(onehot, axis=0)
    rank = jnp.sum(onehot * csum, axis=1) - 1
    counts = csum[-1]
    padded = ((counts + tm - 1) // tm) * tm
    ends = jnp.cumsum(padded)
    offs = ends - padded
    pos = jnp.sum(onehot * offs[None, :], axis=1) + rank
    rows = 2 * n_tok + N_EXPERTS * tm
    src_token = jnp.zeros((rows,), jnp.int32).at[pos].set(tok_pair)
    tile_start = jnp.arange(rows // tm, dtype=jnp.int32) * tm
    tile_expert = jnp.minimum(jnp.sum((tile_start[:, None] >= ends[None, :]).astype(jnp.int32), axis=1),
                              N_EXPERTS - 1)
    tile_active = (tile_start < ends[-1]).astype(jnp.int32)
    prev_expert = jnp.concatenate([jnp.full((1,), -1, jnp.int32), tile_expert[:-1]])
    tile_first = tile_active * (tile_expert != prev_expert).astype(jnp.int32)
    eid = jnp.arange(N_EXPERTS, dtype=jnp.int32)
    later = (padded > 0)[None, :] & (eid[None, :] > eid[:, None])
    next_expert = jnp.min(jnp.where(later, eid[None, :], N_EXPERTS), axis=1)
    next_expert = jnp.where(next_expert == N_EXPERTS, -1, next_expert).astype(jnp.int32)
    tile_next = next_expert[tile_expert]
    return pos.astype(jnp.int32), src_token, (tile_expert, tile_active, tile_first, tile_next)


def _rope_tables(positions, dim):
    rot = dim // ROPE_FRAC
    half = rot // 2
    inv_freq = ROPE_THETA ** (-jnp.arange(half, dtype=F32) * 2.0 / rot)
    ang = positions.astype(F32)[..., None] * inv_freq
    cos, sin = jnp.cos(ang), jnp.sin(ang)
    shape = ang.shape[:-1]
    one = jnp.ones(shape + (dim - rot,), F32)
    zero = jnp.zeros(shape + (dim - rot,), F32)
    zh = jnp.zeros(shape + (half,), F32)
    c = jnp.concatenate([cos, cos, one], axis=-1)
    s1 = jnp.concatenate([-sin, zh, zero], axis=-1)
    s2 = jnp.concatenate([zh, sin, zero], axis=-1)
    rep = LANES // dim
    return jnp.concatenate([jnp.tile(a, (1, 1, rep)) for a in (c, s1, s2)], axis=-1)


def kernel(x, c, positions, w_ada, b_ada, w_in, w_out, ssm_lam_re, ssm_lam_im, ssm_log_step, ssm_b_re, ssm_b_im, ssm_c_re, ssm_c_im, ssm_d, ssm_w_glu, ssm_b_glu, pool_w, pool_scale, ln1_g, ln1_b, ln2_g, ln2_b, w_router, e_gate, e_up, e_down):
    bsz, seq, d = x.shape
    depth = w_ada.shape[0]
    n_tok = bsz * seq
    alpha = (2.0 * depth) ** 0.25
    sw = ssm_d.shape[1]
    pw = pool_scale.shape[1]
    aw = w_out.shape[1] - sw - pw
    kw = KV_HEADS * HEAD_DIM
    iqw = IDX_HEADS * IDX_DIM
    widths = (aw, kw, iqw, sw, pw)
    top_k = min(MAX_TOPK, seq // 4)

    c_pad = jnp.concatenate([c, jnp.zeros((8 - bsz % 8, d), F32)], axis=0) if bsz % 8 else c
    ada = _ada_all(c_pad, w_ada, b_ada)
    rq = _rope_tables(positions, HEAD_DIM)
    ri = _rope_tables(positions, IDX_DIM)
    wr_pad = jnp.concatenate(
        [w_router.reshape(d, N_EXPERT_GROUPS, EXPERTS_PER_GROUP).transpose(0, 2, 1).reshape(d, N_EXPERTS),
         jnp.zeros((d, LANES - N_EXPERTS), F32)], axis=1)
    wr_hi = wr_pad.astype(BF16)
    wr_lo = (wr_pad - wr_hi.astype(F32)).astype(BF16)
    wr_pad = jnp.concatenate([wr_hi, wr_lo, wr_hi], axis=0)

    o_q, o_k, o_v, o_iq, o_ik, o_iw, o_us, o_up = (0, aw, aw + kw, aw + 2 * kw, aw + 2 * kw + iqw,
                                                   aw + 2 * kw + iqw + IDX_DIM,
                                                   aw + 2 * kw + iqw + IDX_DIM + IDX_HEADS,
                                                   aw + 2 * kw + iqw + IDX_DIM + IDX_HEADS + sw)
    for l in range(depth):
        sh1, sc1, g1, sh2, sc2, g2 = [ada[l, :bsz, j * d:(j + 1) * d].reshape(bsz, 1, d) for j in range(6)]
        wl = w_in[l]
        w_r = jnp.concatenate([wl[:, o_q:o_v], wl[:, o_iq:o_ik], wl[:, o_us:], wl[:, o_ik:o_iw],
                               jnp.zeros((d, LANES - IDX_DIM), F32)], axis=1).astype(BF16)
        wvt = wl[:, o_v:o_iq].T.astype(BF16)
        wiwt = wl[:, o_iw:o_us].T.astype(BF16)
        q, k, vt, iq, ik, iwt, us, up = _inproj(x, sc1, sh1, w_r, wvt, wiwt, rq, ri, widths)
        ya = _attention(q, k, vt, iq, ik, iwt, top_k)
        prep = _ssm_prepare(ssm_lam_re[l], ssm_lam_im[l], ssm_log_step[l], ssm_b_re[l], ssm_b_im[l],
                            ssm_c_re[l], ssm_c_im[l], ssm_d[l], SSM_CHUNK)
        ys = _ssm(us, prep, SSM_CHUNK)
        x1, u2, ids, gates = _mix(x, ya, ys, up, g1, sc2, sh2, ln1_g[l][None], ln1_b[l][None],
                                  w_out[l].astype(BF16), ssm_w_glu[l].astype(BF16), ssm_b_glu[l][None],
                                  pool_w[l].astype(BF16), pool_scale[l][None], wr_pad, alpha)
        pos, src_token, tile_plan = _dispatch_plan(ids, n_tok)
        xg = _gather_rows(u2.reshape((n_tok,) + _row_tile(d)), src_token)
        yg = _experts(xg, tile_plan, e_gate, e_up, e_down, l)
        x = _combine(yg, pos, x1, gates.transpose(0, 2, 1), g2, ln2_g[l][None], ln2_b[l][None], alpha)
    return x
```
